```python
import math
import jax
import jax.numpy as jnp
from jax import lax
import numpy as np

D_MODEL = 1024
BATCH = 8
SEQ = 4096
DEPTH = 4

N_EVEN = (DEPTH + 1) // 2
N_ODD = DEPTH // 2
RMS_EPS = 1e-6

A_HEADS = 8
A_KV_HEADS = 2
A_HEAD_DIM = 64
A_WINDOW = 128
B_HEADS = 4
B_KEY_DIM = 64
B_VAL_DIM = 128
B_GATE_RANK = 16
B_GATE_TAU = 16.0
B_CHUNK = 64
C_HEADS = 8
C_HEAD_DIM = 128
C_BLOCK = 256
C_TOPK = 3
C_ROW_BLOCK = 128
REL_BUCKETS = 32
REL_MAX_DIST = 128
REL_HEADS = 8
MOE_GROUPS = 4
MOE_EXPERTS_PER_GROUP = 8
MOE_EXPERTS = MOE_GROUPS * MOE_EXPERTS_PER_GROUP
MOE_TOPK = 2
MOE_HIDDEN = 512
MOE_ROW_BLOCK = 128

EVEN_SPLITS = (A_HEADS * A_HEAD_DIM, A_KV_HEADS * A_HEAD_DIM, A_KV_HEADS * A_HEAD_DIM,
               B_HEADS * B_KEY_DIM, B_HEADS * B_KEY_DIM, B_HEADS * B_VAL_DIM,
               B_HEADS * B_VAL_DIM, B_GATE_RANK)
EVEN_IN = sum(EVEN_SPLITS)
EVEN_MIX = A_HEADS * A_HEAD_DIM + B_HEADS * B_VAL_DIM
ODD_MIX = C_HEADS * C_HEAD_DIM

kernel_name = 'hybrid_swa_gla_moba_hmoe_block'


def _rms(x, w):
    xf = x.astype(jnp.float32)
    y = xf * lax.rsqrt(jnp.mean(xf * xf, axis=-1, keepdims=True) + RMS_EPS)
    return (y * w.astype(jnp.float32)).astype(x.dtype)


def _rel_bucket(dist):
    exact = REL_BUCKETS // 2
    d = jnp.maximum(dist, 0)
    logd = jnp.log(jnp.maximum(d, 1).astype(jnp.float32) / exact) / math.log(REL_MAX_DIST / exact)
    far = jnp.minimum(exact + (logd * (REL_BUCKETS - exact)).astype(jnp.int32), REL_BUCKETS - 1)
    return jnp.where(d < exact, d, far)


def _group_rows(ids, n_groups, rows):
    n = ids.shape[0]
    n_rows = ((n + rows - 1) // rows) * rows + n_groups * rows
    counts = jnp.zeros((n_groups + 1,), jnp.int32).at[ids].add(1)[:n_groups]
    padded = ((counts + rows - 1) // rows) * rows
    start = jnp.cumsum(counts) - counts
    pend = jnp.cumsum(padded)
    pstart = pend - padded
    order = jnp.argsort(ids)
    sid = ids[order]
    gid = jnp.minimum(sid, n_groups - 1)
    dest = jnp.where(sid < n_groups, pstart[gid] + jnp.arange(n) - start[gid], n_rows)
    src = jnp.full((n_rows,), n, jnp.int32).at[dest].set(order.astype(jnp.int32), mode='drop')
    blk_group = jnp.searchsorted(pend, jnp.arange(n_rows // rows) * rows, side='right')
    return src, jnp.minimum(blk_group, n_groups - 1)


def _swa_sink_attention(q, k, v, sinks, rel_bias):
    B, S, _, hd = q.shape
    W = A_WINDOW
    nb = S // W
    G = A_HEADS // A_KV_HEADS
    f32 = jnp.float32
    qb = q.reshape(B, nb, W, A_KV_HEADS, G, hd)

    def band(t):
        tb = t.reshape(B, nb, W, A_KV_HEADS, hd)
        prev = jnp.pad(tb, ((0, 0), (1, 0), (0, 0), (0, 0), (0, 0)))[:, :-1]
        return jnp.concatenate([prev, tb], axis=2)

    kband, vband = band(k), band(v)
    lg = jnp.einsum('bnqkgd,bnskd->bkgnqs', qb, kband, preferred_element_type=f32) * hd ** -0.5
    dist = jnp.arange(W)[:, None] + W - jnp.arange(2 * W)[None, :]
    mask = ((dist >= 0) & (dist < W)
            & ((jnp.arange(nb)[:, None, None] > 0) | (jnp.arange(2 * W) >= W)[None, None, :]))
    bias = rel_bias[_rel_bucket(dist)].transpose(2, 0, 1).reshape(A_KV_HEADS, G, 1, W, 2 * W)
    lg = jnp.where(mask, lg + bias, -jnp.inf)
    sink = sinks.astype(f32).reshape(A_KV_HEADS, G, 1, 1, 1)
    m = jnp.maximum(lg.max(-1, keepdims=True), sink)
    p = jnp.exp(lg - m)
    probs = p / (p.sum(-1, keepdims=True) + jnp.exp(sink - m))
    o = jnp.einsum('bkgnqs,bnskd->bnqkgd', probs.astype(v.dtype), vband, preferred_element_type=f32)
    return o.reshape(B, S, A_HEADS * hd)


def _gla(q, k, v, log_a):
    B, S, H, dk = q.shape
    dv = v.shape[-1]
    C = B_CHUNK
    n = S // C

    def chunks(t):
        return t.reshape(B, n, C, H, t.shape[-1]).transpose(0, 3, 1, 2, 4)

    q, k, v, g = chunks(q * dk ** -0.5), chunks(k), chunks(v), chunks(log_a)
    b = jnp.cumsum(g, axis=3)
    b_last = b[:, :, :, -1:]
    q_dec = q * jnp.exp(b)
    att = jnp.einsum('bhnid,bhnjd->bhnij', q_dec, k * jnp.exp(-b))
    att = jnp.where(jnp.tril(jnp.ones((C, C), bool)), att, 0.0)
    o_intra = jnp.einsum('bhnij,bhnjv->bhniv', att, v)
    upd = jnp.einsum('bhnjd,bhnjv->bhndv', k * jnp.exp(b_last - b), v)
    decay = jnp.exp(b_last[:, :, :, 0])

    def step(state, inp):
        dec, u = inp
        return state * dec[..., None] + u, state

    _, s_in = lax.scan(step, jnp.zeros((B, H, dk, dv), q.dtype),
                       (decay.transpose(2, 0, 1, 3), upd.transpose(2, 0, 1, 3, 4)))
    o_inter = jnp.einsum('bhnid,bhndv->bhniv', q_dec, s_in.transpose(1, 2, 0, 3, 4))
    return (o_intra + o_inter).transpose(0, 2, 3, 1, 4).reshape(B, S, H, dv)


def _moba(q, k, v, rel_bias):
    B, S, H, hd = q.shape
    L = C_BLOCK
    nb = -(-S // L)
    Sp = nb * L
    k_sel = min(C_TOPK, nb)
    scale = hd ** -0.5
    f32 = jnp.float32
    q, k, v = [jnp.pad(t, ((0, 0), (0, Sp - S), (0, 0), (0, 0))).transpose(0, 2, 1, 3) for t in (q, k, v)]
    qb = q.reshape(B, H, nb, L, hd)
    kb = k.reshape(B, H, nb, L, hd)
    vb = v.reshape(B, H, nb, L, hd)
    li = jnp.arange(L)
    d_own = li[:, None] - li[None, :]
    bias_own = rel_bias[_rel_bucket(d_own)].transpose(2, 0, 1)[None, :, None]
    lg = jnp.einsum('bhnqd,bhnsd->bhnqs', qb, kb, preferred_element_type=f32) * scale + bias_own
    lg = jnp.where(d_own >= 0, lg, -jnp.inf)
    lse_own = jax.nn.logsumexp(lg, axis=-1)
    o_own = jnp.einsum('bhnqs,bhnsd->bhnqd', jnp.exp(lg - lse_own[..., None]).astype(v.dtype), vb,
                       preferred_element_type=f32)
    pos = jnp.arange(Sp)
    qblk = pos // L
    score = jnp.einsum('bhtd,bhjd->bhtj', q.astype(f32), kb.astype(f32).mean(axis=3))
    score = jnp.where(jnp.arange(nb)[None, :] < qblk[:, None], score, -jnp.inf)
    _, sel = lax.top_k(score, k_sel)
    valid = (jnp.arange(k_sel)[None, :] < jnp.minimum(qblk, k_sel)[:, None]) & (pos < S)[:, None]
    n_groups = B * H * nb
    grp = jnp.arange(B * H).reshape(B, H, 1, 1) * nb + sel
    grp = jnp.where(valid, grp, n_groups).reshape(-1)
    src, blk_grp = _group_rows(grp, n_groups, C_ROW_BLOCK)
    n_pairs = grp.shape[0]
    row_ok = src < n_pairs
    row_tok = jnp.minimum(src, n_pairs - 1) // k_sel
    n_tok = B * H * Sp
    q_rows = q.reshape(n_tok, hd)
    k_grp = kb.reshape(n_groups, L, hd)
    v_grp = vb.reshape(n_groups, L, hd)

    def attend_block(args):
        tok, g = args
        qr = q_rows[tok]
        dist = (tok % Sp)[:, None] - ((g % nb) * L + li)[None, :]
        bias = rel_bias[_rel_bucket(dist), (g // nb) % H]
        lg_r = jnp.einsum('rd,sd->rs', qr, k_grp[g], preferred_element_type=f32) * scale + bias
        lse_r = jax.nn.logsumexp(lg_r, axis=-1)
        o_r = jnp.einsum('rs,sd->rd', jnp.exp(lg_r - lse_r[:, None]).astype(v.dtype), v_grp[g],
                         preferred_element_type=f32)
        return o_r, lse_r

    o_sel, lse_sel = lax.map(attend_block, (row_tok.reshape(-1, C_ROW_BLOCK), blk_grp))
    o_sel = o_sel.reshape(-1, hd)
    lse_sel = lse_sel.reshape(-1)
    seg = jnp.where(row_ok, row_tok, n_tok)
    lse_own = lse_own.reshape(n_tok)
    m_sel = jax.ops.segment_max(jnp.where(row_ok, lse_sel, -jnp.inf), seg, num_segments=n_tok + 1)[:n_tok]
    m = jnp.maximum(lse_own, m_sel)
    w_sel = jnp.where(row_ok, jnp.exp(lse_sel - m[jnp.minimum(seg, n_tok - 1)]), 0.0)
    w_own = jnp.exp(lse_own - m)
    num = (w_own[:, None] * o_own.reshape(n_tok, hd)
           + jax.ops.segment_sum(w_sel[:, None] * o_sel, seg, num_segments=n_tok + 1)[:n_tok])
    den = w_own + jax.ops.segment_sum(w_sel, seg, num_segments=n_tok + 1)[:n_tok]
    o = (num / den[:, None]).reshape(B, H, Sp, hd)[:, :, :S]
    return o.transpose(0, 2, 1, 3).reshape(B, S, H * hd).astype(q.dtype)


def _even_mixer(h, w_in, w_out, q_norm, k_norm, sinks, gate_up, gate_bias, out_norm, rel_bias):
    B, S, _ = h.shape
    f32 = jnp.float32
    qa, ka, va, qb, kb, vb, rb, ab = jnp.split(h @ w_in, np.cumsum(EVEN_SPLITS)[:-1].tolist(), axis=-1)
    qa = _rms(qa.reshape(B, S, A_HEADS, A_HEAD_DIM), q_norm)
    ka = _rms(ka.reshape(B, S, A_KV_HEADS, A_HEAD_DIM), k_norm)
    va = va.reshape(B, S, A_KV_HEADS, A_HEAD_DIM)
    oa = _swa_sink_attention(qa, ka, va, sinks, rel_bias).astype(h.dtype)
    log_a = jax.nn.log_sigmoid((ab @ gate_up + gate_bias).astype(f32)) / B_GATE_TAU
    ob = _gla(qb.reshape(B, S, B_HEADS, B_KEY_DIM).astype(f32),
              kb.reshape(B, S, B_HEADS, B_KEY_DIM).astype(f32),
              vb.reshape(B, S, B_HEADS, B_VAL_DIM).astype(f32),
              log_a.reshape(B, S, B_HEADS, B_KEY_DIM))
    ob = _rms(ob, out_norm).astype(h.dtype) * jax.nn.silu(rb).reshape(B, S, B_HEADS, B_VAL_DIM)
    return jnp.concatenate([oa, ob.reshape(B, S, B_HEADS * B_VAL_DIM)], axis=-1) @ w_out


def _odd_mixer(h, w_in, w_out, q_norm, k_norm, rel_bias):
    B, S, _ = h.shape
    q, k, v = jnp.split(h @ w_in, 3, axis=-1)
    q = _rms(q.reshape(B, S, C_HEADS, C_HEAD_DIM), q_norm)
    k = _rms(k.reshape(B, S, C_HEADS, C_HEAD_DIM), k_norm)
    v = v.reshape(B, S, C_HEADS, C_HEAD_DIM)
    return _moba(q, k, v, rel_bias) @ w_out


def _hier_moe(h, w_group, b_group, w_expert, b_expert, w1, w3, w2):
    B, S, D = h.shape
    N = B * S
    f32 = jnp.float32
    xt = h.reshape(N, D)
    g_logits = (xt @ w_group).astype(f32) + b_group.astype(f32)
    g_sel = jnp.argmax(g_logits, axis=-1)
    p_g = jnp.take_along_axis(jax.nn.softmax(g_logits, axis=-1), g_sel[:, None], axis=-1)
    e_logits = ((xt @ w_expert).astype(f32) + b_expert.astype(f32)).reshape(N, MOE_GROUPS, MOE_EXPERTS_PER_GROUP)
    e_in = jnp.take_along_axis(e_logits, g_sel[:, None, None], axis=1)[:, 0]
    top_v, top_i = lax.top_k(e_in, MOE_TOPK)
    gate = (p_g * jax.nn.softmax(top_v, axis=-1)).reshape(-1)
    ids = (g_sel[:, None] * MOE_EXPERTS_PER_GROUP + top_i).reshape(-1)
    src, blk_e = _group_rows(ids, MOE_EXPERTS, MOE_ROW_BLOCK)
    n_pairs = ids.shape[0]
    row_ok = src < n_pairs
    pair = jnp.minimum(src, n_pairs - 1)
    row_tok = pair // MOE_TOPK

    def expert_block(args):
        tok, e = args
        xr = xt[tok]
        return (jax.nn.silu(xr @ w1[e]) * (xr @ w3[e])) @ w2[e]

    y = lax.map(expert_block, (row_tok.reshape(-1, MOE_ROW_BLOCK), blk_e)).reshape(-1, D)
    w_row = jnp.where(row_ok, gate[pair], 0.0).astype(y.dtype)
    seg = jnp.where(row_ok, row_tok, N)
    out = jax.ops.segment_sum(y * w_row[:, None], seg, num_segments=N + 1)[:N]
    return out.reshape(B, S, D)


def setup_inputs(seed: int = 0) -> dict:
    key = jax.random.key(seed)
    ks = iter(jax.random.split(key, 32))
    D = D_MODEL

    def nrm(shape, s):
        return jax.random.normal(next(ks), shape, jnp.float32) * s

    return {
        'x': nrm((BATCH, SEQ, D), 1.0),
        'c': nrm((BATCH, D), 1.0),
        'rel_bias': nrm((REL_BUCKETS, REL_HEADS), 0.5),
        'ada_w': nrm((DEPTH, D, 6 * D), 0.5 * D ** -0.5),
        'ada_b': nrm((DEPTH, 6 * D), 0.02),
        'norm1_w': 1.0 + nrm((DEPTH, D), 0.05),
        'norm2_w': 1.0 + nrm((DEPTH, D), 0.05),
        'even_w_in': nrm((N_EVEN, D, EVEN_IN), D ** -0.5),
        'even_w_out': nrm((N_EVEN, EVEN_MIX, D), EVEN_MIX ** -0.5),
        'a_q_norm': 1.0 + nrm((N_EVEN, A_HEAD_DIM), 0.05),
        'a_k_norm': 1.0 + nrm((N_EVEN, A_HEAD_DIM), 0.05),
        'a_sinks': nrm((N_EVEN, A_HEADS), 0.5),
        'b_gate_up': nrm((N_EVEN, B_GATE_RANK, B_HEADS * B_KEY_DIM), B_GATE_RANK ** -0.5),
        'b_gate_bias': nrm((N_EVEN, B_HEADS * B_KEY_DIM), 0.1),
        'b_out_norm': 1.0 + nrm((N_EVEN, B_VAL_DIM), 0.05),
        'odd_w_in': nrm((N_ODD, D, 3 * ODD_MIX), D ** -0.5),
        'odd_w_out': nrm((N_ODD, ODD_MIX, D), ODD_MIX ** -0.5),
        'c_q_norm': 1.0 + nrm((N_ODD, C_HEAD_DIM), 0.05),
        'c_k_norm': 1.0 + nrm((N_ODD, C_HEAD_DIM), 0.05),
        'moe_w_group': nrm((DEPTH, D, MOE_GROUPS), D ** -0.5),
        'moe_b_group': nrm((DEPTH, MOE_GROUPS), 0.01),
        'moe_w_expert': nrm((DEPTH, D, MOE_EXPERTS), D ** -0.5),
        'moe_b_expert': nrm((DEPTH, MOE_EXPERTS), 0.01),
        'moe_w1': nrm((DEPTH, MOE_EXPERTS, D, MOE_HIDDEN), D ** -0.5),
        'moe_w3': nrm((DEPTH, MOE_EXPERTS, D, MOE_HIDDEN), D ** -0.5),
        'moe_w2': nrm((DEPTH, MOE_EXPERTS, MOE_HIDDEN, D), MOE_HIDDEN ** -0.5),
    }


def reference(x, c, rel_bias, ada_w, ada_b, norm1_w, norm2_w, even_w_in, even_w_out, a_q_norm,
              a_k_norm, a_sinks, b_gate_up, b_gate_bias, b_out_norm, odd_w_in, odd_w_out, c_q_norm,
              c_k_norm, moe_w_group, moe_b_group, moe_w_expert, moe_b_expert, moe_w1, moe_w3, moe_w2):
    cond = jax.nn.silu(c)
    for layer in range(DEPTH):
        mod = (cond @ ada_w[layer] + ada_b[layer])[:, None, :]
        shift1, scale1, gate1, shift2, scale2, gate2 = jnp.split(mod, 6, axis=-1)
        h = _rms(x, norm1_w[layer]) * (1 + scale1) + shift1
        j = layer // 2
        if layer % 2 == 0:
            y = _even_mixer(h, even_w_in[j], even_w_out[j], a_q_norm[j], a_k_norm[j], a_sinks[j],
                            b_gate_up[j], b_gate_bias[j], b_out_norm[j], rel_bias)
        else:
            y = _odd_mixer(h, odd_w_in[j], odd_w_out[j], c_q_norm[j], c_k_norm[j], rel_bias)
        x = x + gate1 * y
        h = _rms(x, norm2_w[layer]) * (1 + scale2) + shift2
        x = x + gate2 * _hier_moe(h, moe_w_group[layer], moe_b_group[layer], moe_w_expert[layer],
                                  moe_b_expert[layer], moe_w1[layer], moe_w3[layer], moe_w2[layer])
    return x
```

```python
import functools
import math

import jax
import jax.numpy as jnp
import numpy as np
from jax import lax
from jax.experimental import pallas as pl
from jax.experimental.pallas import tpu as pltpu

RMS_EPS = 1e-6
A_HEADS, A_KV_HEADS, A_HEAD_DIM, A_WINDOW = 8, 2, 64, 128
B_HEADS, B_KEY_DIM, B_VAL_DIM, B_GATE_RANK, B_GATE_TAU, B_CHUNK = 4, 64, 128, 16, 16.0, 64
C_HEADS, C_HEAD_DIM, C_BLOCK, C_TOPK = 8, 128, 256, 3
REL_BUCKETS, REL_MAX_DIST, REL_HEADS = 32, 128, 8
MOE_GROUPS, MOE_EXPERTS_PER_GROUP, MOE_TOPK = 4, 8, 2
MOE_EXPERTS = MOE_GROUPS * MOE_EXPERTS_PER_GROUP

LANES = 128
V7X_VMEM_LIMIT_BYTES = 56 * 1024 * 1024
NEG_BIG = -1e30
PROJ_ROWS = 512
GLA_ROWS = 256
ROUTER_ROWS = 512
MOE_BLOCK_ROWS = 256
DISPATCH_ROWS = 256
ROUTER_LOGIT_ROWS = 8 + MOE_EXPERTS

F32 = jnp.float32
BF16 = jnp.bfloat16


def _cparams(*sem):
    return pltpu.CompilerParams(dimension_semantics=sem, vmem_limit_bytes=V7X_VMEM_LIMIT_BYTES)


def _rel_bucket_np(dist):
    exact = REL_BUCKETS // 2
    d = np.maximum(dist, 0)
    logd = np.log(np.maximum(d, 1).astype(np.float64) / exact) / math.log(REL_MAX_DIST / exact)
    far = np.minimum(exact + (logd * (REL_BUCKETS - exact)).astype(np.int64), REL_BUCKETS - 1)
    return np.where(d < exact, d, far).astype(np.int32)


def _bucket_tiles():
    li = np.arange(C_BLOCK)
    own = _rel_bucket_np(li[:, None] - li[None, :])
    prev = _rel_bucket_np(li[:, None] - li[None, :] + C_BLOCK)
    band = _rel_bucket_np(np.arange(A_WINDOW)[:, None] + A_WINDOW - np.arange(2 * A_WINDOW)[None, :])
    return np.concatenate([own, prev, band], axis=0)


def _rms_rows(x, w):
    return x * lax.rsqrt(jnp.mean(x * x, axis=-1, keepdims=True) + RMS_EPS) * w


def _dot(a, b):
    return jnp.dot(a.astype(BF16), b.astype(BF16), preferred_element_type=F32)


def _dot_nt(a, b):
    return lax.dot_general(a.astype(BF16), b.astype(BF16), (((1,), (1,)), ((), ())),
                           preferred_element_type=F32)


def _dot_tn(a, b):
    return lax.dot_general(a.astype(BF16), b.astype(BF16), (((0,), (0,)), ((), ())),
                           preferred_element_type=F32)


def _bias_kernel(rb_ref, bkt_ref, o_ref):
    h = pl.program_id(0)
    bkt = bkt_ref[...]
    acc = jnp.zeros(bkt.shape, F32)
    for b in range(REL_BUCKETS):
        acc = jnp.where(bkt == b, rb_ref[b, h], acc)
    o_ref[0] = acc


def _bias_tiles(rel_bias):
    bkt = jnp.asarray(_bucket_tiles())
    rows, cols = bkt.shape
    return pl.pallas_call(
        _bias_kernel,
        grid=(REL_HEADS,),
        in_specs=[pl.BlockSpec(memory_space=pltpu.SMEM),
                  pl.BlockSpec((rows, cols), lambda h: (0, 0))],
        out_specs=pl.BlockSpec((1, rows, cols), lambda h: (h, 0, 0)),
        out_shape=jax.ShapeDtypeStruct((REL_HEADS, rows, cols), F32),
        compiler_params=_cparams("arbitrary"),
        name="rel_bias_tiles",
    )(rel_bias, bkt)


def _adaln_kernel(c_ref, w_ref, b_ref, o_ref):
    c = c_ref[...]
    cond = c * jax.nn.sigmoid(c)
    o_ref[0] = _dot(cond, w_ref[0]) + b_ref[0]


def _adaln(c, ada_w, ada_b):
    depth, d, n6 = ada_w.shape
    bsz = c.shape[0]
    tn = 1536 if n6 % 1536 == 0 else n6
    return pl.pallas_call(
        _adaln_kernel,
        grid=(depth, n6 // tn),
        in_specs=[pl.BlockSpec((bsz, d), lambda l, j: (0, 0)),
                  pl.BlockSpec((1, d, tn), lambda l, j: (l, 0, j)),
                  pl.BlockSpec((1, 1, tn), lambda l, j: (l, 0, j))],
        out_specs=pl.BlockSpec((1, bsz, tn), lambda l, j: (l, 0, j)),
        out_shape=jax.ShapeDtypeStruct((depth, bsz, n6), F32),
        compiler_params=_cparams("arbitrary", "arbitrary"),
        name="adaln_mod",
    )(c, ada_w, ada_b.reshape(depth, 1, n6))


def _proj_kernel(x_ref, nw_ref, mod_ref, w_ref, *o_refs, shift_row, widths):
    x = x_ref[0]
    h = _rms_rows(x, nw_ref[...])
    h = h * (1.0 + mod_ref[0, shift_row + 1:shift_row + 2, :]) + mod_ref[0, shift_row:shift_row + 1, :]
    y = _dot(h, w_ref[...])
    off = 0
    for o_ref, wd in zip(o_refs, widths):
        o_ref[0] = y[:, off:off + wd].astype(o_ref.dtype)
        off += wd


def _norm_mod_project(x, norm_w, mod, w_bf16, widths, shift_row):
    bsz, s, d = x.shape
    tm = min(PROJ_ROWS, s)
    n = w_bf16.shape[1]
    assert sum(widths) == n and all(wd % LANES == 0 for wd in widths)
    return pl.pallas_call(
        functools.partial(_proj_kernel, shift_row=shift_row, widths=tuple(widths)),
        grid=(bsz, s // tm),
        in_specs=[pl.BlockSpec((1, tm, d), lambda b, i: (b, i, 0)),
                  pl.BlockSpec((1, d), lambda b, i: (0, 0)),
                  pl.BlockSpec((1, 6, d), lambda b, i: (b, 0, 0)),
                  pl.BlockSpec((d, n), lambda b, i: (0, 0))],
        out_specs=[pl.BlockSpec((1, tm, wd), lambda b, i: (b, i, 0)) for wd in widths],
        out_shape=[jax.ShapeDtypeStruct((bsz, s, wd), F32) for wd in widths],
        compiler_params=_cparams("arbitrary", "arbitrary"),
        name="norm_mod_project",
    )(x, norm_w.reshape(1, d), mod, w_bf16)


def _out_proj_kernel(*refs, n_in, gate_row):
    a_refs = refs[:n_in]
    w_refs = refs[n_in:2 * n_in]
    x_ref, mod_ref, o_ref = refs[2 * n_in:]
    y = _dot(a_refs[0][0], w_refs[0][...])
    for a_ref, w_ref in zip(a_refs[1:], w_refs[1:]):
        y = y + _dot(a_ref[0], w_ref[...])
    o_ref[0] = x_ref[0] + mod_ref[0, gate_row:gate_row + 1, :] * y


def _out_project_residual(acts, weights, x, mod, gate_row):
    bsz, s, d = x.shape
    tm = min(PROJ_ROWS, s)
    n_in = len(acts)
    in_specs = [pl.BlockSpec((1, tm, a.shape[2]), lambda b, i: (b, i, 0)) for a in acts]
    in_specs += [pl.BlockSpec(w.shape, lambda b, i: (0, 0)) for w in weights]
    in_specs += [pl.BlockSpec((1, tm, d), lambda b, i: (b, i, 0)),
                 pl.BlockSpec((1, 6, d), lambda b, i: (b, 0, 0))]
    return pl.pallas_call(
        functools.partial(_out_proj_kernel, n_in=n_in, gate_row=gate_row),
        grid=(bsz, s // tm),
        in_specs=in_specs,
        out_specs=pl.BlockSpec((1, tm, d), lambda b, i: (b, i, 0)),
        out_shape=jax.ShapeDtypeStruct((bsz, s, d), F32),
        compiler_params=_cparams("arbitrary", "arbitrary"),
        name="out_project_residual",
    )(*acts, *weights, x, mod)


def _swa_kernel(sink_ref, q_ref, kc_ref, kp_ref, vc_ref, vp_ref, bias_ref, qn_ref, kn_ref, o_ref):
    n = pl.program_id(1)
    w = A_WINDOW
    hd = A_HEAD_DIM
    group = A_HEADS // A_KV_HEADS
    q = q_ref[0]
    k = jnp.concatenate([kp_ref[0], kc_ref[0]], axis=0)
    v = jnp.concatenate([vp_ref[0], vc_ref[0]], axis=0)
    row = lax.broadcasted_iota(jnp.int32, (w, 2 * w), 0)
    col = lax.broadcasted_iota(jnp.int32, (w, 2 * w), 1)
    dist = row + w - col
    mask = (dist >= 0) & (dist < w) & ((n > 0) | (col >= w))
    outs = []
    for kv in range(A_KV_HEADS):
        k_h = _rms_rows(k[:, kv * hd:(kv + 1) * hd], kn_ref[...]).astype(BF16)
        v_h = v[:, kv * hd:(kv + 1) * hd].astype(BF16)
        for g in range(group):
            h = kv * group + g
            q_h = _rms_rows(q[:, h * hd:(h + 1) * hd], qn_ref[...])
            lg = _dot_nt(q_h, k_h) * (hd ** -0.5) + bias_ref[h]
            lg = jnp.where(mask, lg, NEG_BIG)
            sink = sink_ref[h]
            m = jnp.maximum(jnp.max(lg, axis=-1, keepdims=True), sink)
            p = jnp.exp(lg - m)
            den = jnp.sum(p, axis=-1, keepdims=True) + jnp.exp(sink - m)
            outs.append(_dot(p, v_h) / den)
    o_ref[0] = jnp.concatenate(outs, axis=-1).astype(o_ref.dtype)


def _swa_attention(qa, ka, va, sinks, bias_tiles, q_norm, k_norm):
    bsz, s, _ = qa.shape
    w = A_WINDOW
    nb = s // w
    kvw = A_KV_HEADS * A_HEAD_DIM
    band_blk = (2 * C_BLOCK) // w
    cur = lambda b, n: (b, n, 0)
    prev = lambda b, n: (b, jnp.maximum(n - 1, 0), 0)
    return pl.pallas_call(
        _swa_kernel,
        grid=(bsz, nb),
        in_specs=[pl.BlockSpec(memory_space=pltpu.SMEM),
                  pl.BlockSpec((1, w, A_HEADS * A_HEAD_DIM), cur),
                  pl.BlockSpec((1, w, kvw), cur),
                  pl.BlockSpec((1, w, kvw), prev),
                  pl.BlockSpec((1, w, kvw), cur),
                  pl.BlockSpec((1, w, kvw), prev),
                  pl.BlockSpec((REL_HEADS, w, 2 * w), lambda b, n: (0, band_blk, 0)),
                  pl.BlockSpec((1, A_HEAD_DIM), lambda b, n: (0, 0)),
                  pl.BlockSpec((1, A_HEAD_DIM), lambda b, n: (0, 0))],
        out_specs=pl.BlockSpec((1, w, A_HEADS * A_HEAD_DIM), cur),
        out_shape=jax.ShapeDtypeStruct((bsz, s, A_HEADS * A_HEAD_DIM), BF16),
        compiler_params=_cparams("arbitrary", "arbitrary"),
        name="swa_sink_attention",
    )(sinks, qa, ka, ka, va, va, bias_tiles, q_norm.reshape(1, -1), k_norm.reshape(1, -1))


def _gla_kernel(qk_ref, v_ref, r_ref, ab_ref, gu_ref, gb_ref, on_ref, o_ref, state_ref):
    s_idx = pl.program_id(1)
    c_len = B_CHUNK
    dk, dv = B_KEY_DIM, B_VAL_DIM
    hk = B_HEADS * dk

    @pl.when(s_idx == 0)
    def _():
        state_ref[...] = jnp.zeros(state_ref.shape, F32)

    z = _dot(ab_ref[0], gu_ref[...]) + gb_ref[...]
    log_a = (jnp.minimum(z, 0.0) - jnp.log(1.0 + jnp.exp(-jnp.abs(z)))) / B_GATE_TAU
    rows = qk_ref.shape[1]
    ri = lax.broadcasted_iota(jnp.int32, (c_len, c_len), 0)
    ci = lax.broadcasted_iota(jnp.int32, (c_len, c_len), 1)
    tril = ri >= ci
    tri = tril.astype(BF16)
    for c in range(rows // c_len):
        sl = slice(c * c_len, (c + 1) * c_len)
        g = log_a[sl]
        g_hi = g.astype(BF16)
        g_lo = (g - g_hi.astype(F32)).astype(BF16)
        b = (jnp.dot(tri, g_hi, preferred_element_type=F32)
             + jnp.dot(tri, g_lo, preferred_element_type=F32))
        b_last = b[c_len - 1:c_len, :]
        q = qk_ref[0, sl, 0:hk] * (dk ** -0.5)
        k = qk_ref[0, sl, hk:2 * hk]
        q_dec = q * jnp.exp(b)
        k_dec = k * jnp.exp(-b)
        k_upd = k * jnp.exp(b_last - b)
        decay = jnp.exp(b_last)
        heads = []
        for h in range(B_HEADS):
            ks = slice(h * dk, (h + 1) * dk)
            v_h = v_ref[0, sl, h * dv:(h + 1) * dv]
            st = state_ref[h]
            att = jnp.where(tril, _dot_nt(q_dec[:, ks], k_dec[:, ks]), 0.0)
            o_h = _dot(att, v_h) + _dot_nt(q_dec[:, ks], st)
            state_ref[h] = st * decay[:, ks] + _dot_tn(v_h, k_upd[:, ks])
            o_h = _rms_rows(o_h, on_ref[...])
            r_h = r_ref[0, sl, h * dv:(h + 1) * dv]
            heads.append(o_h * (r_h * jax.nn.sigmoid(r_h)))
        o_ref[0, sl, :] = jnp.concatenate(heads, axis=-1).astype(o_ref.dtype)


def _gla(qk, vb, rb, ab, gate_up_pad, gate_bias, out_norm):
    bsz, s, _ = qk.shape
    tm = min(GLA_ROWS, s)
    hv = B_HEADS * B_VAL_DIM
    hk = B_HEADS * B_KEY_DIM
    blk = lambda b, i: (b, i, 0)
    const = lambda b, i: (0, 0)
    return pl.pallas_call(
        _gla_kernel,
        grid=(bsz, s // tm),
        in_specs=[pl.BlockSpec((1, tm, 2 * hk), blk),
                  pl.BlockSpec((1, tm, hv), blk),
                  pl.BlockSpec((1, tm, hv), blk),
                  pl.BlockSpec((1, tm, LANES), blk),
                  pl.BlockSpec((LANES, hk), const),
                  pl.BlockSpec((1, hk), const),
                  pl.BlockSpec((1, B_VAL_DIM), const)],
        out_specs=pl.BlockSpec((1, tm, hv), blk),
        out_shape=jax.ShapeDtypeStruct((bsz, s, hv), BF16),
        scratch_shapes=[pltpu.VMEM((B_HEADS, B_VAL_DIM, B_KEY_DIM), F32)],
        compiler_params=_cparams("arbitrary", "arbitrary"),
        name="gated_linear_attention",
    )(qk, vb, rb, ab, gate_up_pad, gate_bias.reshape(1, hk), out_norm.reshape(1, -1))


def _moba_kernel(q_ref, k_ref, v_ref, bias_ref, qn_ref, kn_ref, o_ref, kb_ref, vb_ref, kmean_ref):
    i = pl.program_id(2)
    blk = C_BLOCK
    hd = C_HEAD_DIM
    nb = k_ref.shape[1] // blk
    scale = hd ** -0.5

    @pl.when(i == 0)
    def _():
        kn = _rms_rows(k_ref[0], kn_ref[...])
        kb_ref[...] = kn.astype(BF16)
        vb_ref[...] = v_ref[0].astype(BF16)
        kmean_ref[...] = jnp.mean(kn.reshape(nb, blk, hd), axis=1)

    qb = _rms_rows(q_ref[0], qn_ref[...]).astype(BF16)

    score = _dot_nt(qb, kmean_ref[...])
    jcol = lax.broadcasted_iota(jnp.int32, (blk, nb), 1)
    score = jnp.where(jcol < i, score, -jnp.inf)
    beats = jnp.zeros((blk, nb), jnp.int32)
    for j2 in range(nb):
        s2 = score[:, j2:j2 + 1]
        beats = beats + ((s2 > score) | ((s2 == score) & (j2 < jcol))).astype(jnp.int32)
    chosen = ((beats < C_TOPK) & (jcol < i)).astype(F32)

    ri = lax.broadcasted_iota(jnp.int32, (blk, blk), 0)
    ci = lax.broadcasted_iota(jnp.int32, (blk, blk), 1)
    own0 = pl.multiple_of(i * blk, blk)
    lg = _dot_nt(qb, kb_ref[pl.ds(own0, blk), :]) * scale + bias_ref[0, 0:blk, :]
    lg = jnp.where(ri >= ci, lg, NEG_BIG)
    m0 = jnp.max(lg, axis=-1, keepdims=True)
    p = jnp.exp(lg - m0)
    l0 = jnp.sum(p, axis=-1, keepdims=True)
    acc0 = _dot(p, vb_ref[pl.ds(own0, blk), :])
    far_bias = bias_ref[0, blk:blk + 1, 0:1]

    def body(j, carry):
        m, l, acc = carry
        j0 = pl.multiple_of(j * blk, blk)
        bias = jnp.where(j == i - 1, bias_ref[0, blk:2 * blk, :], far_bias)
        lg = _dot_nt(qb, kb_ref[pl.ds(j0, blk), :]) * scale + bias
        picked = jnp.sum(jnp.where(jcol == j, chosen, 0.0), axis=-1, keepdims=True) > 0.5
        lg = jnp.where(picked, lg, NEG_BIG)
        m_new = jnp.maximum(m, jnp.max(lg, axis=-1, keepdims=True))
        alpha = jnp.exp(m - m_new)
        p = jnp.exp(lg - m_new)
        l = alpha * l + jnp.sum(p, axis=-1, keepdims=True)
        acc = alpha * acc + _dot(p, vb_ref[pl.ds(j0, blk), :])
        return m_new, l, acc

    m, l, acc = lax.fori_loop(0, i, body, (m0, l0, acc0))
    o_ref[0] = (acc / l).astype(o_ref.dtype)


def _moba(q, k, v, bias_tiles, q_norm, k_norm):
    bsz, s, _ = q.shape
    blk, hd = C_BLOCK, C_HEAD_DIM
    assert s % blk == 0
    nb = s // blk
    return pl.pallas_call(
        _moba_kernel,
        grid=(bsz, C_HEADS, nb),
        in_specs=[pl.BlockSpec((1, blk, hd), lambda b, h, i: (b, i, h)),
                  pl.BlockSpec((1, s, hd), lambda b, h, i: (b, 0, h)),
                  pl.BlockSpec((1, s, hd), lambda b, h, i: (b, 0, h)),
                  pl.BlockSpec((1, 2 * blk, blk), lambda b, h, i: (h, 0, 0)),
                  pl.BlockSpec((1, hd), lambda b, h, i: (0, 0)),
                  pl.BlockSpec((1, hd), lambda b, h, i: (0, 0))],
        out_specs=pl.BlockSpec((1, blk, hd), lambda b, h, i: (b, i, h)),
        out_shape=jax.ShapeDtypeStruct((bsz, s, C_HEADS * hd), BF16),
        scratch_shapes=[pltpu.VMEM((s, hd), BF16), pltpu.VMEM((s, hd), BF16),
                        pltpu.VMEM((nb, hd), F32)],
        compiler_params=_cparams("arbitrary", "arbitrary", "arbitrary"),
        name="moba_attention",
    )(q, k, v, bias_tiles, q_norm.reshape(1, -1), k_norm.reshape(1, -1))


def _router_kernel(x_ref, nw_ref, mod_ref, wr_ref, br_ref, h_ref, ids_ref, gate_ref, rank_ref, cnt_ref,
                   base_ref):
    first = (pl.program_id(0) == 0) & (pl.program_id(1) == 0)

    @pl.when(first)
    def _():
        base_ref[...] = jnp.zeros(base_ref.shape, F32)

    x = x_ref[0]
    tm = x.shape[0]
    h = _rms_rows(x, nw_ref[...])
    h = h * (1.0 + mod_ref[0, 4:5, :]) + mod_ref[0, 3:4, :]
    h_ref[0] = h
    lt = _dot_nt(wr_ref[...], h) + br_ref[...]
    g = [lt[r:r + 1, :] for r in range(MOE_GROUPS)]
    gmax = functools.reduce(jnp.maximum, g)
    gsel = jnp.full(gmax.shape, MOE_GROUPS - 1, jnp.int32)
    for r in range(MOE_GROUPS - 2, -1, -1):
        gsel = jnp.where(g[r] == gmax, r, gsel)
    p_g = 1.0 / functools.reduce(jnp.add, [jnp.exp(gr - gmax) for gr in g])
    epg = MOE_EXPERTS_PER_GROUP
    e_in = lt[8 + (MOE_GROUPS - 1) * epg:8 + MOE_GROUPS * epg, :]
    for r in range(MOE_GROUPS - 2, -1, -1):
        e_in = jnp.where(gsel == r, lt[8 + r * epg:8 + (r + 1) * epg, :], e_in)
    sub = lax.broadcasted_iota(jnp.int32, (epg, tm), 0)
    v1 = jnp.max(e_in, axis=0, keepdims=True)
    i1 = jnp.min(jnp.where(e_in == v1, sub, epg), axis=0, keepdims=True)
    e2 = jnp.where(sub == i1, -jnp.inf, e_in)
    v2 = jnp.max(e2, axis=0, keepdims=True)
    i2 = jnp.min(jnp.where(e2 == v2, sub, epg), axis=0, keepdims=True)
    t = jnp.exp(v2 - v1)
    w1 = p_g / (1.0 + t)
    w2 = p_g * t / (1.0 + t)
    id1 = gsel * epg + i1
    id2 = gsel * epg + i2
    ids_ref[...] = jnp.concatenate([id1, id2], axis=0)
    gate_ref[...] = jnp.concatenate([w1, w2, jnp.zeros((LANES - 2, tm), F32)], axis=0).T

    eidx = lax.broadcasted_iota(jnp.int32, (MOE_EXPERTS, tm), 0)
    oh1 = eidx == id1
    oh2 = eidx == id2
    onehot = (oh1 | oh2).astype(BF16)
    tr = lax.broadcasted_iota(jnp.int32, (tm, tm), 0)
    tc = lax.broadcasted_iota(jnp.int32, (tm, tm), 1)
    before = (tr < tc).astype(BF16)
    prefix = jnp.dot(onehot, before, preferred_element_type=F32) + base_ref[...]
    r1 = jnp.sum(jnp.where(oh1, prefix, 0.0), axis=0, keepdims=True)
    r2 = jnp.sum(jnp.where(oh2, prefix, 0.0), axis=0, keepdims=True)
    rank_ref[...] = jnp.concatenate([r1, r2], axis=0).astype(jnp.int32)
    base_ref[...] = base_ref[...] + jnp.sum(onehot.astype(F32), axis=1, keepdims=True)
    cnt_ref[...] = base_ref[...].astype(jnp.int32)


def _route(x, norm_w, mod, wr_t, br):
    bsz, s, d = x.shape
    tm = min(ROUTER_ROWS, s)
    n = bsz * s
    nt = s // tm
    tok = lambda b, i: (0, b * nt + i)
    return pl.pallas_call(
        _router_kernel,
        grid=(bsz, nt),
        in_specs=[pl.BlockSpec((1, tm, d), lambda b, i: (b, i, 0)),
                  pl.BlockSpec((1, d), lambda b, i: (0, 0)),
                  pl.BlockSpec((1, 6, d), lambda b, i: (b, 0, 0)),
                  pl.BlockSpec((ROUTER_LOGIT_ROWS, d), lambda b, i: (0, 0)),
                  pl.BlockSpec((ROUTER_LOGIT_ROWS, 1), lambda b, i: (0, 0))],
        out_specs=[pl.BlockSpec((1, tm, d), lambda b, i: (b, i, 0)),
                   pl.BlockSpec((MOE_TOPK, tm), tok),
                   pl.BlockSpec((tm, LANES), lambda b, i: (b * nt + i, 0)),
                   pl.BlockSpec((MOE_TOPK, tm), tok),
                   pl.BlockSpec((MOE_EXPERTS, 1), lambda b, i: (0, 0))],
        out_shape=[jax.ShapeDtypeStruct((bsz, s, d), F32),
                   jax.ShapeDtypeStruct((MOE_TOPK, n), jnp.int32),
                   jax.ShapeDtypeStruct((n, LANES), F32),
                   jax.ShapeDtypeStruct((MOE_TOPK, n), jnp.int32),
                   jax.ShapeDtypeStruct((MOE_EXPERTS, 1), jnp.int32)],
        scratch_shapes=[pltpu.VMEM((MOE_EXPERTS, 1), F32)],
        compiler_params=_cparams("arbitrary", "arbitrary"),
        name="moe_route",
    )(x, norm_w.reshape(1, d), mod, wr_t, br)


def _dest_kernel(ids_ref, rank_ref, start_ref, o_ref):
    ids = ids_ref[...]
    tm = ids.shape[1]
    eidx = lax.broadcasted_iota(jnp.int32, (MOE_EXPERTS, tm), 0)
    rows = []
    for k in range(MOE_TOPK):
        base = jnp.sum(jnp.where(eidx == ids[k:k + 1, :], start_ref[...], 0), axis=0, keepdims=True)
        rows.append(base + rank_ref[k:k + 1, :])
    o_ref[...] = jnp.concatenate(rows, axis=0)


def _dest_rows(ids, rank, start):
    n = ids.shape[1]
    tm = min(2048, n)
    return pl.pallas_call(
        _dest_kernel,
        grid=(n // tm,),
        in_specs=[pl.BlockSpec((MOE_TOPK, tm), lambda i: (0, i)),
                  pl.BlockSpec((MOE_TOPK, tm), lambda i: (0, i)),
                  pl.BlockSpec((MOE_EXPERTS, 1), lambda i: (0, 0))],
        out_specs=pl.BlockSpec((MOE_TOPK, tm), lambda i: (0, i)),
        out_shape=jax.ShapeDtypeStruct((MOE_TOPK, n), jnp.int32),
        compiler_params=_cparams("arbitrary"),
        name="moe_dest_rows",
    )(ids, rank, start)


def _row_copy(src, dst, sem, src_row, dst_row):
    return pltpu.make_async_copy(src.at[pl.ds(src_row, 1)], dst.at[pl.ds(dst_row, 1)], sem)


def _dispatch_kernel(dest_ref, h_ref, xs_in_ref, xs_ref, sem):
    del xs_in_ref
    tm = h_ref.shape[0]

    def start(t, _):
        for k in range(MOE_TOPK):
            _row_copy(h_ref, xs_ref, sem, t, dest_ref[k, t]).start()
        return 0

    def wait(t, _):
        for k in range(MOE_TOPK):
            _row_copy(h_ref, xs_ref, sem, t, dest_ref[k, t]).wait()
        return 0

    lax.fori_loop(0, tm, start, 0)
    lax.fori_loop(0, tm, wait, 0)


def _dispatch(h2d, dest, n_rows):
    n, d = h2d.shape
    tm = min(DISPATCH_ROWS, n)
    return pl.pallas_call(
        _dispatch_kernel,
        grid=(n // tm,),
        in_specs=[pl.BlockSpec((MOE_TOPK, tm), lambda i: (0, i), memory_space=pltpu.SMEM),
                  pl.BlockSpec((tm, d), lambda i: (i, 0)),
                  pl.BlockSpec(memory_space=pl.ANY)],
        out_specs=pl.BlockSpec(memory_space=pl.ANY),
        out_shape=jax.ShapeDtypeStruct((n_rows, d), F32),
        scratch_shapes=[pltpu.SemaphoreType.DMA(())],
        input_output_aliases={2: 0},
        compiler_params=_cparams("arbitrary"),
        name="moe_dispatch",
    )(dest, h2d, jnp.zeros((n_rows, d), F32))


def _expert_kernel(blk_e_ref, n_used_ref, x_ref, w1_ref, w3_ref, w2_ref, o_ref):
    del blk_e_ref
    used = pl.program_id(0) < n_used_ref[0]

    @pl.when(used)
    def _():
        x = x_ref[...]
        a = _dot(x, w1_ref[0])
        g = _dot(x, w3_ref[0])
        o_ref[...] = _dot(a * jax.nn.sigmoid(a) * g, w2_ref[0])

    @pl.when(jnp.logical_not(used))
    def _():
        o_ref[...] = jnp.zeros(o_ref.shape, F32)


def _expert_ffn(xs, blk_e, n_used, w1, w3, w2):
    n_rows, d = xs.shape
    tm = MOE_BLOCK_ROWS
    hid = w1.shape[2]
    row = lambda i, be, nu: (jnp.minimum(i, nu[0] - 1), 0)
    wsel = lambda i, be, nu: (be[i], 0, 0)
    return pl.pallas_call(
        _expert_kernel,
        grid_spec=pltpu.PrefetchScalarGridSpec(
            num_scalar_prefetch=2,
            grid=(n_rows // tm,),
            in_specs=[pl.BlockSpec((tm, d), row),
                      pl.BlockSpec((1, d, hid), wsel),
                      pl.BlockSpec((1, d, hid), wsel),
                      pl.BlockSpec((1, hid, d), wsel)],
            out_specs=pl.BlockSpec((tm, d), lambda i, be, nu: (i, 0))),
        out_shape=jax.ShapeDtypeStruct((n_rows, d), F32),
        compiler_params=_cparams("arbitrary"),
        name="moe_expert_ffn",
    )(blk_e, n_used, xs, w1, w3, w2)


def _combine_kernel(dest_ref, ys_ref, gate_ref, x_ref, mod_ref, o_ref, buf_ref, sem):
    tm = x_ref.shape[1]

    def start(t, _):
        for k in range(MOE_TOPK):
            _row_copy(ys_ref, buf_ref.at[k], sem, dest_ref[k, t], t).start()
        return 0

    def wait(t, _):
        for k in range(MOE_TOPK):
            _row_copy(ys_ref, buf_ref.at[k], sem, dest_ref[k, t], t).wait()
        return 0

    lax.fori_loop(0, tm, start, 0)
    lax.fori_loop(0, tm, wait, 0)
    g = gate_ref[...]
    y = g[:, 0:1] * buf_ref[0] + g[:, 1:2] * buf_ref[1]
    o_ref[0] = x_ref[0] + mod_ref[0, 5:6, :] * y


def _combine(ys, dest, gates, x, mod):
    bsz, s, d = x.shape
    tm = min(DISPATCH_ROWS, s)
    nt = s // tm
    return pl.pallas_call(
        _combine_kernel,
        grid=(bsz, nt),
        in_specs=[pl.BlockSpec((MOE_TOPK, tm), lambda b, i: (0, b * nt + i), memory_space=pltpu.SMEM),
                  pl.BlockSpec(memory_space=pl.ANY),
                  pl.BlockSpec((tm, LANES), lambda b, i: (b * nt + i, 0)),
                  pl.BlockSpec((1, tm, d), lambda b, i: (b, i, 0)),
                  pl.BlockSpec((1, 6, d), lambda b, i: (b, 0, 0))],
        out_specs=pl.BlockSpec((1, tm, d), lambda b, i: (b, i, 0)),
        out_shape=jax.ShapeDtypeStruct((bsz, s, d), F32),
        scratch_shapes=[pltpu.VMEM((MOE_TOPK, tm, d), F32), pltpu.SemaphoreType.DMA(())],
        compiler_params=_cparams("arbitrary", "arbitrary"),
        name="moe_combine",
    )(dest, ys, gates, x, mod)


def _hier_moe_residual(x, norm_w, mod, w_group, b_group, w_expert, b_expert, w1, w3, w2):
    bsz, s, d = x.shape
    n = bsz * s
    tm = MOE_BLOCK_ROWS
    wr_t = jnp.zeros((ROUTER_LOGIT_ROWS, d), F32).at[0:MOE_GROUPS].set(w_group.T).at[8:].set(w_expert.T)
    br = jnp.zeros((ROUTER_LOGIT_ROWS, 1), F32).at[0:MOE_GROUPS, 0].set(b_group).at[8:, 0].set(b_expert)
    h2, ids, gates, rank, counts = _route(x, norm_w, mod, wr_t.astype(BF16), br)
    counts = counts[:, 0]
    padded = ((counts + tm - 1) // tm) * tm
    pend = jnp.cumsum(padded)
    start = (pend - padded).astype(jnp.int32)
    n_rows = ((n * MOE_TOPK + tm - 1) // tm) * tm + MOE_EXPERTS * tm
    n_blk = n_rows // tm
    blk_e = jnp.minimum(jnp.searchsorted(pend, jnp.arange(n_blk, dtype=jnp.int32) * tm, side='right'),
                        MOE_EXPERTS - 1).astype(jnp.int32)
    n_used = (pend[-1:] // tm).astype(jnp.int32)
    dest = _dest_rows(ids, rank, start.reshape(MOE_EXPERTS, 1))
    xs = _dispatch(h2.reshape(n, d), dest, n_rows)
    ys = _expert_ffn(xs, blk_e, n_used, w1, w3, w2)
    return _combine(ys, dest, gates, x, mod)


def _even_mixer_residual(x, norm_w, mod, w_in, w_out, q_norm, k_norm, sinks, gate_up, gate_bias, out_norm,
                         bias_tiles):
    d = x.shape[2]
    aq = A_HEADS * A_HEAD_DIM
    akv = A_KV_HEADS * A_HEAD_DIM
    bk = B_HEADS * B_KEY_DIM
    bv = B_HEADS * B_VAL_DIM
    n_in = w_in.shape[1]
    w_pad = jnp.zeros((d, n_in - B_GATE_RANK + LANES), F32).at[:, :n_in].set(w_in).astype(BF16)
    widths = (aq, akv, akv, 2 * bk, bv, bv, LANES)
    qa, ka, va, qk, vb, rb, ab = _norm_mod_project(x, norm_w, mod, w_pad, widths, shift_row=0)
    oa = _swa_attention(qa, ka, va, sinks, bias_tiles, q_norm, k_norm)
    gu_pad = jnp.zeros((LANES, bk), F32).at[:B_GATE_RANK].set(gate_up).astype(BF16)
    ob = _gla(qk, vb, rb, ab, gu_pad, gate_bias, out_norm)
    w_out = w_out.astype(BF16)
    return _out_project_residual([oa, ob], [w_out[:aq], w_out[aq:]], x, mod, gate_row=2)


def _odd_mixer_residual(x, norm_w, mod, w_in, w_out, q_norm, k_norm, bias_tiles):
    mix = C_HEADS * C_HEAD_DIM
    q, k, v = _norm_mod_project(x, norm_w, mod, w_in.astype(BF16), (mix, mix, mix), shift_row=0)
    o = _moba(q, k, v, bias_tiles, q_norm, k_norm)
    return _out_project_residual([o], [w_out.astype(BF16)], x, mod, gate_row=2)


def kernel(x, c, rel_bias, ada_w, ada_b, norm1_w, norm2_w, even_w_in, even_w_out, a_q_norm, a_k_norm, a_sinks, b_gate_up, b_gate_bias, b_out_norm, odd_w_in, odd_w_out, c_q_norm, c_k_norm, moe_w_group, moe_b_group, moe_w_expert, moe_b_expert, moe_w1, moe_w3, moe_w2):
    depth = ada_w.shape[0]
    bsz, _, d = x.shape
    bias_tiles = _bias_tiles(rel_bias)
    mod_all = _adaln(c, ada_w, ada_b).reshape(depth, bsz, 6, d)
    for layer in range(depth):
        mod = mod_all[layer]
        j = layer // 2
        if layer % 2 == 0:
            x = _even_mixer_residual(x, norm1_w[layer], mod, even_w_in[j], even_w_out[j], a_q_norm[j],
                                     a_k_norm[j], a_sinks[j], b_gate_up[j], b_gate_bias[j], b_out_norm[j],
                                     bias_tiles)
        else:
            x = _odd_mixer_residual(x, norm1_w[layer], mod, odd_w_in[j], odd_w_out[j], c_q_norm[j],
                                    c_k_norm[j], bias_tiles)
        x = _hier_moe_residual(x, norm2_w[layer], mod, moe_w_group[layer], moe_b_group[layer],
                               moe_w_expert[layer], moe_b_expert[layer], moe_w1[layer].astype(BF16),
                               moe_w3[layer].astype(BF16), moe_w2[layer].astype(BF16))
    return x
```

```python
import functools
import math

import jax
import jax.numpy as jnp
import numpy as np
from jax import lax
from jax.experimental import pallas as pl
from jax.experimental.pallas import tpu as pltpu

RMS_EPS = 1e-6
A_HEADS, A_KV_HEADS, A_HEAD_DIM, A_WINDOW = 8, 2, 64, 128
B_HEADS, B_KEY_DIM, B_VAL_DIM, B_GATE_RANK, B_GATE_TAU, B_CHUNK = 4, 64, 128, 16, 16.0, 64
C_HEADS, C_HEAD_DIM, C_BLOCK, C_TOPK = 8, 128, 256, 3
REL_BUCKETS, REL_MAX_DIST, REL_HEADS = 32, 128, 8
MOE_GROUPS, MOE_EXPERTS_PER_GROUP, MOE_TOPK = 4, 8, 2
MOE_EXPERTS = MOE_GROUPS * MOE_EXPERTS_PER_GROUP

LANES = 128
V7X_VMEM_LIMIT_BYTES = 56 * 1024 * 1024
NEG_BIG = -1e30
PROJ_ROWS = 512
GLA_ROWS = 256
ROUTER_ROWS = 512
MOE_BLOCK_ROWS = 256
DISPATCH_ROWS = 512
COMBINE_ROWS = 256
DMA_ISSUE_UNROLL = 8
ROUTER_LOGIT_ROWS = 8 + MOE_EXPERTS
MOBA_GROUP = 4
MOBA_HEADS_PER_STEP = 2
LOG2E = math.log2(math.e)
MOBA_VT_ROWS = C_HEAD_DIM + 16

F32 = jnp.float32
BF16 = jnp.bfloat16


def _cparams(*sem):
    return pltpu.CompilerParams(dimension_semantics=sem, vmem_limit_bytes=V7X_VMEM_LIMIT_BYTES)


def _rel_bucket_np(dist):
    exact = REL_BUCKETS // 2
    d = np.maximum(dist, 0)
    logd = np.log(np.maximum(d, 1).astype(np.float64) / exact) / math.log(REL_MAX_DIST / exact)
    far = np.minimum(exact + (logd * (REL_BUCKETS - exact)).astype(np.int64), REL_BUCKETS - 1)
    return np.where(d < exact, d, far).astype(np.int32)


def _bucket_tiles():
    li = np.arange(C_BLOCK)
    own = _rel_bucket_np(li[None, :] - li[:, None])
    prev = _rel_bucket_np(li[None, :] - li[:, None] + C_BLOCK)
    band = _rel_bucket_np(np.arange(A_WINDOW)[:, None] + A_WINDOW - np.arange(2 * A_WINDOW)[None, :])
    return np.concatenate([own, prev, band], axis=0)


def _rms_rows(x, w):
    return x * lax.rsqrt(jnp.mean(x * x, axis=-1, keepdims=True) + RMS_EPS) * w


def _dot(a, b):
    return jnp.dot(a.astype(BF16), b.astype(BF16), preferred_element_type=F32)


def _dot_nt(a, b):
    return lax.dot_general(a.astype(BF16), b.astype(BF16), (((1,), (1,)), ((), ())),
                           preferred_element_type=F32)


def _dot_tn(a, b):
    return lax.dot_general(a.astype(BF16), b.astype(BF16), (((0,), (0,)), ((), ())),
                           preferred_element_type=F32)


def _bias_kernel(rb_ref, bkt_ref, o_ref):
    h = pl.program_id(0)
    bkt = bkt_ref[...]
    acc = jnp.zeros(bkt.shape, F32)
    for b in range(REL_BUCKETS):
        acc = jnp.where(bkt == b, rb_ref[b, h], acc)
    row = lax.broadcasted_iota(jnp.int32, bkt.shape, 0)
    o_ref[0] = jnp.where(row < 2 * C_BLOCK, (acc - rb_ref[REL_BUCKETS - 1, h]) * LOG2E, acc)


def _bias_tiles(rel_bias):
    bkt = jnp.asarray(_bucket_tiles())
    rows, cols = bkt.shape
    return pl.pallas_call(
        _bias_kernel,
        grid=(REL_HEADS,),
        in_specs=[pl.BlockSpec(memory_space=pltpu.SMEM),
                  pl.BlockSpec((rows, cols), lambda h: (0, 0))],
        out_specs=pl.BlockSpec((1, rows, cols), lambda h: (h, 0, 0)),
        out_shape=jax.ShapeDtypeStruct((REL_HEADS, rows, cols), F32),
        compiler_params=_cparams("arbitrary"),
        name="rel_bias_tiles",
    )(rel_bias, bkt)


def _adaln_kernel(c_ref, w_ref, b_ref, o_ref):
    c = c_ref[...]
    cond = c * jax.nn.sigmoid(c)
    o_ref[0] = _dot(cond, w_ref[0]) + b_ref[0]


def _adaln(c, ada_w, ada_b):
    depth, d, n6 = ada_w.shape
    bsz = c.shape[0]
    tn = 1536 if n6 % 1536 == 0 else n6
    return pl.pallas_call(
        _adaln_kernel,
        grid=(depth, n6 // tn),
        in_specs=[pl.BlockSpec((bsz, d), lambda l, j: (0, 0)),
                  pl.BlockSpec((1, d, tn), lambda l, j: (l, 0, j)),
                  pl.BlockSpec((1, 1, tn), lambda l, j: (l, 0, j))],
        out_specs=pl.BlockSpec((1, bsz, tn), lambda l, j: (l, 0, j)),
        out_shape=jax.ShapeDtypeStruct((depth, bsz, n6), F32),
        compiler_params=_cparams("arbitrary", "arbitrary"),
        name="adaln_mod",
    )(c, ada_w, ada_b.reshape(depth, 1, n6))


def _proj_kernel(x_ref, nw_ref, mod_ref, w_ref, *o_refs, shift_row, widths):
    x = x_ref[0]
    h = _rms_rows(x, nw_ref[...])
    h = h * (1.0 + mod_ref[0, shift_row + 1:shift_row + 2, :]) + mod_ref[0, shift_row:shift_row + 1, :]
    y = _dot(h, w_ref[...])
    off = 0
    for o_ref, wd in zip(o_refs, widths):
        o_ref[0] = y[:, off:off + wd].astype(o_ref.dtype)
        off += wd


def _norm_mod_project(x, norm_w, mod, w_bf16, widths, shift_row):
    bsz, s, d = x.shape
    tm = min(PROJ_ROWS, s)
    n = w_bf16.shape[1]
    assert sum(widths) == n and all(wd % LANES == 0 for wd in widths)
    return pl.pallas_call(
        functools.partial(_proj_kernel, shift_row=shift_row, widths=tuple(widths)),
        grid=(bsz, s // tm),
        in_specs=[pl.BlockSpec((1, tm, d), lambda b, i: (b, i, 0)),
                  pl.BlockSpec((1, d), lambda b, i: (0, 0)),
                  pl.BlockSpec((1, 6, d), lambda b, i: (b, 0, 0)),
                  pl.BlockSpec((d, n), lambda b, i: (0, 0))],
        out_specs=[pl.BlockSpec((1, tm, wd), lambda b, i: (b, i, 0)) for wd in widths],
        out_shape=[jax.ShapeDtypeStruct((bsz, s, wd), F32) for wd in widths],
        compiler_params=_cparams("arbitrary", "arbitrary"),
        name="norm_mod_project",
    )(x, norm_w.reshape(1, d), mod, w_bf16)


def _out_proj_kernel(*refs, n_in, gate_row):
    a_refs = refs[:n_in]
    w_refs = refs[n_in:2 * n_in]
    x_ref, mod_ref, o_ref = refs[2 * n_in:]
    y = _dot(a_refs[0][0], w_refs[0][...])
    for a_ref, w_ref in zip(a_refs[1:], w_refs[1:]):
        y = y + _dot(a_ref[0], w_ref[...])
    o_ref[0] = x_ref[0] + mod_ref[0, gate_row:gate_row + 1, :] * y


def _out_project_residual(acts, weights, x, mod, gate_row):
    bsz, s, d = x.shape
    tm = min(PROJ_ROWS, s)
    n_in = len(acts)
    in_specs = [pl.BlockSpec((1, tm, a.shape[2]), lambda b, i: (b, i, 0)) for a in acts]
    in_specs += [pl.BlockSpec(w.shape, lambda b, i: (0, 0)) for w in weights]
    in_specs += [pl.BlockSpec((1, tm, d), lambda b, i: (b, i, 0)),
                 pl.BlockSpec((1, 6, d), lambda b, i: (b, 0, 0))]
    return pl.pallas_call(
        functools.partial(_out_proj_kernel, n_in=n_in, gate_row=gate_row),
        grid=(bsz, s // tm),
        in_specs=in_specs,
        out_specs=pl.BlockSpec((1, tm, d), lambda b, i: (b, i, 0)),
        out_shape=jax.ShapeDtypeStruct((bsz, s, d), F32),
        compiler_params=_cparams("arbitrary", "arbitrary"),
        name="out_project_residual",
    )(*acts, *weights, x, mod)


def _swa_kernel(sink_ref, q_ref, kc_ref, kp_ref, vc_ref, vp_ref, bias_ref, qn_ref, kn_ref, o_ref):
    n = pl.program_id(1)
    w = A_WINDOW
    hd = A_HEAD_DIM
    group = A_HEADS // A_KV_HEADS
    q = q_ref[0]
    k = jnp.concatenate([kp_ref[0], kc_ref[0]], axis=0)
    v = jnp.concatenate([vp_ref[0], vc_ref[0]], axis=0)
    row = lax.broadcasted_iota(jnp.int32, (w, 2 * w), 0)
    col = lax.broadcasted_iota(jnp.int32, (w, 2 * w), 1)
    dist = row + w - col
    mask = (dist >= 0) & (dist < w) & ((n > 0) | (col >= w))
    outs = []
    for kv in range(A_KV_HEADS):
        k_h = _rms_rows(k[:, kv * hd:(kv + 1) * hd], kn_ref[...]).astype(BF16)
        v_h = v[:, kv * hd:(kv + 1) * hd].astype(BF16)
        for g in range(group):
            h = kv * group + g
            q_h = _rms_rows(q[:, h * hd:(h + 1) * hd], qn_ref[...])
            lg = _dot_nt(q_h, k_h) * (hd ** -0.5) + bias_ref[h]
            lg = jnp.where(mask, lg, NEG_BIG)
            sink = sink_ref[h]
            m = jnp.maximum(jnp.max(lg, axis=-1, keepdims=True), sink)
            p = jnp.exp(lg - m)
            den = jnp.sum(p, axis=-1, keepdims=True) + jnp.exp(sink - m)
            outs.append(_dot(p, v_h) / den)
    o_ref[0] = jnp.concatenate(outs, axis=-1).astype(o_ref.dtype)


def _swa_attention(qa, ka, va, sinks, bias_tiles, q_norm, k_norm):
    bsz, s, _ = qa.shape
    w = A_WINDOW
    nb = s // w
    kvw = A_KV_HEADS * A_HEAD_DIM
    band_blk = (2 * C_BLOCK) // w
    cur = lambda b, n: (b, n, 0)
    prev = lambda b, n: (b, jnp.maximum(n - 1, 0), 0)
    return pl.pallas_call(
        _swa_kernel,
        grid=(bsz, nb),
        in_specs=[pl.BlockSpec(memory_space=pltpu.SMEM),
                  pl.BlockSpec((1, w, A_HEADS * A_HEAD_DIM), cur),
                  pl.BlockSpec((1, w, kvw), cur),
                  pl.BlockSpec((1, w, kvw), prev),
                  pl.BlockSpec((1, w, kvw), cur),
                  pl.BlockSpec((1, w, kvw), prev),
                  pl.BlockSpec((REL_HEADS, w, 2 * w), lambda b, n: (0, band_blk, 0)),
                  pl.BlockSpec((1, A_HEAD_DIM), lambda b, n: (0, 0)),
                  pl.BlockSpec((1, A_HEAD_DIM), lambda b, n: (0, 0))],
        out_specs=pl.BlockSpec((1, w, A_HEADS * A_HEAD_DIM), cur),
        out_shape=jax.ShapeDtypeStruct((bsz, s, A_HEADS * A_HEAD_DIM), BF16),
        compiler_params=_cparams("arbitrary", "arbitrary"),
        name="swa_sink_attention",
    )(sinks, qa, ka, ka, va, va, bias_tiles, q_norm.reshape(1, -1), k_norm.reshape(1, -1))


def _gla_kernel(qk_ref, v_ref, r_ref, ab_ref, gu_ref, gb_ref, on_ref, o_ref, state_ref):
    s_idx = pl.program_id(1)
    c_len = B_CHUNK
    dk, dv = B_KEY_DIM, B_VAL_DIM
    hk = B_HEADS * dk

    @pl.when(s_idx == 0)
    def _():
        state_ref[...] = jnp.zeros(state_ref.shape, F32)

    z = _dot(ab_ref[0], gu_ref[...]) + gb_ref[...]
    log_a = (jnp.minimum(z, 0.0) - jnp.log(1.0 + jnp.exp(-jnp.abs(z)))) / B_GATE_TAU
    rows = qk_ref.shape[1]
    ri = lax.broadcasted_iota(jnp.int32, (c_len, c_len), 0)
    ci = lax.broadcasted_iota(jnp.int32, (c_len, c_len), 1)
    tril = ri >= ci
    tri = tril.astype(BF16)
    for c in range(rows // c_len):
        sl = slice(c * c_len, (c + 1) * c_len)
        g = log_a[sl]
        g_hi = g.astype(BF16)
        g_lo = (g - g_hi.astype(F32)).astype(BF16)
        b = (jnp.dot(tri, g_hi, preferred_element_type=F32)
             + jnp.dot(tri, g_lo, preferred_element_type=F32))
        b_last = b[c_len - 1:c_len, :]
        q = qk_ref[0, sl, 0:hk] * (dk ** -0.5)
        k = qk_ref[0, sl, hk:2 * hk]
        q_dec = q * jnp.exp(b)
        k_dec = k * jnp.exp(-b)
        k_upd = k * jnp.exp(b_last - b)
        decay = jnp.exp(b_last)
        heads = []
        for h in range(B_HEADS):
            ks = slice(h * dk, (h + 1) * dk)
            v_h = v_ref[0, sl, h * dv:(h + 1) * dv]
            st = state_ref[h]
            att = jnp.where(tril, _dot_nt(q_dec[:, ks], k_dec[:, ks]), 0.0)
            o_h = _dot(att, v_h) + _dot_nt(q_dec[:, ks], st)
            state_ref[h] = st * decay[:, ks] + _dot_tn(v_h, k_upd[:, ks])
            o_h = _rms_rows(o_h, on_ref[...])
            r_h = r_ref[0, sl, h * dv:(h + 1) * dv]
            heads.append(o_h * (r_h * jax.nn.sigmoid(r_h)))
        o_ref[0, sl, :] = jnp.concatenate(heads, axis=-1).astype(o_ref.dtype)


def _gla(qk, vb, rb, ab, gate_up_pad, gate_bias, out_norm):
    bsz, s, _ = qk.shape
    tm = min(GLA_ROWS, s)
    hv = B_HEADS * B_VAL_DIM
    hk = B_HEADS * B_KEY_DIM
    blk = lambda b, i: (b, i, 0)
    const = lambda b, i: (0, 0)
    return pl.pallas_call(
        _gla_kernel,
        grid=(bsz, s // tm),
        in_specs=[pl.BlockSpec((1, tm, 2 * hk), blk),
                  pl.BlockSpec((1, tm, hv), blk),
                  pl.BlockSpec((1, tm, hv), blk),
                  pl.BlockSpec((1, tm, LANES), blk),
                  pl.BlockSpec((LANES, hk), const),
                  pl.BlockSpec((1, hk), const),
                  pl.BlockSpec((1, B_VAL_DIM), const)],
        out_specs=pl.BlockSpec((1, tm, hv), blk),
        out_shape=jax.ShapeDtypeStruct((bsz, s, hv), BF16),
        scratch_shapes=[pltpu.VMEM((B_HEADS, B_VAL_DIM, B_KEY_DIM), F32)],
        compiler_params=_cparams("arbitrary", "arbitrary"),
        name="gated_linear_attention",
    )(qk, vb, rb, ab, gate_up_pad, gate_bias.reshape(1, hk), out_norm.reshape(1, -1))


def _moba_kernel(q_ref, k_ref, v_ref, bias_ref, qn_ref, kn_ref, o_ref, kb_ref, vt_ref, kmean_ref, pick_ref):
    i = pl.program_id(2)
    blk = C_BLOCK
    hd = C_HEAD_DIM
    nb = k_ref.shape[1] // blk
    grp = MOBA_GROUP
    heads = range(MOBA_HEADS_PER_STEP)

    def lanes(hh):
        return slice(hh * hd, (hh + 1) * hd)

    @pl.when(i == 0)
    def _():
        ones_rows = (lax.broadcasted_iota(jnp.int32, (MOBA_VT_ROWS - hd, blk), 0) == 0).astype(BF16)
        for hh in heads:
            kn = _rms_rows(k_ref[0, :, lanes(hh)], kn_ref[...])
            kb_ref[:, lanes(hh)] = kn.astype(BF16)
            kmean_ref[hh] = jnp.mean(kn.reshape(nb, blk, hd), axis=1)
            for j in range(nb):
                vt = v_ref[0, j * blk:(j + 1) * blk, lanes(hh)].T.astype(BF16)
                vt_ref[hh, j] = jnp.concatenate([vt, ones_rows], axis=0)

    def logits(hh, q_h, j):
        j0 = pl.multiple_of(j * blk, blk)
        return _dot_nt(kb_ref[pl.ds(j0, blk), lanes(hh)], q_h)

    def col_max(lg, picked):
        return jnp.where(picked, jnp.max(lg, axis=0, keepdims=True), NEG_BIG)

    def weighted_v(hh, j, lg, m, picked):
        p = jnp.exp2(lg - m).astype(BF16)
        return jnp.where(picked, jnp.dot(vt_ref[hh, j], p, preferred_element_type=F32), 0.0)

    jrow = lax.broadcasted_iota(jnp.int32, (nb, blk), 0)
    key = lax.broadcasted_iota(jnp.int32, (blk, blk), 0)
    qry = lax.broadcasted_iota(jnp.int32, (blk, blk), 1)
    j_prev = jnp.maximum(i - 1, 0)
    qs, m0, acc0 = [], [], []
    for hh in heads:
        q_h = (_rms_rows(q_ref[0, :, lanes(hh)], qn_ref[...]) * (hd ** -0.5 * LOG2E)).astype(BF16)
        qs.append(q_h)
        score = jnp.where(jrow < i, _dot_nt(kmean_ref[hh], q_h), -jnp.inf)
        beats = jnp.zeros((nb, blk), jnp.int32)
        for j2 in range(nb):
            s2 = score[j2:j2 + 1, :]
            beats = beats + ((s2 > score) | ((s2 == score) & (j2 < jrow))).astype(jnp.int32)
        pick_ref[hh] = ((beats < C_TOPK) & (jrow < i)).astype(F32)
        lg_own = jnp.where(key <= qry, logits(hh, q_h, i) + bias_ref[hh, 0:blk, :], NEG_BIG)
        pick_prev = pick_ref[hh, pl.ds(j_prev, 1), :] > 0.5
        lg_prev = logits(hh, q_h, j_prev) + bias_ref[hh, blk:2 * blk, :]
        m_h = jnp.maximum(jnp.max(lg_own, axis=0, keepdims=True), col_max(lg_prev, pick_prev))
        m0.append(m_h)
        acc0.append(weighted_v(hh, i, lg_own, m_h, True) + weighted_v(hh, j_prev, lg_prev, m_h, pick_prev))

    n_far = jnp.maximum(i - 1, 0)

    def body(g, carry):
        ms, accs = carry
        new_ms, new_accs = [], []
        for hh in heads:
            tiles = []
            m_new = ms[hh]
            for u in range(grp):
                j = g * grp + u
                jc = jnp.minimum(j, nb - 1)
                lg = logits(hh, qs[hh], jc)
                picked = (pick_ref[hh, pl.ds(jc, 1), :] > 0.5) & (j < n_far)
                m_new = jnp.maximum(m_new, col_max(lg, picked))
                tiles.append((jc, lg, picked))
            acc = accs[hh] * jnp.exp2(ms[hh] - m_new)
            for jc, lg, picked in tiles:
                acc = acc + weighted_v(hh, jc, lg, m_new, picked)
            new_ms.append(m_new)
            new_accs.append(acc)
        return tuple(new_ms), tuple(new_accs)

    _, accs = lax.fori_loop(0, (n_far + grp - 1) // grp, body, (tuple(m0), tuple(acc0)))
    for hh in heads:
        o_ref[0, :, lanes(hh)] = (accs[hh][0:hd] / accs[hh][hd:hd + 1]).T.astype(o_ref.dtype)


def _moba(q, k, v, bias_tiles, q_norm, k_norm):
    bsz, s, _ = q.shape
    blk, hd = C_BLOCK, C_HEAD_DIM
    assert s % blk == 0
    nb = s // blk
    hb = MOBA_HEADS_PER_STEP
    assert C_HEADS % hb == 0
    return pl.pallas_call(
        _moba_kernel,
        grid=(bsz, C_HEADS // hb, nb),
        in_specs=[pl.BlockSpec((1, blk, hb * hd), lambda b, h, i: (b, i, h)),
                  pl.BlockSpec((1, s, hb * hd), lambda b, h, i: (b, 0, h)),
                  pl.BlockSpec((1, s, hb * hd), lambda b, h, i: (b, 0, h)),
                  pl.BlockSpec((hb, 2 * blk, blk), lambda b, h, i: (h, 0, 0)),
                  pl.BlockSpec((1, hd), lambda b, h, i: (0, 0)),
                  pl.BlockSpec((1, hd), lambda b, h, i: (0, 0))],
        out_specs=pl.BlockSpec((1, blk, hb * hd), lambda b, h, i: (b, i, h)),
        out_shape=jax.ShapeDtypeStruct((bsz, s, C_HEADS * hd), BF16),
        scratch_shapes=[pltpu.VMEM((s, hb * hd), BF16), pltpu.VMEM((hb, nb, MOBA_VT_ROWS, blk), BF16),
                        pltpu.VMEM((hb, nb, hd), F32), pltpu.VMEM((hb, nb, blk), F32)],
        compiler_params=_cparams("arbitrary", "arbitrary", "arbitrary"),
        name="moba_attention",
    )(q, k, v, bias_tiles, q_norm.reshape(1, -1), k_norm.reshape(1, -1))


def _router_kernel(x_ref, nw_ref, mod_ref, wr_ref, br_ref, h_ref, ids_ref, gate_ref, rank_ref, cnt_ref,
                   base_ref):
    first = (pl.program_id(0) == 0) & (pl.program_id(1) == 0)

    @pl.when(first)
    def _():
        base_ref[...] = jnp.zeros(base_ref.shape, F32)

    x = x_ref[0]
    tm = x.shape[0]
    h = _rms_rows(x, nw_ref[...])
    h = h * (1.0 + mod_ref[0, 4:5, :]) + mod_ref[0, 3:4, :]
    h_ref[0] = h
    lt = _dot_nt(wr_ref[...], h) + br_ref[...]
    g = [lt[r:r + 1, :] for r in range(MOE_GROUPS)]
    gmax = functools.reduce(jnp.maximum, g)
    gsel = jnp.full(gmax.shape, MOE_GROUPS - 1, jnp.int32)
    for r in range(MOE_GROUPS - 2, -1, -1):
        gsel = jnp.where(g[r] == gmax, r, gsel)
    p_g = 1.0 / functools.reduce(jnp.add, [jnp.exp(gr - gmax) for gr in g])
    epg = MOE_EXPERTS_PER_GROUP
    e_in = lt[8 + (MOE_GROUPS - 1) * epg:8 + MOE_GROUPS * epg, :]
    for r in range(MOE_GROUPS - 2, -1, -1):
        e_in = jnp.where(gsel == r, lt[8 + r * epg:8 + (r + 1) * epg, :], e_in)
    sub = lax.broadcasted_iota(jnp.int32, (epg, tm), 0)
    v1 = jnp.max(e_in, axis=0, keepdims=True)
    i1 = jnp.min(jnp.where(e_in == v1, sub, epg), axis=0, keepdims=True)
    e2 = jnp.where(sub == i1, -jnp.inf, e_in)
    v2 = jnp.max(e2, axis=0, keepdims=True)
    i2 = jnp.min(jnp.where(e2 == v2, sub, epg), axis=0, keepdims=True)
    t = jnp.exp(v2 - v1)
    w1 = p_g / (1.0 + t)
    w2 = p_g * t / (1.0 + t)
    id1 = gsel * epg + i1
    id2 = gsel * epg + i2
    ids_ref[...] = jnp.concatenate([id1, id2], axis=0)
    gate_ref[...] = jnp.concatenate([w1, w2, jnp.zeros((LANES - 2, tm), F32)], axis=0).T

    eidx = lax.broadcasted_iota(jnp.int32, (MOE_EXPERTS, tm), 0)
    oh1 = eidx == id1
    oh2 = eidx == id2
    onehot = (oh1 | oh2).astype(BF16)
    tr = lax.broadcasted_iota(jnp.int32, (tm, tm), 0)
    tc = lax.broadcasted_iota(jnp.int32, (tm, tm), 1)
    before = (tr < tc).astype(BF16)
    prefix = jnp.dot(onehot, before, preferred_element_type=F32) + base_ref[...]
    r1 = jnp.sum(jnp.where(oh1, prefix, 0.0), axis=0, keepdims=True)
    r2 = jnp.sum(jnp.where(oh2, prefix, 0.0), axis=0, keepdims=True)
    rank_ref[...] = jnp.concatenate([r1, r2], axis=0).astype(jnp.int32)
    base_ref[...] = base_ref[...] + jnp.sum(onehot.astype(F32), axis=1, keepdims=True)
    cnt_ref[...] = base_ref[...].astype(jnp.int32)


def _route(x, norm_w, mod, wr_t, br):
    bsz, s, d = x.shape
    tm = min(ROUTER_ROWS, s)
    n = bsz * s
    nt = s // tm
    tok = lambda b, i: (0, b * nt + i)
    return pl.pallas_call(
        _router_kernel,
        grid=(bsz, nt),
        in_specs=[pl.BlockSpec((1, tm, d), lambda b, i: (b, i, 0)),
                  pl.BlockSpec((1, d), lambda b, i: (0, 0)),
                  pl.BlockSpec((1, 6, d), lambda b, i: (b, 0, 0)),
                  pl.BlockSpec((ROUTER_LOGIT_ROWS, d), lambda b, i: (0, 0)),
                  pl.BlockSpec((ROUTER_LOGIT_ROWS, 1), lambda b, i: (0, 0))],
        out_specs=[pl.BlockSpec((1, tm, d), lambda b, i: (b, i, 0)),
                   pl.BlockSpec((MOE_TOPK, tm), tok),
                   pl.BlockSpec((tm, LANES), lambda b, i: (b * nt + i, 0)),
                   pl.BlockSpec((MOE_TOPK, tm), tok),
                   pl.BlockSpec((MOE_EXPERTS, 1), lambda b, i: (0, 0))],
        out_shape=[jax.ShapeDtypeStruct((bsz, s, d), F32),
                   jax.ShapeDtypeStruct((MOE_TOPK, n), jnp.int32),
                   jax.ShapeDtypeStruct((n, LANES), F32),
                   jax.ShapeDtypeStruct((MOE_TOPK, n), jnp.int32),
                   jax.ShapeDtypeStruct((MOE_EXPERTS, 1), jnp.int32)],
        scratch_shapes=[pltpu.VMEM((MOE_EXPERTS, 1), F32)],
        compiler_params=_cparams("arbitrary", "arbitrary"),
        name="moe_route",
    )(x, norm_w.reshape(1, d), mod, wr_t, br)


def _dest_kernel(ids_ref, rank_ref, start_ref, o_ref):
    ids = ids_ref[...]
    tm = ids.shape[1]
    eidx = lax.broadcasted_iota(jnp.int32, (MOE_EXPERTS, tm), 0)
    rows = []
    for k in range(MOE_TOPK):
        base = jnp.sum(jnp.where(eidx == ids[k:k + 1, :], start_ref[...], 0), axis=0, keepdims=True)
        rows.append(base + rank_ref[k:k + 1, :])
    o_ref[...] = jnp.concatenate(rows, axis=0)


def _dest_rows(ids, rank, start):
    n = ids.shape[1]
    tm = min(2048, n)
    return pl.pallas_call(
        _dest_kernel,
        grid=(n // tm,),
        in_specs=[pl.BlockSpec((MOE_TOPK, tm), lambda i: (0, i)),
                  pl.BlockSpec((MOE_TOPK, tm), lambda i: (0, i)),
                  pl.BlockSpec((MOE_EXPERTS, 1), lambda i: (0, 0))],
        out_specs=pl.BlockSpec((MOE_TOPK, tm), lambda i: (0, i)),
        out_shape=jax.ShapeDtypeStruct((MOE_TOPK, n), jnp.int32),
        compiler_params=_cparams("arbitrary"),
        name="moe_dest_rows",
    )(ids, rank, start)


def _row_copy(src, dst, sem, src_row, dst_row):
    return pltpu.make_async_copy(src.at[pl.ds(src_row, 1)], dst.at[pl.ds(dst_row, 1)], sem)


def _dispatch_kernel(pend_ref, padded_ref, dest_ref, h_ref, xs_ref, zero_ref, sem):
    tm = h_ref.shape[0]
    zrows = zero_ref.shape[0]

    @pl.when(pl.program_id(0) == 0)
    def _():
        zero_ref[...] = jnp.zeros(zero_ref.shape, F32)
        for e in range(MOE_EXPERTS):
            @pl.when(padded_ref[e] > 0)
            def _():
                row0 = pl.multiple_of(pend_ref[e] - zrows, zrows)
                cp = pltpu.make_async_copy(zero_ref, xs_ref.at[pl.ds(row0, zrows)], sem)
                cp.start()
                cp.wait()

        def zero_tail(b, _):
            cp = pltpu.make_async_copy(zero_ref, xs_ref.at[pl.ds(pl.multiple_of(b * zrows, zrows), zrows)], sem)
            cp.start()
            cp.wait()
            return 0

        lax.fori_loop(pend_ref[MOE_EXPERTS - 1] // zrows, xs_ref.shape[0] // zrows, zero_tail, 0)

    def start(t, _):
        for k in range(MOE_TOPK):
            _row_copy(h_ref, xs_ref, sem, t, dest_ref[k, t]).start()
        return 0

    lax.fori_loop(0, tm, start, 0, unroll=DMA_ISSUE_UNROLL)
    for k in range(MOE_TOPK):
        pltpu.make_async_copy(h_ref, xs_ref.at[pl.ds(0, tm)], sem).wait()


def _dispatch(h2d, dest, pend, padded, n_rows):
    n, d = h2d.shape
    tm = min(DISPATCH_ROWS, n)
    return pl.pallas_call(
        _dispatch_kernel,
        grid=(n // tm,),
        in_specs=[pl.BlockSpec(memory_space=pltpu.SMEM),
                  pl.BlockSpec(memory_space=pltpu.SMEM),
                  pl.BlockSpec((MOE_TOPK, tm), lambda i: (0, i), memory_space=pltpu.SMEM),
                  pl.BlockSpec((tm, d), lambda i: (i, 0))],
        out_specs=pl.BlockSpec(memory_space=pl.ANY),
        out_shape=jax.ShapeDtypeStruct((n_rows, d), F32),
        scratch_shapes=[pltpu.VMEM((MOE_BLOCK_ROWS, d), F32), pltpu.SemaphoreType.DMA(())],
        compiler_params=_cparams("arbitrary"),
        name="moe_dispatch",
    )(pend, padded, dest, h2d)


def _expert_kernel(blk_e_ref, n_used_ref, x_ref, w1_ref, w3_ref, w2_ref, o_ref, w1b_ref, w3b_ref, w2b_ref):
    i = pl.program_id(0)
    used = i < n_used_ref[0]
    new_expert = (i == 0) | (blk_e_ref[i] != blk_e_ref[jnp.maximum(i - 1, 0)])

    @pl.when(used & new_expert)
    def _():
        w1b_ref[...] = w1_ref[0, 0].astype(BF16)
        w3b_ref[...] = w3_ref[0, 0].astype(BF16)
        w2b_ref[...] = w2_ref[0, 0].astype(BF16)

    @pl.when(used)
    def _():
        x = x_ref[...].astype(BF16)
        a = jnp.dot(x, w1b_ref[...], preferred_element_type=F32)
        g = jnp.dot(x, w3b_ref[...], preferred_element_type=F32)
        o_ref[...] = jnp.dot((a * jax.nn.sigmoid(a) * g).astype(BF16), w2b_ref[...], preferred_element_type=F32)

    @pl.when(jnp.logical_not(used))
    def _():
        o_ref[...] = jnp.zeros(o_ref.shape, F32)


def _expert_ffn(xs, blk_e, n_used, w1, w3, w2, layer):
    n_rows, d = xs.shape
    tm = MOE_BLOCK_ROWS
    hid = w1.shape[3]
    row = lambda i, be, nu: (jnp.minimum(i, nu[0] - 1), 0)
    wsel = lambda i, be, nu: (layer, be[i], 0, 0)
    return pl.pallas_call(
        _expert_kernel,
        grid_spec=pltpu.PrefetchScalarGridSpec(
            num_scalar_prefetch=2,
            grid=(n_rows // tm,),
            in_specs=[pl.BlockSpec((tm, d), row),
                      pl.BlockSpec((1, 1, d, hid), wsel),
                      pl.BlockSpec((1, 1, d, hid), wsel),
                      pl.BlockSpec((1, 1, hid, d), wsel)],
            out_specs=pl.BlockSpec((tm, d), lambda i, be, nu: (i, 0)),
            scratch_shapes=[pltpu.VMEM((d, hid), BF16), pltpu.VMEM((d, hid), BF16),
                            pltpu.VMEM((hid, d), BF16)]),
        out_shape=jax.ShapeDtypeStruct((n_rows, d), F32),
        compiler_params=_cparams("arbitrary"),
        name="moe_expert_ffn",
    )(blk_e, n_used, xs, w1, w3, w2)


def _combine_kernel(dest_ref, ys_ref, gate_ref, x_ref, mod_ref, o_ref, buf_ref, sem):
    tm = x_ref.shape[1]

    def start(t, _):
        for k in range(MOE_TOPK):
            _row_copy(ys_ref, buf_ref.at[k], sem, dest_ref[k, t], t).start()
        return 0

    lax.fori_loop(0, tm, start, 0, unroll=DMA_ISSUE_UNROLL)
    for k in range(MOE_TOPK):
        pltpu.make_async_copy(ys_ref.at[pl.ds(0, tm)], buf_ref.at[k], sem).wait()
    g = gate_ref[...]
    y = g[:, 0:1] * buf_ref[0] + g[:, 1:2] * buf_ref[1]
    o_ref[0] = x_ref[0] + mod_ref[0, 5:6, :] * y


def _combine(ys, dest, gates, x, mod):
    bsz, s, d = x.shape
    tm = min(COMBINE_ROWS, s)
    nt = s // tm
    return pl.pallas_call(
        _combine_kernel,
        grid=(bsz, nt),
        in_specs=[pl.BlockSpec((MOE_TOPK, tm), lambda b, i: (0, b * nt + i), memory_space=pltpu.SMEM),
                  pl.BlockSpec(memory_space=pl.ANY),
                  pl.BlockSpec((tm, LANES), lambda b, i: (b * nt + i, 0)),
                  pl.BlockSpec((1, tm, d), lambda b, i: (b, i, 0)),
                  pl.BlockSpec((1, 6, d), lambda b, i: (b, 0, 0))],
        out_specs=pl.BlockSpec((1, tm, d), lambda b, i: (b, i, 0)),
        out_shape=jax.ShapeDtypeStruct((bsz, s, d), F32),
        scratch_shapes=[pltpu.VMEM((MOE_TOPK, tm, d), F32), pltpu.SemaphoreType.DMA(())],
        compiler_params=_cparams("arbitrary", "arbitrary"),
        name="moe_combine",
    )(dest, ys, gates, x, mod)


def _hier_moe_residual(x, norm_w, mod, w_group, b_group, w_expert, b_expert, w1, w3, w2, layer):
    bsz, s, d = x.shape
    n = bsz * s
    tm = MOE_BLOCK_ROWS
    wr_t = jnp.zeros((ROUTER_LOGIT_ROWS, d), F32).at[0:MOE_GROUPS].set(w_group.T).at[8:].set(w_expert.T)
    br = jnp.zeros((ROUTER_LOGIT_ROWS, 1), F32).at[0:MOE_GROUPS, 0].set(b_group).at[8:, 0].set(b_expert)
    h2, ids, gates, rank, counts = _route(x, norm_w, mod, wr_t.astype(BF16), br)
    counts = counts[:, 0]
    padded = ((counts + tm - 1) // tm) * tm
    pend = jnp.cumsum(padded).astype(jnp.int32)
    start = pend - padded
    n_rows = ((n * MOE_TOPK + tm - 1) // tm) * tm + MOE_EXPERTS * tm
    blk_start = jnp.arange(n_rows // tm, dtype=jnp.int32) * tm
    blk_e = jnp.minimum(jnp.sum((pend[None, :] <= blk_start[:, None]).astype(jnp.int32), axis=1),
                        MOE_EXPERTS - 1)
    n_used = pend[-1:] // tm
    dest = _dest_rows(ids, rank, start.reshape(MOE_EXPERTS, 1))
    xs = _dispatch(h2.reshape(n, d), dest, pend, padded, n_rows)
    ys = _expert_ffn(xs, blk_e, n_used, w1, w3, w2, layer)
    return _combine(ys, dest, gates, x, mod)


def _even_mixer_residual(x, norm_w, mod, w_in, w_out, q_norm, k_norm, sinks, gate_up, gate_bias, out_norm,
                         bias_tiles):
    d = x.shape[2]
    aq = A_HEADS * A_HEAD_DIM
    akv = A_KV_HEADS * A_HEAD_DIM
    bk = B_HEADS * B_KEY_DIM
    bv = B_HEADS * B_VAL_DIM
    n_in = w_in.shape[1]
    w_pad = jnp.zeros((d, n_in - B_GATE_RANK + LANES), F32).at[:, :n_in].set(w_in).astype(BF16)
    widths = (aq, akv, akv, 2 * bk, bv, bv, LANES)
    qa, ka, va, qk, vb, rb, ab = _norm_mod_project(x, norm_w, mod, w_pad, widths, shift_row=0)
    oa = _swa_attention(qa, ka, va, sinks, bias_tiles, q_norm, k_norm)
    gu_pad = jnp.zeros((LANES, bk), F32).at[:B_GATE_RANK].set(gate_up).astype(BF16)
    ob = _gla(qk, vb, rb, ab, gu_pad, gate_bias, out_norm)
    w_out = w_out.astype(BF16)
    return _out_project_residual([oa, ob], [w_out[:aq], w_out[aq:]], x, mod, gate_row=2)


def _odd_mixer_residual(x, norm_w, mod, w_in, w_out, q_norm, k_norm, bias_tiles):
    mix = C_HEADS * C_HEAD_DIM
    q, k, v = _norm_mod_project(x, norm_w, mod, w_in.astype(BF16), (mix, mix, mix), shift_row=0)
    o = _moba(q, k, v, bias_tiles, q_norm, k_norm)
    return _out_project_residual([o], [w_out.astype(BF16)], x, mod, gate_row=2)


def kernel(x, c, rel_bias, ada_w, ada_b, norm1_w, norm2_w, even_w_in, even_w_out, a_q_norm, a_k_norm, a_sinks, b_gate_up, b_gate_bias, b_out_norm, odd_w_in, odd_w_out, c_q_norm, c_k_norm, moe_w_group, moe_b_group, moe_w_expert, moe_b_expert, moe_w1, moe_w3, moe_w2):
    depth = ada_w.shape[0]
    bsz, _, d = x.shape
    bias_tiles = _bias_tiles(rel_bias)
    mod_all = _adaln(c, ada_w, ada_b).reshape(depth, bsz, 6, d)
    for layer in range(depth):
        mod = mod_all[layer]
        j = layer // 2
        if layer % 2 == 0:
            x = _even_mixer_residual(x, norm1_w[layer], mod, even_w_in[j], even_w_out[j], a_q_norm[j],
                                     a_k_norm[j], a_sinks[j], b_gate_up[j], b_gate_bias[j], b_out_norm[j],
                                     bias_tiles)
        else:
            x = _odd_mixer_residual(x, norm1_w[layer], mod, odd_w_in[j], odd_w_out[j], c_q_norm[j],
                                    c_k_norm[j], bias_tiles)
        x = _hier_moe_residual(x, norm2_w[layer], mod, moe_w_group[layer], moe_b_group[layer],
                               moe_w_expert[layer], moe_b_expert[layer], moe_w1, moe_w3, moe_w2, layer)
    return x
```

```python
import functools
import math

import jax
import jax.numpy as jnp
import numpy as np
from jax import lax
from jax.experimental import pallas as pl
from jax.experimental.pallas import tpu as pltpu

RMS_EPS = 1e-6
A_HEADS, A_KV_HEADS, A_HEAD_DIM, A_WINDOW = 8, 2, 64, 128
B_HEADS, B_KEY_DIM, B_VAL_DIM, B_GATE_RANK, B_GATE_TAU, B_CHUNK = 4, 64, 128, 16, 16.0, 64
C_HEADS, C_HEAD_DIM, C_BLOCK, C_TOPK = 8, 128, 256, 3
REL_BUCKETS, REL_MAX_DIST, REL_HEADS = 32, 128, 8
MOE_GROUPS, MOE_EXPERTS_PER_GROUP, MOE_TOPK = 4, 8, 2
MOE_EXPERTS = MOE_GROUPS * MOE_EXPERTS_PER_GROUP

LANES = 128
V7X_VMEM_LIMIT_BYTES = 56 * 1024 * 1024
NEG_BIG = -1e30
PROJ_ROWS = 512
GLA_ROWS = 256
ROUTER_ROWS = 512
MOE_BLOCK_ROWS = 256
DISPATCH_ROWS = 512
COMBINE_ROWS = 256
DMA_ISSUE_UNROLL = 8
ROUTER_LOGIT_ROWS = 8 + MOE_EXPERTS
MOBA_GROUP = 4
MOBA_HEADS_PER_STEP = 4
LOG2E = math.log2(math.e)
SWA_VT_ROWS = A_HEAD_DIM + 16
MOBA_VT_ROWS = C_HEAD_DIM + 16

F32 = jnp.float32
BF16 = jnp.bfloat16


def _cparams(*sem):
    return pltpu.CompilerParams(dimension_semantics=sem, vmem_limit_bytes=V7X_VMEM_LIMIT_BYTES)


def _rel_bucket_np(dist):
    exact = REL_BUCKETS // 2
    d = np.maximum(dist, 0)
    logd = np.log(np.maximum(d, 1).astype(np.float64) / exact) / math.log(REL_MAX_DIST / exact)
    far = np.minimum(exact + (logd * (REL_BUCKETS - exact)).astype(np.int64), REL_BUCKETS - 1)
    return np.where(d < exact, d, far).astype(np.int32)


def _bucket_tiles():
    li = np.arange(C_BLOCK)
    own = _rel_bucket_np(li[None, :] - li[:, None])
    prev = _rel_bucket_np(li[None, :] - li[:, None] + C_BLOCK)
    band = _rel_bucket_np(np.arange(A_WINDOW)[None, :] + A_WINDOW - np.arange(2 * A_WINDOW)[:, None])
    return np.concatenate([own, prev], axis=0), band


def _rms_rows(x, w):
    return x * lax.rsqrt(jnp.mean(x * x, axis=-1, keepdims=True) + RMS_EPS) * w


def _dot(a, b):
    return jnp.dot(a.astype(BF16), b.astype(BF16), preferred_element_type=F32)


def _dot_nt(a, b):
    return lax.dot_general(a.astype(BF16), b.astype(BF16), (((1,), (1,)), ((), ())),
                           preferred_element_type=F32)


def _dot_tn(a, b):
    return lax.dot_general(a.astype(BF16), b.astype(BF16), (((0,), (0,)), ((), ())),
                           preferred_element_type=F32)


def _bias_kernel(rb_ref, bkt_ref, o_ref, *, relative_to_last):
    h = pl.program_id(0)
    bkt = bkt_ref[...]
    acc = jnp.zeros(bkt.shape, F32)
    for b in range(REL_BUCKETS):
        acc = jnp.where(bkt == b, rb_ref[b, h], acc)
    if relative_to_last:
        acc = acc - rb_ref[REL_BUCKETS - 1, h]
    o_ref[0] = acc * LOG2E


def _bias_tile(rel_bias, bkt, relative_to_last):
    rows, cols = bkt.shape
    return pl.pallas_call(
        functools.partial(_bias_kernel, relative_to_last=relative_to_last),
        grid=(REL_HEADS,),
        in_specs=[pl.BlockSpec(memory_space=pltpu.SMEM),
                  pl.BlockSpec((rows, cols), lambda h: (0, 0))],
        out_specs=pl.BlockSpec((1, rows, cols), lambda h: (h, 0, 0)),
        out_shape=jax.ShapeDtypeStruct((REL_HEADS, rows, cols), F32),
        compiler_params=_cparams("arbitrary"),
        name="rel_bias_tiles",
    )(rel_bias, jnp.asarray(bkt))


def _bias_tiles(rel_bias):
    moba_bkt, band_bkt = _bucket_tiles()
    return _bias_tile(rel_bias, moba_bkt, True), _bias_tile(rel_bias, band_bkt, False)


def _adaln_kernel(c_ref, w_ref, b_ref, o_ref):
    c = c_ref[...]
    cond = c * jax.nn.sigmoid(c)
    o_ref[0] = _dot(cond, w_ref[0]) + b_ref[0]


def _adaln(c, ada_w, ada_b):
    depth, d, n6 = ada_w.shape
    bsz = c.shape[0]
    tn = 1536 if n6 % 1536 == 0 else n6
    return pl.pallas_call(
        _adaln_kernel,
        grid=(depth, n6 // tn),
        in_specs=[pl.BlockSpec((bsz, d), lambda l, j: (0, 0)),
                  pl.BlockSpec((1, d, tn), lambda l, j: (l, 0, j)),
                  pl.BlockSpec((1, 1, tn), lambda l, j: (l, 0, j))],
        out_specs=pl.BlockSpec((1, bsz, tn), lambda l, j: (l, 0, j)),
        out_shape=jax.ShapeDtypeStruct((depth, bsz, n6), F32),
        compiler_params=_cparams("arbitrary", "arbitrary"),
        name="adaln_mod",
    )(c, ada_w, ada_b.reshape(depth, 1, n6))


def _head_rms(y, w_ref, hd, post_scale):
    cols = y.shape[1]
    lane = lax.broadcasted_iota(jnp.int32, (1, LANES), 1)
    w = w_ref[...] * post_scale
    out = []
    for c in range(cols // LANES):
        t = y[:, c * LANES:(c + 1) * LANES]
        sq = t * t
        if hd == LANES:
            r = lax.rsqrt(jnp.sum(sq, axis=-1, keepdims=True) * (1.0 / hd) + RMS_EPS)
            out.append(t * r * w)
        else:
            low = lane < hd
            s_lo = jnp.sum(jnp.where(low, sq, 0.0), axis=-1, keepdims=True)
            s_hi = jnp.sum(jnp.where(low, 0.0, sq), axis=-1, keepdims=True)
            r = lax.rsqrt(jnp.where(low, s_lo, s_hi) * (1.0 / hd) + RMS_EPS)
            out.append(t * r * jnp.concatenate([w, w], axis=1))
    return jnp.concatenate(out, axis=1) if len(out) > 1 else out[0]


def _proj_kernel(x_ref, nw_ref, mod_ref, w_ref, *refs, shift_row, outs):
    n_norm = len({o[2] for o in outs if o[1]})
    norm_refs, o_refs = refs[:n_norm], refs[n_norm:]
    x = x_ref[0]
    h = _rms_rows(x, nw_ref[...])
    h = h * (1.0 + mod_ref[0, shift_row + 1:shift_row + 2, :]) + mod_ref[0, shift_row:shift_row + 1, :]
    y = _dot(h, w_ref[...])
    off = 0
    for o_ref, (wd, hd, widx, post) in zip(o_refs, outs):
        t = y[:, off:off + wd]
        if hd:
            t = _head_rms(t, norm_refs[widx], hd, post)
        o_ref[0] = t.astype(o_ref.dtype)
        off += wd


def _norm_mod_project(x, norm_w, mod, w_bf16, outs, dtypes, head_norms, shift_row):
    bsz, s, d = x.shape
    tm = min(PROJ_ROWS, s)
    n = w_bf16.shape[1]
    widths = [o[0] for o in outs]
    assert sum(widths) == n and all(wd % LANES == 0 for wd in widths)
    return pl.pallas_call(
        functools.partial(_proj_kernel, shift_row=shift_row, outs=tuple(outs)),
        grid=(bsz, s // tm),
        in_specs=[pl.BlockSpec((1, tm, d), lambda b, i: (b, i, 0)),
                  pl.BlockSpec((1, d), lambda b, i: (0, 0)),
                  pl.BlockSpec((1, 6, d), lambda b, i: (b, 0, 0)),
                  pl.BlockSpec((d, n), lambda b, i: (0, 0))]
                 + [pl.BlockSpec((1, hn.shape[0]), lambda b, i: (0, 0)) for hn in head_norms],
        out_specs=[pl.BlockSpec((1, tm, wd), lambda b, i: (b, i, 0)) for wd in widths],
        out_shape=[jax.ShapeDtypeStruct((bsz, s, wd), dt) for wd, dt in zip(widths, dtypes)],
        compiler_params=_cparams("arbitrary", "arbitrary"),
        name="norm_mod_project",
    )(x, norm_w.reshape(1, d), mod, w_bf16, *[hn.reshape(1, -1) for hn in head_norms])


def _out_proj_kernel(*refs, n_in, gate_row):
    a_refs = refs[:n_in]
    w_refs = refs[n_in:2 * n_in]
    x_ref, mod_ref, o_ref = refs[2 * n_in:]
    y = _dot(a_refs[0][0], w_refs[0][...])
    for a_ref, w_ref in zip(a_refs[1:], w_refs[1:]):
        y = y + _dot(a_ref[0], w_ref[...])
    o_ref[0] = x_ref[0] + mod_ref[0, gate_row:gate_row + 1, :] * y


def _out_project_residual(acts, weights, x, mod, gate_row):
    bsz, s, d = x.shape
    tm = min(PROJ_ROWS, s)
    n_in = len(acts)
    in_specs = [pl.BlockSpec((1, tm, a.shape[2]), lambda b, i: (b, i, 0)) for a in acts]
    in_specs += [pl.BlockSpec(w.shape, lambda b, i: (0, 0)) for w in weights]
    in_specs += [pl.BlockSpec((1, tm, d), lambda b, i: (b, i, 0)),
                 pl.BlockSpec((1, 6, d), lambda b, i: (b, 0, 0))]
    return pl.pallas_call(
        functools.partial(_out_proj_kernel, n_in=n_in, gate_row=gate_row),
        grid=(bsz, s // tm),
        in_specs=in_specs,
        out_specs=pl.BlockSpec((1, tm, d), lambda b, i: (b, i, 0)),
        out_shape=jax.ShapeDtypeStruct((bsz, s, d), F32),
        compiler_params=_cparams("arbitrary", "arbitrary"),
        name="out_project_residual",
    )(*acts, *weights, x, mod)


def _swa_kernel(sink_ref, q_ref, kc_ref, kp_ref, vc_ref, vp_ref, bias_ref, o_ref):
    n = pl.program_id(1)
    w = A_WINDOW
    hd = A_HEAD_DIM
    group = A_HEADS // A_KV_HEADS
    q = q_ref[0]
    k = jnp.concatenate([kp_ref[0], kc_ref[0]], axis=0)
    v = jnp.concatenate([vp_ref[0], vc_ref[0]], axis=0)
    v_t = v.astype(F32).T
    ones_rows = (lax.broadcasted_iota(jnp.int32, (SWA_VT_ROWS - hd, 2 * w), 0) == 0).astype(BF16)
    key = lax.broadcasted_iota(jnp.int32, (2 * w, group * w), 0)
    qry = lax.broadcasted_iota(jnp.int32, (2 * w, group * w), 1) % w
    dist = qry + w - key
    mask = (dist >= 0) & (dist < w) & ((n > 0) | (key >= w))
    head_of_lane = lax.broadcasted_iota(jnp.int32, (1, group * w), 1) // w
    for kv in range(A_KV_HEADS):
        k_g = k[:, kv * hd:(kv + 1) * hd]
        vt_g = jnp.concatenate([v_t[kv * hd:(kv + 1) * hd, :].astype(BF16), ones_rows], axis=0)
        h0 = kv * group
        q_g = jnp.concatenate([q[:, (h0 + g) * hd:(h0 + g + 1) * hd] for g in range(group)], axis=0)
        bias = jnp.concatenate([bias_ref[h0 + g] for g in range(group)], axis=1)
        lg = jnp.where(mask, _dot_nt(k_g, q_g) + bias, NEG_BIG)
        sink = jnp.full((1, group * w), sink_ref[h0 + group - 1] * LOG2E, F32)
        for g in range(group - 2, -1, -1):
            sink = jnp.where(head_of_lane == g, sink_ref[h0 + g] * LOG2E, sink)
        m = jnp.maximum(jnp.max(lg, axis=0, keepdims=True), sink)
        p = jnp.exp2(lg - m).astype(BF16)
        acc = jnp.dot(vt_g, p, preferred_element_type=F32)
        o_t = acc[0:hd] / (acc[hd:hd + 1] + jnp.exp2(sink - m))
        for pair in range(group // 2):
            two = jnp.concatenate([o_t[:, (2 * pair) * w:(2 * pair + 1) * w],
                                   o_t[:, (2 * pair + 1) * w:(2 * pair + 2) * w]], axis=0)
            c0 = (h0 + 2 * pair) * hd
            o_ref[0, :, c0:c0 + 2 * hd] = two.T.astype(o_ref.dtype)


def _swa_attention(qa, ka, va, sinks, band_bias):
    bsz, s, _ = qa.shape
    w = A_WINDOW
    nb = s // w
    kvw = A_KV_HEADS * A_HEAD_DIM
    cur = lambda b, n: (b, n, 0)
    prev = lambda b, n: (b, jnp.maximum(n - 1, 0), 0)
    return pl.pallas_call(
        _swa_kernel,
        grid=(bsz, nb),
        in_specs=[pl.BlockSpec(memory_space=pltpu.SMEM),
                  pl.BlockSpec((1, w, A_HEADS * A_HEAD_DIM), cur),
                  pl.BlockSpec((1, w, kvw), cur),
                  pl.BlockSpec((1, w, kvw), prev),
                  pl.BlockSpec((1, w, kvw), cur),
                  pl.BlockSpec((1, w, kvw), prev),
                  pl.BlockSpec((REL_HEADS, 2 * w, w), lambda b, n: (0, 0, 0))],
        out_specs=pl.BlockSpec((1, w, A_HEADS * A_HEAD_DIM), cur),
        out_shape=jax.ShapeDtypeStruct((bsz, s, A_HEADS * A_HEAD_DIM), BF16),
        compiler_params=_cparams("arbitrary", "arbitrary"),
        name="swa_sink_attention",
    )(sinks, qa, ka, ka, va, va, band_bias)


def _gla_kernel(qk_ref, v_ref, r_ref, ab_ref, gu_ref, gb_ref, on_ref, o_ref, state_ref):
    s_idx = pl.program_id(1)
    c_len = B_CHUNK
    dk, dv = B_KEY_DIM, B_VAL_DIM
    hk = B_HEADS * dk

    @pl.when(s_idx == 0)
    def _():
        state_ref[...] = jnp.zeros(state_ref.shape, F32)

    z = _dot(ab_ref[0], gu_ref[...]) + gb_ref[...]
    log_a = (jnp.minimum(z, 0.0) - jnp.log(1.0 + jnp.exp(-jnp.abs(z)))) / B_GATE_TAU
    rows = qk_ref.shape[1]
    ri = lax.broadcasted_iota(jnp.int32, (c_len, c_len), 0)
    ci = lax.broadcasted_iota(jnp.int32, (c_len, c_len), 1)
    tril = ri >= ci
    tri = tril.astype(BF16)
    for c in range(rows // c_len):
        sl = slice(c * c_len, (c + 1) * c_len)
        g = log_a[sl]
        g_hi = g.astype(BF16)
        g_lo = (g - g_hi.astype(F32)).astype(BF16)
        b = (jnp.dot(tri, g_hi, preferred_element_type=F32)
             + jnp.dot(tri, g_lo, preferred_element_type=F32))
        b_last = b[c_len - 1:c_len, :]
        q = qk_ref[0, sl, 0:hk] * (dk ** -0.5)
        k = qk_ref[0, sl, hk:2 * hk]
        q_dec = q * jnp.exp(b)
        k_dec = k * jnp.exp(-b)
        k_upd = k * jnp.exp(b_last - b)
        decay = jnp.exp(b_last)
        heads = []
        for h in range(B_HEADS):
            ks = slice(h * dk, (h + 1) * dk)
            v_h = v_ref[0, sl, h * dv:(h + 1) * dv]
            st = state_ref[h]
            att = jnp.where(tril, _dot_nt(q_dec[:, ks], k_dec[:, ks]), 0.0)
            o_h = _dot(att, v_h) + _dot_nt(q_dec[:, ks], st)
            state_ref[h] = st * decay[:, ks] + _dot_tn(v_h, k_upd[:, ks])
            o_h = _rms_rows(o_h, on_ref[...])
            r_h = r_ref[0, sl, h * dv:(h + 1) * dv]
            heads.append(o_h * (r_h * jax.nn.sigmoid(r_h)))
        o_ref[0, sl, :] = jnp.concatenate(heads, axis=-1).astype(o_ref.dtype)


def _gla(qk, vb, rb, ab, gate_up_pad, gate_bias, out_norm):
    bsz, s, _ = qk.shape
    tm = min(GLA_ROWS, s)
    hv = B_HEADS * B_VAL_DIM
    hk = B_HEADS * B_KEY_DIM
    blk = lambda b, i: (b, i, 0)
    const = lambda b, i: (0, 0)
    return pl.pallas_call(
        _gla_kernel,
        grid=(bsz, s // tm),
        in_specs=[pl.BlockSpec((1, tm, 2 * hk), blk),
                  pl.BlockSpec((1, tm, hv), blk),
                  pl.BlockSpec((1, tm, hv), blk),
                  pl.BlockSpec((1, tm, LANES), blk),
                  pl.BlockSpec((LANES, hk), const),
                  pl.BlockSpec((1, hk), const),
                  pl.BlockSpec((1, B_VAL_DIM), const)],
        out_specs=pl.BlockSpec((1, tm, hv), blk),
        out_shape=jax.ShapeDtypeStruct((bsz, s, hv), BF16),
        scratch_shapes=[pltpu.VMEM((B_HEADS, B_VAL_DIM, B_KEY_DIM), F32)],
        compiler_params=_cparams("arbitrary", "arbitrary"),
        name="gated_linear_attention",
    )(qk, vb, rb, ab, gate_up_pad, gate_bias.reshape(1, hk), out_norm.reshape(1, -1))


def _moba_kernel(q_ref, k_ref, v_ref, bias_ref, o_ref, vt_ref, kmean_ref, pick_ref):
    i = pl.program_id(2)
    blk = C_BLOCK
    hd = C_HEAD_DIM
    nb = k_ref.shape[1] // blk
    grp = MOBA_GROUP
    heads = range(MOBA_HEADS_PER_STEP)

    def lanes(hh):
        return slice(hh * hd, (hh + 1) * hd)

    @pl.when(i == 0)
    def _():
        ones_rows = (lax.broadcasted_iota(jnp.int32, (MOBA_VT_ROWS - hd, blk), 0) == 0).astype(BF16)
        for hh in heads:
            kn = k_ref[0, :, lanes(hh)].astype(F32)
            kmean_ref[hh] = jnp.mean(kn.reshape(nb, blk, hd), axis=1)
            for j in range(nb):
                vt = v_ref[0, j * blk:(j + 1) * blk, lanes(hh)].astype(F32).T.astype(BF16)
                vt_ref[hh, j] = jnp.concatenate([vt, ones_rows], axis=0)

    def logits(hh, q_h, j):
        j0 = pl.multiple_of(j * blk, blk)
        return _dot_nt(k_ref[0, pl.ds(j0, blk), lanes(hh)], q_h)

    def col_max(lg, picked):
        return jnp.where(picked, jnp.max(lg, axis=0, keepdims=True), NEG_BIG)

    def weighted_v(hh, j, lg, m, picked):
        p = jnp.exp2(lg - m).astype(BF16)
        return jnp.where(picked, jnp.dot(vt_ref[hh, j], p, preferred_element_type=F32), 0.0)

    def softmax_step(ms, accs, tiles):
        m_new = []
        for hh in heads:
            m_h = ms[hh]
            for _, lg, picked in tiles[hh]:
                m_h = jnp.maximum(m_h, col_max(lg, picked))
            m_new.append(m_h)
        out = []
        for hh in heads:
            acc = accs[hh] * jnp.exp2(ms[hh] - m_new[hh])
            for j, lg, picked in tiles[hh]:
                acc = acc + weighted_v(hh, j, lg, m_new[hh], picked)
            out.append(acc)
        return tuple(m_new), tuple(out)

    jrow = lax.broadcasted_iota(jnp.int32, (nb, blk), 0)
    key = lax.broadcasted_iota(jnp.int32, (blk, blk), 0)
    qry = lax.broadcasted_iota(jnp.int32, (blk, blk), 1)
    j_prev = jnp.maximum(i - 1, 0)
    qs = []
    for hh in heads:
        q_h = q_ref[0, :, lanes(hh)]
        qs.append(q_h)
        score = jnp.where(jrow < i, _dot_nt(kmean_ref[hh], q_h), -jnp.inf)
        beats = jnp.zeros((nb, blk), jnp.int32)
        for j2 in range(nb):
            s2 = score[j2:j2 + 1, :]
            beats = beats + ((s2 > score) | ((s2 == score) & (j2 < jrow))).astype(jnp.int32)
        pick_ref[hh] = ((beats < C_TOPK) & (jrow < i)).astype(F32)

    first = []
    for hh in heads:
        lg_own = jnp.where(key <= qry, logits(hh, qs[hh], i) + bias_ref[hh, 0:blk, :], NEG_BIG)
        lg_prev = logits(hh, qs[hh], j_prev) + bias_ref[hh, blk:2 * blk, :]
        pick_prev = pick_ref[hh, pl.ds(j_prev, 1), :] > 0.5
        first.append([(i, lg_own, True), (j_prev, lg_prev, pick_prev)])
    start = (tuple(jnp.full((1, blk), NEG_BIG, F32) for _ in heads),
             tuple(jnp.zeros((MOBA_VT_ROWS, blk), F32) for _ in heads))
    carry0 = softmax_step(start[0], start[1], first)

    n_far = jnp.maximum(i - 1, 0)

    def body(g, carry):
        tiles = []
        for hh in heads:
            row = []
            for u in range(grp):
                j = g * grp + u
                jc = jnp.minimum(j, nb - 1)
                picked = (pick_ref[hh, pl.ds(jc, 1), :] > 0.5) & (j < n_far)
                row.append((jc, logits(hh, qs[hh], jc), picked))
            tiles.append(row)
        return softmax_step(carry[0], carry[1], tiles)

    _, accs = lax.fori_loop(0, (n_far + grp - 1) // grp, body, carry0)
    for hh in heads:
        o_ref[0, :, lanes(hh)] = (accs[hh][0:hd] / accs[hh][hd:hd + 1]).T.astype(o_ref.dtype)


def _moba(q, k, v, bias_tiles):
    bsz, s, _ = q.shape
    blk, hd = C_BLOCK, C_HEAD_DIM
    assert s % blk == 0
    nb = s // blk
    hb = MOBA_HEADS_PER_STEP
    assert C_HEADS % hb == 0
    return pl.pallas_call(
        _moba_kernel,
        grid=(bsz, C_HEADS // hb, nb),
        in_specs=[pl.BlockSpec((1, blk, hb * hd), lambda b, h, i: (b, i, h)),
                  pl.BlockSpec((1, s, hb * hd), lambda b, h, i: (b, 0, h)),
                  pl.BlockSpec((1, s, hb * hd), lambda b, h, i: (b, 0, h)),
                  pl.BlockSpec((hb, 2 * blk, blk), lambda b, h, i: (h, 0, 0))],
        out_specs=pl.BlockSpec((1, blk, hb * hd), lambda b, h, i: (b, i, h)),
        out_shape=jax.ShapeDtypeStruct((bsz, s, C_HEADS * hd), BF16),
        scratch_shapes=[pltpu.VMEM((hb, nb, MOBA_VT_ROWS, blk), BF16),
                        pltpu.VMEM((hb, nb, hd), F32), pltpu.VMEM((hb, nb, blk), F32)],
        compiler_params=_cparams("arbitrary", "arbitrary", "arbitrary"),
        name="moba_attention",
    )(q, k, v, bias_tiles)


def _router_kernel(x_ref, nw_ref, mod_ref, wr_ref, br_ref, h_ref, ids_ref, gate_ref, rank_ref, cnt_ref,
                   base_ref):
    first = (pl.program_id(0) == 0) & (pl.program_id(1) == 0)

    @pl.when(first)
    def _():
        base_ref[...] = jnp.zeros(base_ref.shape, F32)

    x = x_ref[0]
    tm = x.shape[0]
    h = _rms_rows(x, nw_ref[...])
    h = h * (1.0 + mod_ref[0, 4:5, :]) + mod_ref[0, 3:4, :]
    h_ref[0] = h
    lt = _dot_nt(wr_ref[...], h) + br_ref[...]
    g = [lt[r:r + 1, :] for r in range(MOE_GROUPS)]
    gmax = functools.reduce(jnp.maximum, g)
    gsel = jnp.full(gmax.shape, MOE_GROUPS - 1, jnp.int32)
    for r in range(MOE_GROUPS - 2, -1, -1):
        gsel = jnp.where(g[r] == gmax, r, gsel)
    p_g = 1.0 / functools.reduce(jnp.add, [jnp.exp(gr - gmax) for gr in g])
    epg = MOE_EXPERTS_PER_GROUP
    e_in = lt[8 + (MOE_GROUPS - 1) * epg:8 + MOE_GROUPS * epg, :]
    for r in range(MOE_GROUPS - 2, -1, -1):
        e_in = jnp.where(gsel == r, lt[8 + r * epg:8 + (r + 1) * epg, :], e_in)
    sub = lax.broadcasted_iota(jnp.int32, (epg, tm), 0)
    v1 = jnp.max(e_in, axis=0, keepdims=True)
    i1 = jnp.min(jnp.where(e_in == v1, sub, epg), axis=0, keepdims=True)
    e2 = jnp.where(sub == i1, -jnp.inf, e_in)
    v2 = jnp.max(e2, axis=0, keepdims=True)
    i2 = jnp.min(jnp.where(e2 == v2, sub, epg), axis=0, keepdims=True)
    t = jnp.exp(v2 - v1)
    w1 = p_g / (1.0 + t)
    w2 = p_g * t / (1.0 + t)
    id1 = gsel * epg + i1
    id2 = gsel * epg + i2
    ids_ref[...] = jnp.concatenate([id1, id2], axis=0)
    gate_ref[...] = jnp.concatenate([w1, w2, jnp.zeros((LANES - 2, tm), F32)], axis=0).T

    eidx = lax.broadcasted_iota(jnp.int32, (MOE_EXPERTS, tm), 0)
    oh1 = eidx == id1
    oh2 = eidx == id2
    onehot = (oh1 | oh2).astype(BF16)
    tr = lax.broadcasted_iota(jnp.int32, (tm, tm), 0)
    tc = lax.broadcasted_iota(jnp.int32, (tm, tm), 1)
    before = (tr < tc).astype(BF16)
    prefix = jnp.dot(onehot, before, preferred_element_type=F32) + base_ref[...]
    r1 = jnp.sum(jnp.where(oh1, prefix, 0.0), axis=0, keepdims=True)
    r2 = jnp.sum(jnp.where(oh2, prefix, 0.0), axis=0, keepdims=True)
    rank_ref[...] = jnp.concatenate([r1, r2], axis=0).astype(jnp.int32)
    base_ref[...] = base_ref[...] + jnp.sum(onehot.astype(F32), axis=1, keepdims=True)
    cnt_ref[...] = base_ref[...].astype(jnp.int32)


def _route(x, norm_w, mod, wr_t, br):
    bsz, s, d = x.shape
    tm = min(ROUTER_ROWS, s)
    n = bsz * s
    nt = s // tm
    tok = lambda b, i: (0, b * nt + i)
    return pl.pallas_call(
        _router_kernel,
        grid=(bsz, nt),
        in_specs=[pl.BlockSpec((1, tm, d), lambda b, i: (b, i, 0)),
                  pl.BlockSpec((1, d), lambda b, i: (0, 0)),
                  pl.BlockSpec((1, 6, d), lambda b, i: (b, 0, 0)),
                  pl.BlockSpec((ROUTER_LOGIT_ROWS, d), lambda b, i: (0, 0)),
                  pl.BlockSpec((ROUTER_LOGIT_ROWS, 1), lambda b, i: (0, 0))],
        out_specs=[pl.BlockSpec((1, tm, d), lambda b, i: (b, i, 0)),
                   pl.BlockSpec((MOE_TOPK, tm), tok),
                   pl.BlockSpec((tm, LANES), lambda b, i: (b * nt + i, 0)),
                   pl.BlockSpec((MOE_TOPK, tm), tok),
                   pl.BlockSpec((MOE_EXPERTS, 1), lambda b, i: (0, 0))],
        out_shape=[jax.ShapeDtypeStruct((bsz, s, d), F32),
                   jax.ShapeDtypeStruct((MOE_TOPK, n), jnp.int32),
                   jax.ShapeDtypeStruct((n, LANES), F32),
                   jax.ShapeDtypeStruct((MOE_TOPK, n), jnp.int32),
                   jax.ShapeDtypeStruct((MOE_EXPERTS, 1), jnp.int32)],
        scratch_shapes=[pltpu.VMEM((MOE_EXPERTS, 1), F32)],
        compiler_params=_cparams("arbitrary", "arbitrary"),
        name="moe_route",
    )(x, norm_w.reshape(1, d), mod, wr_t, br)


def _dest_kernel(ids_ref, rank_ref, start_ref, o_ref):
    ids = ids_ref[...]
    tm = ids.shape[1]
    eidx = lax.broadcasted_iota(jnp.int32, (MOE_EXPERTS, tm), 0)
    rows = []
    for k in range(MOE_TOPK):
        base = jnp.sum(jnp.where(eidx == ids[k:k + 1, :], start_ref[...], 0), axis=0, keepdims=True)
        rows.append(base + rank_ref[k:k + 1, :])
    o_ref[...] = jnp.concatenate(rows, axis=0)


def _dest_rows(ids, rank, start):
    n = ids.shape[1]
    tm = min(2048, n)
    return pl.pallas_call(
        _dest_kernel,
        grid=(n // tm,),
        in_specs=[pl.BlockSpec((MOE_TOPK, tm), lambda i: (0, i)),
                  pl.BlockSpec((MOE_TOPK, tm), lambda i: (0, i)),
                  pl.BlockSpec((MOE_EXPERTS, 1), lambda i: (0, 0))],
        out_specs=pl.BlockSpec((MOE_TOPK, tm), lambda i: (0, i)),
        out_shape=jax.ShapeDtypeStruct((MOE_TOPK, n), jnp.int32),
        compiler_params=_cparams("arbitrary"),
        name="moe_dest_rows",
    )(ids, rank, start)


def _row_copy(src, dst, sem, src_row, dst_row):
    return pltpu.make_async_copy(src.at[pl.ds(src_row, 1)], dst.at[pl.ds(dst_row, 1)], sem)


def _dispatch_kernel(pend_ref, padded_ref, dest_ref, h_ref, xs_ref, zero_ref, sem):
    tm = h_ref.shape[0]
    zrows = zero_ref.shape[0]

    @pl.when(pl.program_id(0) == 0)
    def _():
        zero_ref[...] = jnp.zeros(zero_ref.shape, F32)
        for e in range(MOE_EXPERTS):
            @pl.when(padded_ref[e] > 0)
            def _():
                row0 = pl.multiple_of(pend_ref[e] - zrows, zrows)
                cp = pltpu.make_async_copy(zero_ref, xs_ref.at[pl.ds(row0, zrows)], sem)
                cp.start()
                cp.wait()

        def zero_tail(b, _):
            cp = pltpu.make_async_copy(zero_ref, xs_ref.at[pl.ds(pl.multiple_of(b * zrows, zrows), zrows)], sem)
            cp.start()
            cp.wait()
            return 0

        lax.fori_loop(pend_ref[MOE_EXPERTS - 1] // zrows, xs_ref.shape[0] // zrows, zero_tail, 0)

    def start(t, _):
        for k in range(MOE_TOPK):
            _row_copy(h_ref, xs_ref, sem, t, dest_ref[k, t]).start(priority=k)
        return 0

    lax.fori_loop(0, tm, start, 0, unroll=DMA_ISSUE_UNROLL)
    for k in range(MOE_TOPK):
        pltpu.make_async_copy(h_ref, xs_ref.at[pl.ds(0, tm)], sem).wait()


def _dispatch(h2d, dest, pend, padded, n_rows):
    n, d = h2d.shape
    tm = min(DISPATCH_ROWS, n)
    return pl.pallas_call(
        _dispatch_kernel,
        grid=(n // tm,),
        in_specs=[pl.BlockSpec(memory_space=pltpu.SMEM),
                  pl.BlockSpec(memory_space=pltpu.SMEM),
                  pl.BlockSpec((MOE_TOPK, tm), lambda i: (0, i), memory_space=pltpu.SMEM),
                  pl.BlockSpec((tm, d), lambda i: (i, 0))],
        out_specs=pl.BlockSpec(memory_space=pl.ANY),
        out_shape=jax.ShapeDtypeStruct((n_rows, d), F32),
        scratch_shapes=[pltpu.VMEM((MOE_BLOCK_ROWS, d), F32), pltpu.SemaphoreType.DMA(())],
        compiler_params=_cparams("arbitrary"),
        name="moe_dispatch",
    )(pend, padded, dest, h2d)


def _expert_kernel(blk_e_ref, n_used_ref, x_ref, w1_ref, w3_ref, w2_ref, o_ref, w1b_ref, w3b_ref, w2b_ref):
    i = pl.program_id(0)
    used = i < n_used_ref[0]
    new_expert = (i == 0) | (blk_e_ref[i] != blk_e_ref[jnp.maximum(i - 1, 0)])

    @pl.when(used & new_expert)
    def _():
        w1b_ref[...] = w1_ref[0, 0].astype(BF16)
        w3b_ref[...] = w3_ref[0, 0].astype(BF16)
        w2b_ref[...] = w2_ref[0, 0].astype(BF16)

    @pl.when(used)
    def _():
        x = x_ref[...].astype(BF16)
        a = jnp.dot(x, w1b_ref[...], preferred_element_type=F32)
        g = jnp.dot(x, w3b_ref[...], preferred_element_type=F32)
        o_ref[...] = jnp.dot((a * jax.nn.sigmoid(a) * g).astype(BF16), w2b_ref[...], preferred_element_type=F32)

    @pl.when(jnp.logical_not(used))
    def _():
        o_ref[...] = jnp.zeros(o_ref.shape, F32)


def _expert_ffn(xs, blk_e, n_used, w1, w3, w2, layer):
    n_rows, d = xs.shape
    tm = MOE_BLOCK_ROWS
    hid = w1.shape[3]
    row = lambda i, be, nu: (jnp.minimum(i, nu[0] - 1), 0)
    wsel = lambda i, be, nu: (layer, be[i], 0, 0)
    return pl.pallas_call(
        _expert_kernel,
        grid_spec=pltpu.PrefetchScalarGridSpec(
            num_scalar_prefetch=2,
            grid=(n_rows // tm,),
            in_specs=[pl.BlockSpec((tm, d), row),
                      pl.BlockSpec((1, 1, d, hid), wsel),
                      pl.BlockSpec((1, 1, d, hid), wsel),
                      pl.BlockSpec((1, 1, hid, d), wsel)],
            out_specs=pl.BlockSpec((tm, d), lambda i, be, nu: (i, 0)),
            scratch_shapes=[pltpu.VMEM((d, hid), BF16), pltpu.VMEM((d, hid), BF16),
                            pltpu.VMEM((hid, d), BF16)]),
        out_shape=jax.ShapeDtypeStruct((n_rows, d), F32),
        compiler_params=_cparams("arbitrary"),
        name="moe_expert_ffn",
    )(blk_e, n_used, xs, w1, w3, w2)


def _combine_kernel(dest_ref, ys_ref, gate_ref, x_ref, mod_ref, o_ref, buf_ref, sem):
    tm = x_ref.shape[1]

    def start(t, _):
        for k in range(MOE_TOPK):
            _row_copy(ys_ref, buf_ref.at[k], sem, dest_ref[k, t], t).start(priority=k)
        return 0

    lax.fori_loop(0, tm, start, 0, unroll=DMA_ISSUE_UNROLL)
    for k in range(MOE_TOPK):
        pltpu.make_async_copy(ys_ref.at[pl.ds(0, tm)], buf_ref.at[k], sem).wait()
    g = gate_ref[...]
    y = g[:, 0:1] * buf_ref[0] + g[:, 1:2] * buf_ref[1]
    o_ref[0] = x_ref[0] + mod_ref[0, 5:6, :] * y


def _combine(ys, dest, gates, x, mod):
    bsz, s, d = x.shape
    tm = min(COMBINE_ROWS, s)
    nt = s // tm
    return pl.pallas_call(
        _combine_kernel,
        grid=(bsz, nt),
        in_specs=[pl.BlockSpec((MOE_TOPK, tm), lambda b, i: (0, b * nt + i), memory_space=pltpu.SMEM),
                  pl.BlockSpec(memory_space=pl.ANY),
                  pl.BlockSpec((tm, LANES), lambda b, i: (b * nt + i, 0)),
                  pl.BlockSpec((1, tm, d), lambda b, i: (b, i, 0)),
                  pl.BlockSpec((1, 6, d), lambda b, i: (b, 0, 0))],
        out_specs=pl.BlockSpec((1, tm, d), lambda b, i: (b, i, 0)),
        out_shape=jax.ShapeDtypeStruct((bsz, s, d), F32),
        scratch_shapes=[pltpu.VMEM((MOE_TOPK, tm, d), F32), pltpu.SemaphoreType.DMA(())],
        compiler_params=_cparams("arbitrary", "arbitrary"),
        name="moe_combine",
    )(dest, ys, gates, x, mod)


def _hier_moe_residual(x, norm_w, mod, w_group, b_group, w_expert, b_expert, w1, w3, w2, layer):
    bsz, s, d = x.shape
    n = bsz * s
    tm = MOE_BLOCK_ROWS
    wr_t = jnp.zeros((ROUTER_LOGIT_ROWS, d), F32).at[0:MOE_GROUPS].set(w_group.T).at[8:].set(w_expert.T)
    br = jnp.zeros((ROUTER_LOGIT_ROWS, 1), F32).at[0:MOE_GROUPS, 0].set(b_group).at[8:, 0].set(b_expert)
    h2, ids, gates, rank, counts = _route(x, norm_w, mod, wr_t.astype(BF16), br)
    counts = counts[:, 0]
    padded = ((counts + tm - 1) // tm) * tm
    pend = jnp.cumsum(padded).astype(jnp.int32)
    start = pend - padded
    n_rows = ((n * MOE_TOPK + tm - 1) // tm) * tm + MOE_EXPERTS * tm
    blk_start = jnp.arange(n_rows // tm, dtype=jnp.int32) * tm
    blk_e = jnp.minimum(jnp.sum((pend[None, :] <= blk_start[:, None]).astype(jnp.int32), axis=1),
                        MOE_EXPERTS - 1)
    n_used = pend[-1:] // tm
    dest = _dest_rows(ids, rank, start.reshape(MOE_EXPERTS, 1))
    xs = _dispatch(h2.reshape(n, d), dest, pend, padded, n_rows)
    ys = _expert_ffn(xs, blk_e, n_used, w1, w3, w2, layer)
    return _combine(ys, dest, gates, x, mod)


def _even_mixer_residual(x, norm_w, mod, w_in, w_out, q_norm, k_norm, sinks, gate_up, gate_bias, out_norm,
                         band_bias):
    d = x.shape[2]
    aq = A_HEADS * A_HEAD_DIM
    akv = A_KV_HEADS * A_HEAD_DIM
    bk = B_HEADS * B_KEY_DIM
    bv = B_HEADS * B_VAL_DIM
    n_in = w_in.shape[1]
    w_pad = jnp.zeros((d, n_in - B_GATE_RANK + LANES), F32).at[:, :n_in].set(w_in).astype(BF16)
    outs = ((aq, A_HEAD_DIM, 0, A_HEAD_DIM ** -0.5 * LOG2E), (akv, A_HEAD_DIM, 1, 1.0), (akv, 0, 0, 1.0),
            (2 * bk, 0, 0, 1.0), (bv, 0, 0, 1.0), (bv, 0, 0, 1.0), (LANES, 0, 0, 1.0))
    dtypes = (BF16, BF16, BF16, F32, BF16, F32, F32)
    qa, ka, va, qk, vb, rb, ab = _norm_mod_project(x, norm_w, mod, w_pad, outs, dtypes, (q_norm, k_norm),
                                                   shift_row=0)
    oa = _swa_attention(qa, ka, va, sinks, band_bias)
    gu_pad = jnp.zeros((LANES, bk), F32).at[:B_GATE_RANK].set(gate_up).astype(BF16)
    ob = _gla(qk, vb, rb, ab, gu_pad, gate_bias, out_norm)
    w_out = w_out.astype(BF16)
    return _out_project_residual([oa, ob], [w_out[:aq], w_out[aq:]], x, mod, gate_row=2)


def _odd_mixer_residual(x, norm_w, mod, w_in, w_out, q_norm, k_norm, bias_tiles):
    mix = C_HEADS * C_HEAD_DIM
    outs = ((mix, C_HEAD_DIM, 0, C_HEAD_DIM ** -0.5 * LOG2E), (mix, C_HEAD_DIM, 1, 1.0), (mix, 0, 0, 1.0))
    q, k, v = _norm_mod_project(x, norm_w, mod, w_in.astype(BF16), outs, (BF16, BF16, BF16), (q_norm, k_norm),
                                shift_row=0)
    o = _moba(q, k, v, bias_tiles)
    return _out_project_residual([o], [w_out.astype(BF16)], x, mod, gate_row=2)


def kernel(x, c, rel_bias, ada_w, ada_b, norm1_w, norm2_w, even_w_in, even_w_out, a_q_norm, a_k_norm, a_sinks, b_gate_up, b_gate_bias, b_out_norm, odd_w_in, odd_w_out, c_q_norm, c_k_norm, moe_w_group, moe_b_group, moe_w_expert, moe_b_expert, moe_w1, moe_w3, moe_w2):
    depth = ada_w.shape[0]
    bsz, _, d = x.shape
    moba_bias, band_bias = _bias_tiles(rel_bias)
    mod_all = _adaln(c, ada_w, ada_b).reshape(depth, bsz, 6, d)
    for layer in range(depth):
        mod = mod_all[layer]
        j = layer // 2
        if layer % 2 == 0:
            x = _even_mixer_residual(x, norm1_w[layer], mod, even_w_in[j], even_w_out[j], a_q_norm[j],
                                     a_k_norm[j], a_sinks[j], b_gate_up[j], b_gate_bias[j], b_out_norm[j],
                                     band_bias)
        else:
            x = _odd_mixer_residual(x, norm1_w[layer], mod, odd_w_in[j], odd_w_out[j], c_q_norm[j],
                                    c_k_norm[j], moba_bias)
        x = _hier_moe_residual(x, norm2_w[layer], mod, moe_w_group[layer], moe_b_group[layer],
                               moe_w_expert[layer], moe_b_expert[layer], moe_w1, moe_w3, moe_w2, layer)
    return x
```

```python
import functools
import math

import jax
import jax.numpy as jnp
import numpy as np
from jax import lax
from jax.experimental import pallas as pl
from jax.experimental.pallas import tpu as pltpu

RMS_EPS = 1e-6
A_HEADS, A_KV_HEADS, A_HEAD_DIM, A_WINDOW = 8, 2, 64, 128
B_HEADS, B_KEY_DIM, B_VAL_DIM, B_GATE_RANK, B_GATE_TAU, B_CHUNK = 4, 64, 128, 16, 16.0, 64
C_HEADS, C_HEAD_DIM, C_BLOCK, C_TOPK = 8, 128, 256, 3
REL_BUCKETS, REL_MAX_DIST, REL_HEADS = 32, 128, 8
MOE_GROUPS, MOE_EXPERTS_PER_GROUP, MOE_TOPK = 4, 8, 2
MOE_EXPERTS = MOE_GROUPS * MOE_EXPERTS_PER_GROUP

LANES = 128
V7X_VMEM_LIMIT_BYTES = 56 * 1024 * 1024
NEG_BIG = -1e30
PROJ_ROWS = 512
GLA_ROWS = 256
ROUTER_ROWS = 512
MOE_BLOCK_ROWS = 256
DISPATCH_ROWS = 512
COMBINE_ROWS = 256
DMA_ISSUE_UNROLL = 8
ROUTER_LOGIT_ROWS = 8 + MOE_EXPERTS
MOBA_GROUP = 4
MOBA_HEADS_PER_STEP = 4
LOG2E = math.log2(math.e)
SWA_VT_ROWS = A_HEAD_DIM + 16
MOBA_VT_ROWS = C_HEAD_DIM + 16

F32 = jnp.float32
BF16 = jnp.bfloat16


def _cparams(*sem):
    return pltpu.CompilerParams(dimension_semantics=sem, vmem_limit_bytes=V7X_VMEM_LIMIT_BYTES)


def _rel_bucket_np(dist):
    exact = REL_BUCKETS // 2
    d = np.maximum(dist, 0)
    logd = np.log(np.maximum(d, 1).astype(np.float64) / exact) / math.log(REL_MAX_DIST / exact)
    far = np.minimum(exact + (logd * (REL_BUCKETS - exact)).astype(np.int64), REL_BUCKETS - 1)
    return np.where(d < exact, d, far).astype(np.int32)


def _bucket_tiles():
    li = np.arange(C_BLOCK)
    own = _rel_bucket_np(li[None, :] - li[:, None])
    prev = _rel_bucket_np(li[None, :] - li[:, None] + C_BLOCK)
    band = _rel_bucket_np(np.arange(A_WINDOW)[None, :] + A_WINDOW - np.arange(2 * A_WINDOW)[:, None])
    return np.concatenate([own, prev], axis=0), band


def _rms_rows(x, w):
    return x * lax.rsqrt(jnp.mean(x * x, axis=-1, keepdims=True) + RMS_EPS) * w


def _dot(a, b):
    return jnp.dot(a.astype(BF16), b.astype(BF16), preferred_element_type=F32)


def _dot_nt(a, b):
    return lax.dot_general(a.astype(BF16), b.astype(BF16), (((1,), (1,)), ((), ())),
                           preferred_element_type=F32)


def _to_row_tiles(ref, x, lead=()):
    rows, d = x.shape
    chunks = d // LANES
    for c in range(chunks):
        ref[lead + (pl.ds(c, rows, stride=chunks), slice(None))] = x[:, c * LANES:(c + 1) * LANES]


def _from_row_tiles(ref, rows, d, lead=()):
    chunks = d // LANES
    return jnp.concatenate([ref[lead + (pl.ds(c, rows, stride=chunks), slice(None))] for c in range(chunks)],
                           axis=1)


def _dot_tn(a, b):
    return lax.dot_general(a.astype(BF16), b.astype(BF16), (((0,), (0,)), ((), ())),
                           preferred_element_type=F32)


def _bias_kernel(rb_ref, bkt_ref, o_ref, *, relative_to_last):
    h = pl.program_id(0)
    bkt = bkt_ref[...]
    acc = jnp.zeros(bkt.shape, F32)
    for b in range(REL_BUCKETS):
        acc = jnp.where(bkt == b, rb_ref[b, h], acc)
    if relative_to_last:
        acc = acc - rb_ref[REL_BUCKETS - 1, h]
    o_ref[0] = acc * LOG2E


def _bias_tile(rel_bias, bkt, relative_to_last):
    rows, cols = bkt.shape
    return pl.pallas_call(
        functools.partial(_bias_kernel, relative_to_last=relative_to_last),
        grid=(REL_HEADS,),
        in_specs=[pl.BlockSpec(memory_space=pltpu.SMEM),
                  pl.BlockSpec((rows, cols), lambda h: (0, 0))],
        out_specs=pl.BlockSpec((1, rows, cols), lambda h: (h, 0, 0)),
        out_shape=jax.ShapeDtypeStruct((REL_HEADS, rows, cols), F32),
        compiler_params=_cparams("arbitrary"),
        name="rel_bias_tiles",
    )(rel_bias, jnp.asarray(bkt))


def _bias_tiles(rel_bias):
    moba_bkt, band_bkt = _bucket_tiles()
    return _bias_tile(rel_bias, moba_bkt, True), _bias_tile(rel_bias, band_bkt, False)


def _adaln_kernel(c_ref, w_ref, b_ref, o_ref):
    c = c_ref[...]
    cond = c * jax.nn.sigmoid(c)
    o_ref[0] = _dot(cond, w_ref[0]) + b_ref[0]


def _adaln(c, ada_w, ada_b):
    depth, d, n6 = ada_w.shape
    bsz = c.shape[0]
    tn = 1536 if n6 % 1536 == 0 else n6
    return pl.pallas_call(
        _adaln_kernel,
        grid=(depth, n6 // tn),
        in_specs=[pl.BlockSpec((bsz, d), lambda l, j: (0, 0)),
                  pl.BlockSpec((1, d, tn), lambda l, j: (l, 0, j)),
                  pl.BlockSpec((1, 1, tn), lambda l, j: (l, 0, j))],
        out_specs=pl.BlockSpec((1, bsz, tn), lambda l, j: (l, 0, j)),
        out_shape=jax.ShapeDtypeStruct((depth, bsz, n6), F32),
        compiler_params=_cparams("arbitrary", "arbitrary"),
        name="adaln_mod",
    )(c, ada_w, ada_b.reshape(depth, 1, n6))


def _head_rms(y, w_ref, hd, post_scale):
    cols = y.shape[1]
    lane = lax.broadcasted_iota(jnp.int32, (1, LANES), 1)
    w = w_ref[...] * post_scale
    out = []
    for c in range(cols // LANES):
        t = y[:, c * LANES:(c + 1) * LANES]
        sq = t * t
        if hd == LANES:
            r = lax.rsqrt(jnp.sum(sq, axis=-1, keepdims=True) * (1.0 / hd) + RMS_EPS)
            out.append(t * r * w)
        else:
            low = lane < hd
            s_lo = jnp.sum(jnp.where(low, sq, 0.0), axis=-1, keepdims=True)
            s_hi = jnp.sum(jnp.where(low, 0.0, sq), axis=-1, keepdims=True)
            r = lax.rsqrt(jnp.where(low, s_lo, s_hi) * (1.0 / hd) + RMS_EPS)
            out.append(t * r * jnp.concatenate([w, w], axis=1))
    return jnp.concatenate(out, axis=1) if len(out) > 1 else out[0]


def _proj_kernel(x_ref, nw_ref, mod_ref, w_ref, *refs, shift_row, outs):
    n_norm = len({o[2] for o in outs if o[1]})
    norm_refs, o_refs = refs[:n_norm], refs[n_norm:]
    x = x_ref[0]
    h = _rms_rows(x, nw_ref[...])
    h = h * (1.0 + mod_ref[0, shift_row + 1:shift_row + 2, :]) + mod_ref[0, shift_row:shift_row + 1, :]
    y = _dot(h, w_ref[...])
    off = 0
    for o_ref, (wd, hd, widx, post) in zip(o_refs, outs):
        t = y[:, off:off + wd]
        if hd:
            t = _head_rms(t, norm_refs[widx], hd, post)
        o_ref[0] = t.astype(o_ref.dtype)
        off += wd


def _norm_mod_project(x, norm_w, mod, w_bf16, outs, dtypes, head_norms, shift_row):
    bsz, s, d = x.shape
    tm = min(PROJ_ROWS, s)
    n = w_bf16.shape[1]
    widths = [o[0] for o in outs]
    assert sum(widths) == n and all(wd % LANES == 0 for wd in widths)
    return pl.pallas_call(
        functools.partial(_proj_kernel, shift_row=shift_row, outs=tuple(outs)),
        grid=(bsz, s // tm),
        in_specs=[pl.BlockSpec((1, tm, d), lambda b, i: (b, i, 0)),
                  pl.BlockSpec((1, d), lambda b, i: (0, 0)),
                  pl.BlockSpec((1, 6, d), lambda b, i: (b, 0, 0)),
                  pl.BlockSpec((d, n), lambda b, i: (0, 0))]
                 + [pl.BlockSpec((1, hn.shape[0]), lambda b, i: (0, 0)) for hn in head_norms],
        out_specs=[pl.BlockSpec((1, tm, wd), lambda b, i: (b, i, 0)) for wd in widths],
        out_shape=[jax.ShapeDtypeStruct((bsz, s, wd), dt) for wd, dt in zip(widths, dtypes)],
        compiler_params=_cparams("arbitrary", "arbitrary"),
        name="norm_mod_project",
    )(x, norm_w.reshape(1, d), mod, w_bf16, *[hn.reshape(1, -1) for hn in head_norms])


def _swa_kernel(sink_ref, q_ref, kc_ref, kp_ref, vc_ref, vp_ref, bias_ref, o_ref):
    n = pl.program_id(1)
    w = A_WINDOW
    hd = A_HEAD_DIM
    group = A_HEADS // A_KV_HEADS
    q = q_ref[0]
    k = jnp.concatenate([kp_ref[0], kc_ref[0]], axis=0)
    v = jnp.concatenate([vp_ref[0], vc_ref[0]], axis=0)
    v_t = v.astype(F32).T
    ones_rows = (lax.broadcasted_iota(jnp.int32, (SWA_VT_ROWS - hd, 2 * w), 0) == 0).astype(BF16)
    key = lax.broadcasted_iota(jnp.int32, (2 * w, group * w), 0)
    qry = lax.broadcasted_iota(jnp.int32, (2 * w, group * w), 1) % w
    dist = qry + w - key
    mask = (dist >= 0) & (dist < w) & ((n > 0) | (key >= w))
    head_of_lane = lax.broadcasted_iota(jnp.int32, (1, group * w), 1) // w
    for kv in range(A_KV_HEADS):
        k_g = k[:, kv * hd:(kv + 1) * hd]
        vt_g = jnp.concatenate([v_t[kv * hd:(kv + 1) * hd, :].astype(BF16), ones_rows], axis=0)
        h0 = kv * group
        q_g = jnp.concatenate([q[:, (h0 + g) * hd:(h0 + g + 1) * hd] for g in range(group)], axis=0)
        bias = jnp.concatenate([bias_ref[h0 + g] for g in range(group)], axis=1)
        lg = jnp.where(mask, _dot_nt(k_g, q_g) + bias, NEG_BIG)
        sink = jnp.full((1, group * w), sink_ref[h0 + group - 1] * LOG2E, F32)
        for g in range(group - 2, -1, -1):
            sink = jnp.where(head_of_lane == g, sink_ref[h0 + g] * LOG2E, sink)
        m = jnp.maximum(jnp.max(lg, axis=0, keepdims=True), sink)
        p = jnp.exp2(lg - m).astype(BF16)
        acc = jnp.dot(vt_g, p, preferred_element_type=F32)
        o_t = acc[0:hd] / (acc[hd:hd + 1] + jnp.exp2(sink - m))
        for pair in range(group // 2):
            two = jnp.concatenate([o_t[:, (2 * pair) * w:(2 * pair + 1) * w],
                                   o_t[:, (2 * pair + 1) * w:(2 * pair + 2) * w]], axis=0)
            c0 = (h0 + 2 * pair) * hd
            o_ref[0, :, c0:c0 + 2 * hd] = two.T.astype(o_ref.dtype)


def _swa_attention(qa, ka, va, sinks, band_bias):
    bsz, s, _ = qa.shape
    w = A_WINDOW
    nb = s // w
    kvw = A_KV_HEADS * A_HEAD_DIM
    cur = lambda b, n: (b, n, 0)
    prev = lambda b, n: (b, jnp.maximum(n - 1, 0), 0)
    return pl.pallas_call(
        _swa_kernel,
        grid=(bsz, nb),
        in_specs=[pl.BlockSpec(memory_space=pltpu.SMEM),
                  pl.BlockSpec((1, w, A_HEADS * A_HEAD_DIM), cur),
                  pl.BlockSpec((1, w, kvw), cur),
                  pl.BlockSpec((1, w, kvw), prev),
                  pl.BlockSpec((1, w, kvw), cur),
                  pl.BlockSpec((1, w, kvw), prev),
                  pl.BlockSpec((REL_HEADS, 2 * w, w), lambda b, n: (0, 0, 0))],
        out_specs=pl.BlockSpec((1, w, A_HEADS * A_HEAD_DIM), cur),
        out_shape=jax.ShapeDtypeStruct((bsz, s, A_HEADS * A_HEAD_DIM), BF16),
        compiler_params=_cparams("arbitrary", "arbitrary"),
        name="swa_sink_attention",
    )(sinks, qa, ka, ka, va, va, band_bias)


def _gla_kernel(qk_ref, v_ref, r_ref, ab_ref, gu_ref, gb_ref, on_ref, o_ref, state_ref):
    s_idx = pl.program_id(1)
    c_len = B_CHUNK
    dk, dv = B_KEY_DIM, B_VAL_DIM
    hk = B_HEADS * dk

    @pl.when(s_idx == 0)
    def _():
        state_ref[...] = jnp.zeros(state_ref.shape, F32)

    z = _dot(ab_ref[0], gu_ref[...]) + gb_ref[...]
    log_a = (jnp.minimum(z, 0.0) - jnp.log(1.0 + jnp.exp(-jnp.abs(z)))) / B_GATE_TAU
    rows = qk_ref.shape[1]
    ri = lax.broadcasted_iota(jnp.int32, (c_len, c_len), 0)
    ci = lax.broadcasted_iota(jnp.int32, (c_len, c_len), 1)
    tril = ri >= ci
    tri = tril.astype(BF16)
    for c in range(rows // c_len):
        sl = slice(c * c_len, (c + 1) * c_len)
        g = log_a[sl]
        g_hi = g.astype(BF16)
        g_lo = (g - g_hi.astype(F32)).astype(BF16)
        b = (jnp.dot(tri, g_hi, preferred_element_type=F32)
             + jnp.dot(tri, g_lo, preferred_element_type=F32))
        b_last = b[c_len - 1:c_len, :]
        q = qk_ref[0, sl, 0:hk] * (dk ** -0.5)
        k = qk_ref[0, sl, hk:2 * hk]
        q_dec = q * jnp.exp(b)
        k_dec = k * jnp.exp(-b)
        k_upd = k * jnp.exp(b_last - b)
        decay = jnp.exp(b_last)
        heads = []
        for h in range(B_HEADS):
            ks = slice(h * dk, (h + 1) * dk)
            v_h = v_ref[0, sl, h * dv:(h + 1) * dv]
            st = state_ref[h]
            att = jnp.where(tril, _dot_nt(q_dec[:, ks], k_dec[:, ks]), 0.0)
            o_h = _dot(att, v_h) + _dot_nt(q_dec[:, ks], st)
            state_ref[h] = st * decay[:, ks] + _dot_tn(v_h, k_upd[:, ks])
            o_h = _rms_rows(o_h, on_ref[...])
            r_h = r_ref[0, sl, h * dv:(h + 1) * dv]
            heads.append(o_h * (r_h * jax.nn.sigmoid(r_h)))
        o_ref[0, sl, :] = jnp.concatenate(heads, axis=-1).astype(o_ref.dtype)


def _gla(qk, vb, rb, ab, gate_up_pad, gate_bias, out_norm):
    bsz, s, _ = qk.shape
    tm = min(GLA_ROWS, s)
    hv = B_HEADS * B_VAL_DIM
    hk = B_HEADS * B_KEY_DIM
    blk = lambda b, i: (b, i, 0)
    const = lambda b, i: (0, 0)
    return pl.pallas_call(
        _gla_kernel,
        grid=(bsz, s // tm),
        in_specs=[pl.BlockSpec((1, tm, 2 * hk), blk),
                  pl.BlockSpec((1, tm, hv), blk),
                  pl.BlockSpec((1, tm, hv), blk),
                  pl.BlockSpec((1, tm, LANES), blk),
                  pl.BlockSpec((LANES, hk), const),
                  pl.BlockSpec((1, hk), const),
                  pl.BlockSpec((1, B_VAL_DIM), const)],
        out_specs=pl.BlockSpec((1, tm, hv), blk),
        out_shape=jax.ShapeDtypeStruct((bsz, s, hv), BF16),
        scratch_shapes=[pltpu.VMEM((B_HEADS, B_VAL_DIM, B_KEY_DIM), F32)],
        compiler_params=_cparams("arbitrary", "arbitrary"),
        name="gated_linear_attention",
    )(qk, vb, rb, ab, gate_up_pad, gate_bias.reshape(1, hk), out_norm.reshape(1, -1))


def _moba_kernel(q_ref, k_ref, v_ref, bias_ref, o_ref, vt_ref, kmean_ref, pick_ref):
    i = pl.program_id(2)
    blk = C_BLOCK
    hd = C_HEAD_DIM
    nb = k_ref.shape[1] // blk
    grp = MOBA_GROUP
    heads = range(MOBA_HEADS_PER_STEP)

    def lanes(hh):
        return slice(hh * hd, (hh + 1) * hd)

    @pl.when(i == 0)
    def _():
        ones_rows = (lax.broadcasted_iota(jnp.int32, (MOBA_VT_ROWS - hd, blk), 0) == 0).astype(BF16)
        for hh in heads:
            kn = k_ref[0, :, lanes(hh)].astype(F32)
            kmean_ref[hh] = jnp.mean(kn.reshape(nb, blk, hd), axis=1)
            for j in range(nb):
                vt = v_ref[0, j * blk:(j + 1) * blk, lanes(hh)].astype(F32).T.astype(BF16)
                vt_ref[hh, j] = jnp.concatenate([vt, ones_rows], axis=0)

    def logits(hh, q_h, j):
        j0 = pl.multiple_of(j * blk, blk)
        return _dot_nt(k_ref[0, pl.ds(j0, blk), lanes(hh)], q_h)

    def col_max(lg, picked):
        return jnp.where(picked, jnp.max(lg, axis=0, keepdims=True), NEG_BIG)

    def weighted_v(hh, j, lg, m, picked):
        p = jnp.exp2(lg - m).astype(BF16)
        return jnp.where(picked, jnp.dot(vt_ref[hh, j], p, preferred_element_type=F32), 0.0)

    def softmax_step(ms, accs, tiles):
        m_new = []
        for hh in heads:
            m_h = ms[hh]
            for _, lg, picked in tiles[hh]:
                m_h = jnp.maximum(m_h, col_max(lg, picked))
            m_new.append(m_h)
        out = []
        for hh in heads:
            acc = accs[hh] * jnp.exp2(ms[hh] - m_new[hh])
            for j, lg, picked in tiles[hh]:
                acc = acc + weighted_v(hh, j, lg, m_new[hh], picked)
            out.append(acc)
        return tuple(m_new), tuple(out)

    jrow = lax.broadcasted_iota(jnp.int32, (nb, blk), 0)
    key = lax.broadcasted_iota(jnp.int32, (blk, blk), 0)
    qry = lax.broadcasted_iota(jnp.int32, (blk, blk), 1)
    j_prev = jnp.maximum(i - 1, 0)
    qs = []
    for hh in heads:
        q_h = q_ref[0, :, lanes(hh)]
        qs.append(q_h)
        score = jnp.where(jrow < i, _dot_nt(kmean_ref[hh], q_h), -jnp.inf)
        beats = jnp.zeros((nb, blk), jnp.int32)
        for j2 in range(nb):
            s2 = score[j2:j2 + 1, :]
            beats = beats + ((s2 > score) | ((s2 == score) & (j2 < jrow))).astype(jnp.int32)
        pick_ref[hh] = ((beats < C_TOPK) & (jrow < i)).astype(F32)

    first = []
    for hh in heads:
        lg_own = jnp.where(key <= qry, logits(hh, qs[hh], i) + bias_ref[hh, 0:blk, :], NEG_BIG)
        lg_prev = logits(hh, qs[hh], j_prev) + bias_ref[hh, blk:2 * blk, :]
        pick_prev = pick_ref[hh, pl.ds(j_prev, 1), :] > 0.5
        first.append([(i, lg_own, True), (j_prev, lg_prev, pick_prev)])
    start = (tuple(jnp.full((1, blk), NEG_BIG, F32) for _ in heads),
             tuple(jnp.zeros((MOBA_VT_ROWS, blk), F32) for _ in heads))
    carry0 = softmax_step(start[0], start[1], first)

    n_far = jnp.maximum(i - 1, 0)

    def body(g, carry):
        tiles = []
        for hh in heads:
            row = []
            for u in range(grp):
                j = g * grp + u
                jc = jnp.minimum(j, nb - 1)
                picked = (pick_ref[hh, pl.ds(jc, 1), :] > 0.5) & (j < n_far)
                row.append((jc, logits(hh, qs[hh], jc), picked))
            tiles.append(row)
        return softmax_step(carry[0], carry[1], tiles)

    _, accs = lax.fori_loop(0, (n_far + grp - 1) // grp, body, carry0)
    for hh in heads:
        o_ref[0, :, lanes(hh)] = (accs[hh][0:hd] / accs[hh][hd:hd + 1]).T.astype(o_ref.dtype)


def _moba(q, k, v, bias_tiles):
    bsz, s, _ = q.shape
    blk, hd = C_BLOCK, C_HEAD_DIM
    assert s % blk == 0
    nb = s // blk
    hb = MOBA_HEADS_PER_STEP
    assert C_HEADS % hb == 0
    return pl.pallas_call(
        _moba_kernel,
        grid=(bsz, C_HEADS // hb, nb),
        in_specs=[pl.BlockSpec((1, blk, hb * hd), lambda b, h, i: (b, i, h)),
                  pl.BlockSpec((1, s, hb * hd), lambda b, h, i: (b, 0, h)),
                  pl.BlockSpec((1, s, hb * hd), lambda b, h, i: (b, 0, h)),
                  pl.BlockSpec((hb, 2 * blk, blk), lambda b, h, i: (h, 0, 0))],
        out_specs=pl.BlockSpec((1, blk, hb * hd), lambda b, h, i: (b, i, h)),
        out_shape=jax.ShapeDtypeStruct((bsz, s, C_HEADS * hd), BF16),
        scratch_shapes=[pltpu.VMEM((hb, nb, MOBA_VT_ROWS, blk), BF16),
                        pltpu.VMEM((hb, nb, hd), F32), pltpu.VMEM((hb, nb, blk), F32)],
        compiler_params=_cparams("arbitrary", "arbitrary", "arbitrary"),
        name="moba_attention",
    )(q, k, v, bias_tiles)


def _out_route_kernel(*refs, n_in):
    a_refs = refs[:n_in]
    w_refs = refs[n_in:2 * n_in]
    (x_ref, mod_ref, nw_ref, wr_ref, br_ref,
     xo_ref, h_ref, ids_ref, gate_ref, rank_ref, cnt_ref, base_ref) = refs[2 * n_in:]
    first = (pl.program_id(0) == 0) & (pl.program_id(1) == 0)

    @pl.when(first)
    def _():
        base_ref[...] = jnp.zeros(base_ref.shape, F32)

    y = _dot(a_refs[0][0], w_refs[0][...])
    for a_ref, w_ref in zip(a_refs[1:], w_refs[1:]):
        y = y + _dot(a_ref[0], w_ref[...])
    x = x_ref[0] + mod_ref[0, 2:3, :] * y
    xo_ref[0] = x

    tm = x.shape[0]
    h = _rms_rows(x, nw_ref[...])
    h = h * (1.0 + mod_ref[0, 4:5, :]) + mod_ref[0, 3:4, :]
    _to_row_tiles(h_ref, h)
    lt = _dot_nt(wr_ref[...], h) + br_ref[...]
    g = [lt[r:r + 1, :] for r in range(MOE_GROUPS)]
    gmax = functools.reduce(jnp.maximum, g)
    gsel = jnp.full(gmax.shape, MOE_GROUPS - 1, jnp.int32)
    for r in range(MOE_GROUPS - 2, -1, -1):
        gsel = jnp.where(g[r] == gmax, r, gsel)
    p_g = 1.0 / functools.reduce(jnp.add, [jnp.exp(gr - gmax) for gr in g])
    epg = MOE_EXPERTS_PER_GROUP
    e_in = lt[8 + (MOE_GROUPS - 1) * epg:8 + MOE_GROUPS * epg, :]
    for r in range(MOE_GROUPS - 2, -1, -1):
        e_in = jnp.where(gsel == r, lt[8 + r * epg:8 + (r + 1) * epg, :], e_in)
    sub = lax.broadcasted_iota(jnp.int32, (epg, tm), 0)
    v1 = jnp.max(e_in, axis=0, keepdims=True)
    i1 = jnp.min(jnp.where(e_in == v1, sub, epg), axis=0, keepdims=True)
    e2 = jnp.where(sub == i1, -jnp.inf, e_in)
    v2 = jnp.max(e2, axis=0, keepdims=True)
    i2 = jnp.min(jnp.where(e2 == v2, sub, epg), axis=0, keepdims=True)
    t = jnp.exp(v2 - v1)
    w1 = p_g / (1.0 + t)
    w2 = p_g * t / (1.0 + t)
    id1 = gsel * epg + i1
    id2 = gsel * epg + i2
    ids_ref[...] = jnp.concatenate([id1, id2], axis=0)
    gate_ref[...] = jnp.concatenate([w1, w2, jnp.zeros((LANES - 2, tm), F32)], axis=0).T

    eidx = lax.broadcasted_iota(jnp.int32, (MOE_EXPERTS, tm), 0)
    oh1 = eidx == id1
    oh2 = eidx == id2
    onehot = (oh1 | oh2).astype(BF16)
    tr = lax.broadcasted_iota(jnp.int32, (tm, tm), 0)
    tc = lax.broadcasted_iota(jnp.int32, (tm, tm), 1)
    before = (tr < tc).astype(BF16)
    prefix = jnp.dot(onehot, before, preferred_element_type=F32) + base_ref[...]
    r1 = jnp.sum(jnp.where(oh1, prefix, 0.0), axis=0, keepdims=True)
    r2 = jnp.sum(jnp.where(oh2, prefix, 0.0), axis=0, keepdims=True)
    rank_ref[...] = jnp.concatenate([r1, r2], axis=0).astype(jnp.int32)
    base_ref[...] = base_ref[...] + jnp.sum(onehot.astype(F32), axis=1, keepdims=True)
    cnt_ref[...] = base_ref[...].astype(jnp.int32)


def _out_project_route(acts, weights, x, mod, norm_w, wr_t, br):
    bsz, s, d = x.shape
    tm = min(ROUTER_ROWS, s)
    n = bsz * s
    nt = s // tm
    n_in = len(acts)
    tok = lambda b, i: (0, b * nt + i)
    const = lambda b, i: (0, 0)
    in_specs = [pl.BlockSpec((1, tm, a.shape[2]), lambda b, i: (b, i, 0)) for a in acts]
    in_specs += [pl.BlockSpec(w.shape, const) for w in weights]
    in_specs += [pl.BlockSpec((1, tm, d), lambda b, i: (b, i, 0)),
                 pl.BlockSpec((1, 6, d), lambda b, i: (b, 0, 0)),
                 pl.BlockSpec((1, d), const),
                 pl.BlockSpec((ROUTER_LOGIT_ROWS, d), const),
                 pl.BlockSpec((ROUTER_LOGIT_ROWS, 1), const)]
    return pl.pallas_call(
        functools.partial(_out_route_kernel, n_in=n_in),
        grid=(bsz, nt),
        in_specs=in_specs,
        out_specs=[pl.BlockSpec((1, tm, d), lambda b, i: (b, i, 0)),
                   pl.BlockSpec((tm * d // LANES, LANES), lambda b, i: (b * nt + i, 0)),
                   pl.BlockSpec((MOE_TOPK, tm), tok),
                   pl.BlockSpec((tm, LANES), lambda b, i: (b * nt + i, 0)),
                   pl.BlockSpec((MOE_TOPK, tm), tok),
                   pl.BlockSpec((MOE_EXPERTS, 1), const)],
        out_shape=[jax.ShapeDtypeStruct((bsz, s, d), F32),
                   jax.ShapeDtypeStruct((n * d // LANES, LANES), F32),
                   jax.ShapeDtypeStruct((MOE_TOPK, n), jnp.int32),
                   jax.ShapeDtypeStruct((n, LANES), F32),
                   jax.ShapeDtypeStruct((MOE_TOPK, n), jnp.int32),
                   jax.ShapeDtypeStruct((MOE_EXPERTS, 1), jnp.int32)],
        scratch_shapes=[pltpu.VMEM((MOE_EXPERTS, 1), F32)],
        compiler_params=_cparams("arbitrary", "arbitrary"),
        name="out_project_route",
    )(*acts, *weights, x, mod, norm_w.reshape(1, d), wr_t, br)


def _dest_kernel(ids_ref, rank_ref, start_ref, o_ref):
    ids = ids_ref[...]
    tm = ids.shape[1]
    eidx = lax.broadcasted_iota(jnp.int32, (MOE_EXPERTS, tm), 0)
    rows = []
    for k in range(MOE_TOPK):
        base = jnp.sum(jnp.where(eidx == ids[k:k + 1, :], start_ref[...], 0), axis=0, keepdims=True)
        rows.append(base + rank_ref[k:k + 1, :])
    o_ref[...] = jnp.concatenate(rows, axis=0)


def _dest_rows(ids, rank, start):
    n = ids.shape[1]
    tm = min(2048, n)
    return pl.pallas_call(
        _dest_kernel,
        grid=(n // tm,),
        in_specs=[pl.BlockSpec((MOE_TOPK, tm), lambda i: (0, i)),
                  pl.BlockSpec((MOE_TOPK, tm), lambda i: (0, i)),
                  pl.BlockSpec((MOE_EXPERTS, 1), lambda i: (0, 0))],
        out_specs=pl.BlockSpec((MOE_TOPK, tm), lambda i: (0, i)),
        out_shape=jax.ShapeDtypeStruct((MOE_TOPK, n), jnp.int32),
        compiler_params=_cparams("arbitrary"),
        name="moe_dest_rows",
    )(ids, rank, start)


def _row_copy(src, dst, sem, src_row, dst_row, rt):
    return pltpu.make_async_copy(src.at[pl.ds(pl.multiple_of(src_row * rt, rt), rt)],
                                 dst.at[pl.ds(pl.multiple_of(dst_row * rt, rt), rt)], sem)


def _dispatch_kernel(pend_ref, padded_ref, dest_ref, h_ref, xs_ref, zero_ref, sem, *, rt):
    tm = h_ref.shape[0] // rt
    zrows = zero_ref.shape[0] // rt

    @pl.when(pl.program_id(0) == 0)
    def _():
        zero_ref[...] = jnp.zeros(zero_ref.shape, F32)
        for e in range(MOE_EXPERTS):
            @pl.when(padded_ref[e] > 0)
            def _():
                row0 = pl.multiple_of((pend_ref[e] - zrows) * rt, zrows * rt)
                cp = pltpu.make_async_copy(zero_ref, xs_ref.at[pl.ds(row0, zrows * rt)], sem)
                cp.start()
                cp.wait()

        def zero_tail(b, _):
            row0 = pl.multiple_of(b * zrows * rt, zrows * rt)
            cp = pltpu.make_async_copy(zero_ref, xs_ref.at[pl.ds(row0, zrows * rt)], sem)
            cp.start()
            cp.wait()
            return 0

        lax.fori_loop(pend_ref[MOE_EXPERTS - 1] // zrows, xs_ref.shape[0] // (zrows * rt), zero_tail, 0)

    def start(t, _):
        for k in range(MOE_TOPK):
            _row_copy(h_ref, xs_ref, sem, t, dest_ref[k, t], rt).start(priority=k)
        return 0

    lax.fori_loop(0, tm, start, 0, unroll=DMA_ISSUE_UNROLL)
    for k in range(MOE_TOPK):
        pltpu.make_async_copy(h_ref, xs_ref.at[pl.ds(0, tm * rt)], sem).wait()


def _dispatch(h_tiles, dest, pend, padded, n_rows, rt):
    n = h_tiles.shape[0] // rt
    tm = min(DISPATCH_ROWS, n)
    return pl.pallas_call(
        functools.partial(_dispatch_kernel, rt=rt),
        grid=(n // tm,),
        in_specs=[pl.BlockSpec(memory_space=pltpu.SMEM),
                  pl.BlockSpec(memory_space=pltpu.SMEM),
                  pl.BlockSpec((MOE_TOPK, tm), lambda i: (0, i), memory_space=pltpu.SMEM),
                  pl.BlockSpec((tm * rt, LANES), lambda i: (i, 0))],
        out_specs=pl.BlockSpec(memory_space=pl.ANY),
        out_shape=jax.ShapeDtypeStruct((n_rows * rt, LANES), F32),
        scratch_shapes=[pltpu.VMEM((MOE_BLOCK_ROWS * rt, LANES), F32), pltpu.SemaphoreType.DMA(())],
        compiler_params=_cparams("arbitrary"),
        name="moe_dispatch",
    )(pend, padded, dest, h_tiles)


def _expert_kernel(blk_e_ref, n_used_ref, x_ref, w1_ref, w3_ref, w2_ref, o_ref, w1b_ref, w3b_ref, w2b_ref):
    i = pl.program_id(0)
    used = i < n_used_ref[0]
    new_expert = (i == 0) | (blk_e_ref[i] != blk_e_ref[jnp.maximum(i - 1, 0)])

    @pl.when(used & new_expert)
    def _():
        w1b_ref[...] = w1_ref[0, 0].astype(BF16)
        w3b_ref[...] = w3_ref[0, 0].astype(BF16)
        w2b_ref[...] = w2_ref[0, 0].astype(BF16)

    @pl.when(used)
    def _():
        d = w1b_ref.shape[0]
        rows = x_ref.shape[0] * LANES // d
        x = _from_row_tiles(x_ref, rows, d).astype(BF16)
        a = jnp.dot(x, w1b_ref[...], preferred_element_type=F32)
        g = jnp.dot(x, w3b_ref[...], preferred_element_type=F32)
        y = jnp.dot((a * jax.nn.sigmoid(a) * g).astype(BF16), w2b_ref[...], preferred_element_type=F32)
        _to_row_tiles(o_ref, y)

    @pl.when(jnp.logical_not(used))
    def _():
        o_ref[...] = jnp.zeros(o_ref.shape, F32)


def _expert_ffn(xs, blk_e, n_used, w1, w3, w2, layer):
    d, hid = w1.shape[2], w1.shape[3]
    rt = d // LANES
    n_rows = xs.shape[0] // rt
    tm = MOE_BLOCK_ROWS
    row = lambda i, be, nu: (jnp.minimum(i, nu[0] - 1), 0)
    wsel = lambda i, be, nu: (layer, be[i], 0, 0)
    return pl.pallas_call(
        _expert_kernel,
        grid_spec=pltpu.PrefetchScalarGridSpec(
            num_scalar_prefetch=2,
            grid=(n_rows // tm,),
            in_specs=[pl.BlockSpec((tm * rt, LANES), row),
                      pl.BlockSpec((1, 1, d, hid), wsel),
                      pl.BlockSpec((1, 1, d, hid), wsel),
                      pl.BlockSpec((1, 1, hid, d), wsel)],
            out_specs=pl.BlockSpec((tm * rt, LANES), lambda i, be, nu: (i, 0)),
            scratch_shapes=[pltpu.VMEM((d, hid), BF16), pltpu.VMEM((d, hid), BF16),
                            pltpu.VMEM((hid, d), BF16)]),
        out_shape=jax.ShapeDtypeStruct((n_rows * rt, LANES), F32),
        compiler_params=_cparams("arbitrary"),
        name="moe_expert_ffn",
    )(blk_e, n_used, xs, w1, w3, w2)


def _combine_kernel(dest_ref, ys_ref, gate_ref, x_ref, mod_ref, o_ref, buf_ref, sem):
    tm, d = x_ref.shape[1], x_ref.shape[2]
    rt = d // LANES

    def start(t, _):
        for k in range(MOE_TOPK):
            _row_copy(ys_ref, buf_ref.at[k], sem, dest_ref[k, t], t, rt).start(priority=k)
        return 0

    lax.fori_loop(0, tm, start, 0, unroll=DMA_ISSUE_UNROLL)
    for k in range(MOE_TOPK):
        pltpu.make_async_copy(ys_ref.at[pl.ds(0, tm * rt)], buf_ref.at[k], sem).wait()
    g = gate_ref[...]
    y = g[:, 0:1] * _from_row_tiles(buf_ref, tm, d, (0,)) + g[:, 1:2] * _from_row_tiles(buf_ref, tm, d, (1,))
    o_ref[0] = x_ref[0] + mod_ref[0, 5:6, :] * y


def _combine(ys, dest, gates, x, mod):
    bsz, s, d = x.shape
    tm = min(COMBINE_ROWS, s)
    nt = s // tm
    return pl.pallas_call(
        _combine_kernel,
        grid=(bsz, nt),
        in_specs=[pl.BlockSpec((MOE_TOPK, tm), lambda b, i: (0, b * nt + i), memory_space=pltpu.SMEM),
                  pl.BlockSpec(memory_space=pl.ANY),
                  pl.BlockSpec((tm, LANES), lambda b, i: (b * nt + i, 0)),
                  pl.BlockSpec((1, tm, d), lambda b, i: (b, i, 0)),
                  pl.BlockSpec((1, 6, d), lambda b, i: (b, 0, 0))],
        out_specs=pl.BlockSpec((1, tm, d), lambda b, i: (b, i, 0)),
        out_shape=jax.ShapeDtypeStruct((bsz, s, d), F32),
        scratch_shapes=[pltpu.VMEM((MOE_TOPK, tm * d // LANES, LANES), F32), pltpu.SemaphoreType.DMA(())],
        compiler_params=_cparams("arbitrary", "arbitrary"),
        name="moe_combine",
    )(dest, ys, gates, x, mod)


def _mixer_out_moe_residual(acts, w_outs, x, norm_w, mod, w_group, b_group, w_expert, b_expert, w1, w3, w2, layer):
    bsz, s, d = x.shape
    n = bsz * s
    tm = MOE_BLOCK_ROWS
    wr_t = jnp.zeros((ROUTER_LOGIT_ROWS, d), F32).at[0:MOE_GROUPS].set(w_group.T).at[8:].set(w_expert.T)
    br = jnp.zeros((ROUTER_LOGIT_ROWS, 1), F32).at[0:MOE_GROUPS, 0].set(b_group).at[8:, 0].set(b_expert)
    x_mid, h2, ids, gates, rank, counts = _out_project_route(acts, w_outs, x, mod, norm_w, wr_t.astype(BF16), br)
    counts = counts[:, 0]
    padded = ((counts + tm - 1) // tm) * tm
    pend = jnp.cumsum(padded).astype(jnp.int32)
    start = pend - padded
    n_rows = ((n * MOE_TOPK + tm - 1) // tm) * tm + MOE_EXPERTS * tm
    blk_start = jnp.arange(n_rows // tm, dtype=jnp.int32) * tm
    blk_e = jnp.minimum(jnp.sum((pend[None, :] <= blk_start[:, None]).astype(jnp.int32), axis=1),
                        MOE_EXPERTS - 1)
    n_used = pend[-1:] // tm
    dest = _dest_rows(ids, rank, start.reshape(MOE_EXPERTS, 1))
    xs = _dispatch(h2, dest, pend, padded, n_rows, d // LANES)
    ys = _expert_ffn(xs, blk_e, n_used, w1, w3, w2, layer)
    return _combine(ys, dest, gates, x_mid, mod)


def _even_mixer(x, norm_w, mod, w_in, w_out, q_norm, k_norm, sinks, gate_up, gate_bias, out_norm, band_bias):
    d = x.shape[2]
    aq = A_HEADS * A_HEAD_DIM
    akv = A_KV_HEADS * A_HEAD_DIM
    bk = B_HEADS * B_KEY_DIM
    bv = B_HEADS * B_VAL_DIM
    n_in = w_in.shape[1]
    w_pad = jnp.zeros((d, n_in - B_GATE_RANK + LANES), F32).at[:, :n_in].set(w_in).astype(BF16)
    outs = ((aq, A_HEAD_DIM, 0, A_HEAD_DIM ** -0.5 * LOG2E), (akv, A_HEAD_DIM, 1, 1.0), (akv, 0, 0, 1.0),
            (2 * bk, 0, 0, 1.0), (bv, 0, 0, 1.0), (bv, 0, 0, 1.0), (LANES, 0, 0, 1.0))
    dtypes = (BF16, BF16, BF16, F32, BF16, F32, F32)
    qa, ka, va, qk, vb, rb, ab = _norm_mod_project(x, norm_w, mod, w_pad, outs, dtypes, (q_norm, k_norm),
                                                   shift_row=0)
    oa = _swa_attention(qa, ka, va, sinks, band_bias)
    gu_pad = jnp.zeros((LANES, bk), F32).at[:B_GATE_RANK].set(gate_up).astype(BF16)
    ob = _gla(qk, vb, rb, ab, gu_pad, gate_bias, out_norm)
    w_out = w_out.astype(BF16)
    return [oa, ob], [w_out[:aq], w_out[aq:]]


def _odd_mixer(x, norm_w, mod, w_in, w_out, q_norm, k_norm, bias_tiles):
    mix = C_HEADS * C_HEAD_DIM
    outs = ((mix, C_HEAD_DIM, 0, C_HEAD_DIM ** -0.5 * LOG2E), (mix, C_HEAD_DIM, 1, 1.0), (mix, 0, 0, 1.0))
    q, k, v = _norm_mod_project(x, norm_w, mod, w_in.astype(BF16), outs, (BF16, BF16, BF16), (q_norm, k_norm),
                                shift_row=0)
    return [_moba(q, k, v, bias_tiles)], [w_out.astype(BF16)]


def kernel(x, c, rel_bias, ada_w, ada_b, norm1_w, norm2_w, even_w_in, even_w_out, a_q_norm, a_k_norm, a_sinks, b_gate_up, b_gate_bias, b_out_norm, odd_w_in, odd_w_out, c_q_norm, c_k_norm, moe_w_group, moe_b_group, moe_w_expert, moe_b_expert, moe_w1, moe_w3, moe_w2):
    depth = ada_w.shape[0]
    bsz, _, d = x.shape
    moba_bias, band_bias = _bias_tiles(rel_bias)
    mod_all = _adaln(c, ada_w, ada_b).reshape(depth, bsz, 6, d)
    for layer in range(depth):
        mod = mod_all[layer]
        j = layer // 2
        if layer % 2 == 0:
            acts, w_outs = _even_mixer(x, norm1_w[layer], mod, even_w_in[j], even_w_out[j], a_q_norm[j],
                                       a_k_norm[j], a_sinks[j], b_gate_up[j], b_gate_bias[j], b_out_norm[j],
                                       band_bias)
        else:
            acts, w_outs = _odd_mixer(x, norm1_w[layer], mod, odd_w_in[j], odd_w_out[j], c_q_norm[j],
                                      c_k_norm[j], moba_bias)
        x = _mixer_out_moe_residual(acts, w_outs, x, norm2_w[layer], mod, moe_w_group[layer],
                                    moe_b_group[layer], moe_w_expert[layer], moe_b_expert[layer],
                                    moe_w1, moe_w3, moe_w2, layer)
    return x
```

```python
import functools
import math

import jax
import jax.numpy as jnp
import numpy as np
from jax import lax
from jax.experimental import pallas as pl
from jax.experimental.pallas import tpu as pltpu

RMS_EPS = 1e-6
A_HEADS, A_KV_HEADS, A_HEAD_DIM, A_WINDOW = 8, 2, 64, 128
B_HEADS, B_KEY_DIM, B_VAL_DIM, B_GATE_RANK, B_GATE_TAU, B_CHUNK = 4, 64, 128, 16, 16.0, 64
C_HEADS, C_HEAD_DIM, C_BLOCK, C_TOPK = 8, 128, 256, 3
REL_BUCKETS, REL_MAX_DIST, REL_HEADS = 32, 128, 8
MOE_GROUPS, MOE_EXPERTS_PER_GROUP, MOE_TOPK = 4, 8, 2
MOE_EXPERTS = MOE_GROUPS * MOE_EXPERTS_PER_GROUP

LANES = 128
V7X_VMEM_LIMIT_BYTES = 56 * 1024 * 1024
NEG_BIG = -1e30
PROJ_ROWS = 512
GLA_ROWS = 512
ROUTER_ROWS = 512
MOE_BLOCK_ROWS = 512
DISPATCH_ROWS = 1024
COMBINE_ROWS = 512
DMA_ISSUE_UNROLL = 8
ROUTER_LOGIT_ROWS = 8 + MOE_EXPERTS
MOBA_GROUP = 4
MOBA_HEADS_PER_STEP = 4
LOG2E = math.log2(math.e)
SWA_VT_ROWS = A_HEAD_DIM + 16
MOBA_VT_ROWS = C_HEAD_DIM + 16

F32 = jnp.float32
BF16 = jnp.bfloat16


def _cparams(*sem):
    return pltpu.CompilerParams(dimension_semantics=sem, vmem_limit_bytes=V7X_VMEM_LIMIT_BYTES)


def _rel_bucket_np(dist):
    exact = REL_BUCKETS // 2
    d = np.maximum(dist, 0)
    logd = np.log(np.maximum(d, 1).astype(np.float64) / exact) / math.log(REL_MAX_DIST / exact)
    far = np.minimum(exact + (logd * (REL_BUCKETS - exact)).astype(np.int64), REL_BUCKETS - 1)
    return np.where(d < exact, d, far).astype(np.int32)


def _bucket_tiles():
    li = np.arange(C_BLOCK)
    own = _rel_bucket_np(li[None, :] - li[:, None])
    prev = _rel_bucket_np(li[None, :] - li[:, None] + C_BLOCK)
    band = _rel_bucket_np(np.arange(A_WINDOW)[None, :] + A_WINDOW - np.arange(2 * A_WINDOW)[:, None])
    return np.concatenate([own, prev], axis=0), band


def _rms_rows(x, w):
    return x * lax.rsqrt(jnp.mean(x * x, axis=-1, keepdims=True) + RMS_EPS) * w


def _dot(a, b):
    return jnp.dot(a.astype(BF16), b.astype(BF16), preferred_element_type=F32)


def _dot_nt(a, b):
    return lax.dot_general(a.astype(BF16), b.astype(BF16), (((1,), (1,)), ((), ())),
                           preferred_element_type=F32)


def _to_row_tiles(ref, x, lead=()):
    rows, d = x.shape
    chunks = d // LANES
    for c in range(chunks):
        ref[lead + (pl.ds(c, rows, stride=chunks), slice(None))] = x[:, c * LANES:(c + 1) * LANES]


def _from_row_tiles(ref, rows, d, lead=()):
    chunks = d // LANES
    return jnp.concatenate([ref[lead + (pl.ds(c, rows, stride=chunks), slice(None))] for c in range(chunks)],
                           axis=1)


def _dot_tn(a, b):
    return lax.dot_general(a.astype(BF16), b.astype(BF16), (((0,), (0,)), ((), ())),
                           preferred_element_type=F32)


def _bias_kernel(rb_ref, bkt_ref, o_ref, *, relative_to_last):
    h = pl.program_id(0)
    bkt = bkt_ref[...]
    acc = jnp.zeros(bkt.shape, F32)
    for b in range(REL_BUCKETS):
        acc = jnp.where(bkt == b, rb_ref[b, h], acc)
    if relative_to_last:
        acc = acc - rb_ref[REL_BUCKETS - 1, h]
    o_ref[0] = acc * LOG2E


def _bias_tile(rel_bias, bkt, relative_to_last):
    rows, cols = bkt.shape
    return pl.pallas_call(
        functools.partial(_bias_kernel, relative_to_last=relative_to_last),
        grid=(REL_HEADS,),
        in_specs=[pl.BlockSpec(memory_space=pltpu.SMEM),
                  pl.BlockSpec((rows, cols), lambda h: (0, 0))],
        out_specs=pl.BlockSpec((1, rows, cols), lambda h: (h, 0, 0)),
        out_shape=jax.ShapeDtypeStruct((REL_HEADS, rows, cols), F32),
        compiler_params=_cparams("arbitrary"),
        name="rel_bias_tiles",
    )(rel_bias, jnp.asarray(bkt))


def _bias_tiles(rel_bias):
    moba_bkt, band_bkt = _bucket_tiles()
    return _bias_tile(rel_bias, moba_bkt, True), _bias_tile(rel_bias, band_bkt, False)


def _adaln_kernel(c_ref, w_ref, b_ref, o_ref):
    c = c_ref[...]
    cond = c * jax.nn.sigmoid(c)
    o_ref[0] = _dot(cond, w_ref[0]) + b_ref[0]


def _adaln(c, ada_w, ada_b):
    depth, d, n6 = ada_w.shape
    bsz = c.shape[0]
    tn = 1536 if n6 % 1536 == 0 else n6
    return pl.pallas_call(
        _adaln_kernel,
        grid=(depth, n6 // tn),
        in_specs=[pl.BlockSpec((bsz, d), lambda l, j: (0, 0)),
                  pl.BlockSpec((1, d, tn), lambda l, j: (l, 0, j)),
                  pl.BlockSpec((1, 1, tn), lambda l, j: (l, 0, j))],
        out_specs=pl.BlockSpec((1, bsz, tn), lambda l, j: (l, 0, j)),
        out_shape=jax.ShapeDtypeStruct((depth, bsz, n6), F32),
        compiler_params=_cparams("arbitrary", "arbitrary"),
        name="adaln_mod",
    )(c, ada_w, ada_b.reshape(depth, 1, n6))


def _head_rms(y, w_ref, hd, post_scale):
    cols = y.shape[1]
    lane = lax.broadcasted_iota(jnp.int32, (1, LANES), 1)
    w = w_ref[...] * post_scale
    out = []
    for c in range(cols // LANES):
        t = y[:, c * LANES:(c + 1) * LANES]
        sq = t * t
        if hd == LANES:
            r = lax.rsqrt(jnp.sum(sq, axis=-1, keepdims=True) * (1.0 / hd) + RMS_EPS)
            out.append(t * r * w)
        else:
            low = lane < hd
            s_lo = jnp.sum(jnp.where(low, sq, 0.0), axis=-1, keepdims=True)
            s_hi = jnp.sum(jnp.where(low, 0.0, sq), axis=-1, keepdims=True)
            r = lax.rsqrt(jnp.where(low, s_lo, s_hi) * (1.0 / hd) + RMS_EPS)
            out.append(t * r * jnp.concatenate([w, w], axis=1))
    return jnp.concatenate(out, axis=1) if len(out) > 1 else out[0]


def _project_rows(x, nw_ref, mod_ref, w_ref, norm_refs, o_refs, outs):
    h = _rms_rows(x, nw_ref[...])
    h = h * (1.0 + mod_ref[0, 1:2, :]) + mod_ref[0, 0:1, :]
    y = _dot(h, w_ref[...])
    off = 0
    for o_ref, (wd, hd, widx, post) in zip(o_refs, outs):
        t = y[:, off:off + wd]
        if hd:
            t = _head_rms(t, norm_refs[widx], hd, post)
        o_ref[0] = t.astype(o_ref.dtype)
        off += wd


def _proj_kernel(x_ref, nw_ref, mod_ref, w_ref, *refs, outs):
    n_norm = len(refs) - len(outs)
    _project_rows(x_ref[0], nw_ref, mod_ref, w_ref, refs[:n_norm], refs[n_norm:], outs)


def _norm_mod_project(x, norm_w, mod, w_bf16, outs, dtypes, head_norms):
    bsz, s, d = x.shape
    tm = min(PROJ_ROWS, s)
    n = w_bf16.shape[1]
    widths = [o[0] for o in outs]
    assert sum(widths) == n and all(wd % LANES == 0 for wd in widths)
    return pl.pallas_call(
        functools.partial(_proj_kernel, outs=tuple(outs)),
        grid=(bsz, s // tm),
        in_specs=[pl.BlockSpec((1, tm, d), lambda b, i: (b, i, 0)),
                  pl.BlockSpec((1, d), lambda b, i: (0, 0)),
                  pl.BlockSpec((1, 6, d), lambda b, i: (b, 0, 0)),
                  pl.BlockSpec((d, n), lambda b, i: (0, 0))]
                 + [pl.BlockSpec((1, hn.shape[0]), lambda b, i: (0, 0)) for hn in head_norms],
        out_specs=[pl.BlockSpec((1, tm, wd), lambda b, i: (b, i, 0)) for wd in widths],
        out_shape=[jax.ShapeDtypeStruct((bsz, s, wd), dt) for wd, dt in zip(widths, dtypes)],
        compiler_params=_cparams("arbitrary", "arbitrary"),
        name="norm_mod_project",
    )(x, norm_w.reshape(1, d), mod, w_bf16, *[hn.reshape(1, -1) for hn in head_norms])


def _swa_kernel(sink_ref, q_ref, kc_ref, kp_ref, vc_ref, vp_ref, bias_ref, o_ref):
    n = pl.program_id(1)
    w = A_WINDOW
    hd = A_HEAD_DIM
    group = A_HEADS // A_KV_HEADS
    q = q_ref[0]
    k = jnp.concatenate([kp_ref[0], kc_ref[0]], axis=0)
    v = jnp.concatenate([vp_ref[0], vc_ref[0]], axis=0)
    v_t = v.astype(F32).T
    ones_rows = (lax.broadcasted_iota(jnp.int32, (SWA_VT_ROWS - hd, 2 * w), 0) == 0).astype(BF16)
    key = lax.broadcasted_iota(jnp.int32, (2 * w, group * w), 0)
    qry = lax.broadcasted_iota(jnp.int32, (2 * w, group * w), 1) % w
    dist = qry + w - key
    mask = (dist >= 0) & (dist < w) & ((n > 0) | (key >= w))
    head_of_lane = lax.broadcasted_iota(jnp.int32, (1, group * w), 1) // w
    logits, sinks, vts = [], [], []
    for kv in range(A_KV_HEADS):
        h0 = kv * group
        k_g = k[:, kv * hd:(kv + 1) * hd]
        vts.append(jnp.concatenate([v_t[kv * hd:(kv + 1) * hd, :].astype(BF16), ones_rows], axis=0))
        q_g = jnp.concatenate([q[:, (h0 + g) * hd:(h0 + g + 1) * hd] for g in range(group)], axis=0)
        bias = jnp.concatenate([bias_ref[h0 + g] for g in range(group)], axis=1)
        logits.append(jnp.where(mask, _dot_nt(k_g, q_g) + bias, NEG_BIG))
        sink = jnp.full((1, group * w), sink_ref[h0 + group - 1] * LOG2E, F32)
        for g in range(group - 2, -1, -1):
            sink = jnp.where(head_of_lane == g, sink_ref[h0 + g] * LOG2E, sink)
        sinks.append(sink)
    maxima = [jnp.maximum(jnp.max(lg, axis=0, keepdims=True), sink) for lg, sink in zip(logits, sinks)]
    accs = [jnp.dot(vt_g, jnp.exp2(lg - m).astype(BF16), preferred_element_type=F32)
            for vt_g, lg, m in zip(vts, logits, maxima)]
    for kv in range(A_KV_HEADS):
        h0 = kv * group
        acc = accs[kv]
        o_t = acc[0:hd] / (acc[hd:hd + 1] + jnp.exp2(sinks[kv] - maxima[kv]))
        for pair in range(group // 2):
            two = jnp.concatenate([o_t[:, (2 * pair) * w:(2 * pair + 1) * w],
                                   o_t[:, (2 * pair + 1) * w:(2 * pair + 2) * w]], axis=0)
            c0 = (h0 + 2 * pair) * hd
            o_ref[0, :, c0:c0 + 2 * hd] = two.T.astype(o_ref.dtype)


def _swa_attention(qa, ka, va, sinks, band_bias):
    bsz, s, _ = qa.shape
    w = A_WINDOW
    nb = s // w
    kvw = A_KV_HEADS * A_HEAD_DIM
    cur = lambda b, n: (b, n, 0)
    prev = lambda b, n: (b, jnp.maximum(n - 1, 0), 0)
    return pl.pallas_call(
        _swa_kernel,
        grid=(bsz, nb),
        in_specs=[pl.BlockSpec(memory_space=pltpu.SMEM),
                  pl.BlockSpec((1, w, A_HEADS * A_HEAD_DIM), cur),
                  pl.BlockSpec((1, w, kvw), cur),
                  pl.BlockSpec((1, w, kvw), prev),
                  pl.BlockSpec((1, w, kvw), cur),
                  pl.BlockSpec((1, w, kvw), prev),
                  pl.BlockSpec((REL_HEADS, 2 * w, w), lambda b, n: (0, 0, 0))],
        out_specs=pl.BlockSpec((1, w, A_HEADS * A_HEAD_DIM), cur),
        out_shape=jax.ShapeDtypeStruct((bsz, s, A_HEADS * A_HEAD_DIM), BF16),
        compiler_params=_cparams("arbitrary", "arbitrary"),
        name="swa_sink_attention",
    )(sinks, qa, ka, ka, va, va, band_bias)


def _gla_kernel(qk_ref, v_ref, r_ref, ab_ref, gu_ref, gb_ref, on_ref, o_ref, state_ref):
    s_idx = pl.program_id(1)
    c_len = B_CHUNK
    dk, dv = B_KEY_DIM, B_VAL_DIM
    hk = B_HEADS * dk

    @pl.when(s_idx == 0)
    def _():
        state_ref[...] = jnp.zeros(state_ref.shape, F32)

    z = _dot(ab_ref[0], gu_ref[...]) + gb_ref[...]
    log_a = (jnp.minimum(z, 0.0) - jnp.log(1.0 + jnp.exp(-jnp.abs(z)))) / B_GATE_TAU
    rows = qk_ref.shape[1]
    chunks = range(rows // c_len)
    heads = range(B_HEADS)
    ri = lax.broadcasted_iota(jnp.int32, (c_len, c_len), 0)
    ci = lax.broadcasted_iota(jnp.int32, (c_len, c_len), 1)
    tril = ri >= ci
    tri = tril.astype(BF16)
    g_hi = log_a.astype(BF16)
    g_lo = (log_a - g_hi.astype(F32)).astype(BF16)
    b_c = [jnp.dot(tri, g_hi[c * c_len:(c + 1) * c_len], preferred_element_type=F32)
           + jnp.dot(tri, g_lo[c * c_len:(c + 1) * c_len], preferred_element_type=F32) for c in chunks]
    b = jnp.concatenate(b_c, axis=0)
    last_c = [bc[c_len - 1:c_len, :] for bc in b_c]
    b_last = jnp.concatenate([jnp.broadcast_to(l, (c_len, hk)) for l in last_c], axis=0)
    q = qk_ref[0, :, 0:hk] * (dk ** -0.5)
    k = qk_ref[0, :, hk:2 * hk]
    q_dec = (q * jnp.exp(b)).astype(BF16)
    k_dec = (k * jnp.exp(-b)).astype(BF16)
    k_upd = (k * jnp.exp(b_last - b)).astype(BF16)
    v = v_ref[0]

    def rows_of(x, c, cols):
        return x[c * c_len:(c + 1) * c_len, cols]

    att = [[jnp.where(tril, _dot_nt(rows_of(q_dec, c, slice(h * dk, (h + 1) * dk)),
                                    rows_of(k_dec, c, slice(h * dk, (h + 1) * dk))), 0.0).astype(BF16)
            for h in heads] for c in chunks]
    upd = [[_dot_tn(rows_of(v, c, slice(h * dv, (h + 1) * dv)), rows_of(k_upd, c, slice(h * dk, (h + 1) * dk)))
            for h in heads] for c in chunks]
    state_in = []
    st = [state_ref[h] for h in heads]
    for c in chunks:
        state_in.append(st)
        decay = jnp.exp(last_c[c])
        st = [st[h] * decay[:, h * dk:(h + 1) * dk] + upd[c][h] for h in heads]
    for h in heads:
        state_ref[h] = st[h]
    for c in chunks:
        outs = []
        for h in heads:
            o_h = (_dot(att[c][h], rows_of(v, c, slice(h * dv, (h + 1) * dv)))
                   + _dot_nt(rows_of(q_dec, c, slice(h * dk, (h + 1) * dk)), state_in[c][h]))
            o_h = _rms_rows(o_h, on_ref[...])
            r_h = r_ref[0, c * c_len:(c + 1) * c_len, h * dv:(h + 1) * dv]
            outs.append(o_h * (r_h * jax.nn.sigmoid(r_h)))
        o_ref[0, c * c_len:(c + 1) * c_len, :] = jnp.concatenate(outs, axis=-1).astype(o_ref.dtype)


def _gla(qk, vb, rb, ab, gate_up_pad, gate_bias, out_norm):
    bsz, s, _ = qk.shape
    tm = min(GLA_ROWS, s)
    hv = B_HEADS * B_VAL_DIM
    hk = B_HEADS * B_KEY_DIM
    blk = lambda b, i: (b, i, 0)
    const = lambda b, i: (0, 0)
    return pl.pallas_call(
        _gla_kernel,
        grid=(bsz, s // tm),
        in_specs=[pl.BlockSpec((1, tm, 2 * hk), blk),
                  pl.BlockSpec((1, tm, hv), blk),
                  pl.BlockSpec((1, tm, hv), blk),
                  pl.BlockSpec((1, tm, LANES), blk),
                  pl.BlockSpec((LANES, hk), const),
                  pl.BlockSpec((1, hk), const),
                  pl.BlockSpec((1, B_VAL_DIM), const)],
        out_specs=pl.BlockSpec((1, tm, hv), blk),
        out_shape=jax.ShapeDtypeStruct((bsz, s, hv), BF16),
        scratch_shapes=[pltpu.VMEM((B_HEADS, B_VAL_DIM, B_KEY_DIM), F32)],
        compiler_params=_cparams("arbitrary", "arbitrary"),
        name="gated_linear_attention",
    )(qk, vb, rb, ab, gate_up_pad, gate_bias.reshape(1, hk), out_norm.reshape(1, -1))


def _moba_kernel(q_ref, k_ref, v_ref, bias_ref, o_ref, vt_ref, kmean_ref, pick_ref):
    i = pl.program_id(2)
    blk = C_BLOCK
    hd = C_HEAD_DIM
    nb = k_ref.shape[1] // blk
    grp = MOBA_GROUP
    heads = range(MOBA_HEADS_PER_STEP)

    def lanes(hh):
        return slice(hh * hd, (hh + 1) * hd)

    @pl.when(i == 0)
    def _():
        ones_rows = (lax.broadcasted_iota(jnp.int32, (MOBA_VT_ROWS - hd, blk), 0) == 0).astype(BF16)
        for hh in heads:
            kn = k_ref[0, :, lanes(hh)].astype(F32)
            kmean_ref[hh] = jnp.mean(kn.reshape(nb, blk, hd), axis=1)
            for j in range(nb):
                vt = v_ref[0, j * blk:(j + 1) * blk, lanes(hh)].astype(F32).T.astype(BF16)
                vt_ref[hh, j] = jnp.concatenate([vt, ones_rows], axis=0)

    def logits(hh, q_h, j):
        j0 = pl.multiple_of(j * blk, blk)
        return _dot_nt(k_ref[0, pl.ds(j0, blk), lanes(hh)], q_h)

    def col_max(lg, picked):
        return jnp.where(picked, jnp.max(lg, axis=0, keepdims=True), NEG_BIG)

    def weighted_v(hh, j, lg, m, picked):
        p = jnp.exp2(lg - m).astype(BF16)
        return jnp.where(picked, jnp.dot(vt_ref[hh, j], p, preferred_element_type=F32), 0.0)

    def softmax_step(ms, accs, tiles):
        m_new = []
        for hh in heads:
            m_h = ms[hh]
            for _, lg, picked in tiles[hh]:
                m_h = jnp.maximum(m_h, col_max(lg, picked))
            m_new.append(m_h)
        out = []
        for hh in heads:
            acc = accs[hh] * jnp.exp2(ms[hh] - m_new[hh])
            for j, lg, picked in tiles[hh]:
                acc = acc + weighted_v(hh, j, lg, m_new[hh], picked)
            out.append(acc)
        return tuple(m_new), tuple(out)

    jrow = lax.broadcasted_iota(jnp.int32, (nb, blk), 0)
    key = lax.broadcasted_iota(jnp.int32, (blk, blk), 0)
    qry = lax.broadcasted_iota(jnp.int32, (blk, blk), 1)
    j_prev = jnp.maximum(i - 1, 0)
    qs = []
    for hh in heads:
        q_h = q_ref[0, :, lanes(hh)]
        qs.append(q_h)
        score = jnp.where(jrow < i, _dot_nt(kmean_ref[hh], q_h), -jnp.inf)
        beats = jnp.zeros((nb, blk), jnp.int32)
        for j2 in range(nb):
            s2 = score[j2:j2 + 1, :]
            beats = beats + ((s2 > score) | ((s2 == score) & (j2 < jrow))).astype(jnp.int32)
        pick_ref[hh] = ((beats < C_TOPK) & (jrow < i)).astype(F32)

    first = []
    for hh in heads:
        lg_own = jnp.where(key <= qry, logits(hh, qs[hh], i) + bias_ref[hh, 0:blk, :], NEG_BIG)
        lg_prev = logits(hh, qs[hh], j_prev) + bias_ref[hh, blk:2 * blk, :]
        pick_prev = pick_ref[hh, pl.ds(j_prev, 1), :] > 0.5
        first.append([(i, lg_own, True), (j_prev, lg_prev, pick_prev)])
    start = (tuple(jnp.full((1, blk), NEG_BIG, F32) for _ in heads),
             tuple(jnp.zeros((MOBA_VT_ROWS, blk), F32) for _ in heads))
    carry0 = softmax_step(start[0], start[1], first)

    n_far = jnp.maximum(i - 1, 0)

    def body(g, carry):
        tiles = []
        for hh in heads:
            row = []
            for u in range(grp):
                j = g * grp + u
                jc = jnp.minimum(j, nb - 1)
                picked = (pick_ref[hh, pl.ds(jc, 1), :] > 0.5) & (j < n_far)
                row.append((jc, logits(hh, qs[hh], jc), picked))
            tiles.append(row)
        return softmax_step(carry[0], carry[1], tiles)

    _, accs = lax.fori_loop(0, (n_far + grp - 1) // grp, body, carry0)
    for hh in heads:
        o_ref[0, :, lanes(hh)] = (accs[hh][0:hd] / accs[hh][hd:hd + 1]).T.astype(o_ref.dtype)


def _moba(q, k, v, bias_tiles):
    bsz, s, _ = q.shape
    blk, hd = C_BLOCK, C_HEAD_DIM
    assert s % blk == 0
    nb = s // blk
    hb = MOBA_HEADS_PER_STEP
    assert C_HEADS % hb == 0
    return pl.pallas_call(
        _moba_kernel,
        grid=(bsz, C_HEADS // hb, nb),
        in_specs=[pl.BlockSpec((1, blk, hb * hd), lambda b, h, i: (b, i, h)),
                  pl.BlockSpec((1, s, hb * hd), lambda b, h, i: (b, 0, h)),
                  pl.BlockSpec((1, s, hb * hd), lambda b, h, i: (b, 0, h)),
                  pl.BlockSpec((hb, 2 * blk, blk), lambda b, h, i: (h, 0, 0))],
        out_specs=pl.BlockSpec((1, blk, hb * hd), lambda b, h, i: (b, i, h)),
        out_shape=jax.ShapeDtypeStruct((bsz, s, C_HEADS * hd), BF16),
        scratch_shapes=[pltpu.VMEM((hb, nb, MOBA_VT_ROWS, blk), BF16),
                        pltpu.VMEM((hb, nb, hd), F32), pltpu.VMEM((hb, nb, blk), F32)],
        compiler_params=_cparams("arbitrary", "arbitrary", "arbitrary"),
        name="moba_attention",
    )(q, k, v, bias_tiles)


def _out_route_kernel(*refs, n_in):
    a_refs = refs[:n_in]
    w_refs = refs[n_in:2 * n_in]
    (x_ref, mod_ref, nw_ref, wr_ref, br_ref,
     xo_ref, h_ref, ids_ref, gate_ref, rank_ref, cnt_ref, base_ref) = refs[2 * n_in:]
    first = (pl.program_id(0) == 0) & (pl.program_id(1) == 0)

    @pl.when(first)
    def _():
        base_ref[...] = jnp.zeros(base_ref.shape, F32)

    y = _dot(a_refs[0][0], w_refs[0][...])
    for a_ref, w_ref in zip(a_refs[1:], w_refs[1:]):
        y = y + _dot(a_ref[0], w_ref[...])
    x = x_ref[0] + mod_ref[0, 2:3, :] * y
    xo_ref[0] = x

    tm = x.shape[0]
    h = _rms_rows(x, nw_ref[...])
    h = h * (1.0 + mod_ref[0, 4:5, :]) + mod_ref[0, 3:4, :]
    _to_row_tiles(h_ref, h)
    lt = _dot_nt(wr_ref[...], h) + br_ref[...]
    g = [lt[r:r + 1, :] for r in range(MOE_GROUPS)]
    gmax = functools.reduce(jnp.maximum, g)
    gsel = jnp.full(gmax.shape, MOE_GROUPS - 1, jnp.int32)
    for r in range(MOE_GROUPS - 2, -1, -1):
        gsel = jnp.where(g[r] == gmax, r, gsel)
    p_g = 1.0 / functools.reduce(jnp.add, [jnp.exp(gr - gmax) for gr in g])
    epg = MOE_EXPERTS_PER_GROUP
    e_in = lt[8 + (MOE_GROUPS - 1) * epg:8 + MOE_GROUPS * epg, :]
    for r in range(MOE_GROUPS - 2, -1, -1):
        e_in = jnp.where(gsel == r, lt[8 + r * epg:8 + (r + 1) * epg, :], e_in)
    sub = lax.broadcasted_iota(jnp.int32, (epg, tm), 0)
    v1 = jnp.max(e_in, axis=0, keepdims=True)
    i1 = jnp.min(jnp.where(e_in == v1, sub, epg), axis=0, keepdims=True)
    e2 = jnp.where(sub == i1, -jnp.inf, e_in)
    v2 = jnp.max(e2, axis=0, keepdims=True)
    i2 = jnp.min(jnp.where(e2 == v2, sub, epg), axis=0, keepdims=True)
    t = jnp.exp(v2 - v1)
    w1 = p_g / (1.0 + t)
    w2 = p_g * t / (1.0 + t)
    id1 = gsel * epg + i1
    id2 = gsel * epg + i2
    ids_ref[...] = jnp.concatenate([id1, id2], axis=0)
    gate_ref[...] = jnp.concatenate([w1, w2, jnp.zeros((LANES - 2, tm), F32)], axis=0).T

    eidx = lax.broadcasted_iota(jnp.int32, (MOE_EXPERTS, tm), 0)
    oh1 = eidx == id1
    oh2 = eidx == id2
    onehot = (oh1 | oh2).astype(BF16)
    tr = lax.broadcasted_iota(jnp.int32, (tm, tm), 0)
    tc = lax.broadcasted_iota(jnp.int32, (tm, tm), 1)
    before = (tr < tc).astype(BF16)
    prefix = jnp.dot(onehot, before, preferred_element_type=F32) + base_ref[...]
    r1 = jnp.sum(jnp.where(oh1, prefix, 0.0), axis=0, keepdims=True)
    r2 = jnp.sum(jnp.where(oh2, prefix, 0.0), axis=0, keepdims=True)
    rank_ref[...] = jnp.concatenate([r1, r2], axis=0).astype(jnp.int32)
    base_ref[...] = base_ref[...] + jnp.sum(onehot.astype(F32), axis=1, keepdims=True)
    cnt_ref[...] = base_ref[...].astype(jnp.int32)


def _out_project_route(acts, weights, x, mod, norm_w, wr_t, br):
    bsz, s, d = x.shape
    tm = min(ROUTER_ROWS, s)
    n = bsz * s
    nt = s // tm
    n_in = len(acts)
    tok = lambda b, i: (0, b * nt + i)
    const = lambda b, i: (0, 0)
    in_specs = [pl.BlockSpec((1, tm, a.shape[2]), lambda b, i: (b, i, 0)) for a in acts]
    in_specs += [pl.BlockSpec(w.shape, const) for w in weights]
    in_specs += [pl.BlockSpec((1, tm, d), lambda b, i: (b, i, 0)),
                 pl.BlockSpec((1, 6, d), lambda b, i: (b, 0, 0)),
                 pl.BlockSpec((1, d), const),
                 pl.BlockSpec((ROUTER_LOGIT_ROWS, d), const),
                 pl.BlockSpec((ROUTER_LOGIT_ROWS, 1), const)]
    return pl.pallas_call(
        functools.partial(_out_route_kernel, n_in=n_in),
        grid=(bsz, nt),
        in_specs=in_specs,
        out_specs=[pl.BlockSpec((1, tm, d), lambda b, i: (b, i, 0)),
                   pl.BlockSpec((tm * d // LANES, LANES), lambda b, i: (b * nt + i, 0)),
                   pl.BlockSpec((MOE_TOPK, tm), tok),
                   pl.BlockSpec((tm, LANES), lambda b, i: (b * nt + i, 0)),
                   pl.BlockSpec((MOE_TOPK, tm), tok),
                   pl.BlockSpec((MOE_EXPERTS, 1), const)],
        out_shape=[jax.ShapeDtypeStruct((bsz, s, d), F32),
                   jax.ShapeDtypeStruct((n * d // LANES, LANES), F32),
                   jax.ShapeDtypeStruct((MOE_TOPK, n), jnp.int32),
                   jax.ShapeDtypeStruct((n, LANES), F32),
                   jax.ShapeDtypeStruct((MOE_TOPK, n), jnp.int32),
                   jax.ShapeDtypeStruct((MOE_EXPERTS, 1), jnp.int32)],
        scratch_shapes=[pltpu.VMEM((MOE_EXPERTS, 1), F32)],
        compiler_params=_cparams("arbitrary", "arbitrary"),
        name="out_project_route",
    )(*acts, *weights, x, mod, norm_w.reshape(1, d), wr_t, br)


def _dest_kernel(ids_ref, rank_ref, start_ref, o_ref):
    ids = ids_ref[...]
    tm = ids.shape[1]
    eidx = lax.broadcasted_iota(jnp.int32, (MOE_EXPERTS, tm), 0)
    rows = []
    for k in range(MOE_TOPK):
        base = jnp.sum(jnp.where(eidx == ids[k:k + 1, :], start_ref[...], 0), axis=0, keepdims=True)
        rows.append(base + rank_ref[k:k + 1, :])
    o_ref[...] = jnp.concatenate(rows, axis=0)


def _dest_rows(ids, rank, start):
    n = ids.shape[1]
    tm = min(2048, n)
    return pl.pallas_call(
        _dest_kernel,
        grid=(n // tm,),
        in_specs=[pl.BlockSpec((MOE_TOPK, tm), lambda i: (0, i)),
                  pl.BlockSpec((MOE_TOPK, tm), lambda i: (0, i)),
                  pl.BlockSpec((MOE_EXPERTS, 1), lambda i: (0, 0))],
        out_specs=pl.BlockSpec((MOE_TOPK, tm), lambda i: (0, i)),
        out_shape=jax.ShapeDtypeStruct((MOE_TOPK, n), jnp.int32),
        compiler_params=_cparams("arbitrary"),
        name="moe_dest_rows",
    )(ids, rank, start)


def _row_copy(src, dst, sem, src_row, dst_row, rt):
    return pltpu.make_async_copy(src.at[pl.ds(pl.multiple_of(src_row * rt, rt), rt)],
                                 dst.at[pl.ds(pl.multiple_of(dst_row * rt, rt), rt)], sem)


def _dispatch_kernel(pend_ref, padded_ref, dest_ref, h_ref, h_hbm_ref, xs_ref, zero_ref, sem, *, rt):
    tm = h_ref.shape[0] // rt
    tok0 = pl.program_id(0) * tm
    zrows = zero_ref.shape[0] // rt

    @pl.when(pl.program_id(0) == 0)
    def _():
        zero_ref[...] = jnp.zeros(zero_ref.shape, F32)
        for e in range(MOE_EXPERTS):
            @pl.when(padded_ref[e] > 0)
            def _():
                row0 = pl.multiple_of((pend_ref[e] - zrows) * rt, zrows * rt)
                cp = pltpu.make_async_copy(zero_ref, xs_ref.at[pl.ds(row0, zrows * rt)], sem)
                cp.start()
                cp.wait()

        def zero_tail(b, _):
            row0 = pl.multiple_of(b * zrows * rt, zrows * rt)
            cp = pltpu.make_async_copy(zero_ref, xs_ref.at[pl.ds(row0, zrows * rt)], sem)
            cp.start()
            cp.wait()
            return 0

        lax.fori_loop(pend_ref[MOE_EXPERTS - 1] // zrows, xs_ref.shape[0] // (zrows * rt), zero_tail, 0)

    def start(t, _):
        _row_copy(h_ref, xs_ref, sem, t, dest_ref[0, t], rt).start()
        _row_copy(h_hbm_ref, xs_ref, sem, tok0 + t, dest_ref[1, t], rt).start()
        return 0

    lax.fori_loop(0, tm, start, 0, unroll=DMA_ISSUE_UNROLL)
    for k in range(MOE_TOPK):
        pltpu.make_async_copy(h_ref, xs_ref.at[pl.ds(0, tm * rt)], sem).wait()


def _dispatch(h_tiles, dest, pend, padded, n_rows, rt):
    n = h_tiles.shape[0] // rt
    tm = min(DISPATCH_ROWS, n)
    return pl.pallas_call(
        functools.partial(_dispatch_kernel, rt=rt),
        grid=(n // tm,),
        in_specs=[pl.BlockSpec(memory_space=pltpu.SMEM),
                  pl.BlockSpec(memory_space=pltpu.SMEM),
                  pl.BlockSpec((MOE_TOPK, tm), lambda i: (0, i), memory_space=pltpu.SMEM),
                  pl.BlockSpec((tm * rt, LANES), lambda i: (i, 0)),
                  pl.BlockSpec(memory_space=pl.ANY)],
        out_specs=pl.BlockSpec(memory_space=pl.ANY),
        out_shape=jax.ShapeDtypeStruct((n_rows * rt, LANES), F32),
        scratch_shapes=[pltpu.VMEM((MOE_BLOCK_ROWS * rt, LANES), F32), pltpu.SemaphoreType.DMA(())],
        compiler_params=_cparams("arbitrary"),
        name="moe_dispatch",
    )(pend, padded, dest, h_tiles, h_tiles)


def _expert_kernel(blk_e_ref, n_used_ref, x_ref, w1_ref, w3_ref, w2_ref, o_ref, w1b_ref, w3b_ref, w2b_ref):
    i = pl.program_id(0)
    used = i < n_used_ref[0]
    new_expert = (i == 0) | (blk_e_ref[i] != blk_e_ref[jnp.maximum(i - 1, 0)])

    @pl.when(used & new_expert)
    def _():
        w1b_ref[...] = w1_ref[0, 0].astype(BF16)
        w3b_ref[...] = w3_ref[0, 0].astype(BF16)
        w2b_ref[...] = w2_ref[0, 0].astype(BF16)

    @pl.when(used)
    def _():
        d = w1b_ref.shape[0]
        rows = x_ref.shape[0] * LANES // d
        x = _from_row_tiles(x_ref, rows, d).astype(BF16)
        a = jnp.dot(x, w1b_ref[...], preferred_element_type=F32)
        g = jnp.dot(x, w3b_ref[...], preferred_element_type=F32)
        y = jnp.dot((a * jax.nn.sigmoid(a) * g).astype(BF16), w2b_ref[...], preferred_element_type=F32)
        _to_row_tiles(o_ref, y)

    @pl.when(jnp.logical_not(used))
    def _():
        o_ref[...] = jnp.zeros(o_ref.shape, F32)


def _expert_ffn(xs, blk_e, n_used, w1, w3, w2, layer):
    d, hid = w1.shape[2], w1.shape[3]
    rt = d // LANES
    n_rows = xs.shape[0] // rt
    tm = MOE_BLOCK_ROWS
    row = lambda i, be, nu: (jnp.minimum(i, nu[0] - 1), 0)
    wsel = lambda i, be, nu: (layer, be[i], 0, 0)
    return pl.pallas_call(
        _expert_kernel,
        grid_spec=pltpu.PrefetchScalarGridSpec(
            num_scalar_prefetch=2,
            grid=(n_rows // tm,),
            in_specs=[pl.BlockSpec((tm * rt, LANES), row),
                      pl.BlockSpec((1, 1, d, hid), wsel),
                      pl.BlockSpec((1, 1, d, hid), wsel),
                      pl.BlockSpec((1, 1, hid, d), wsel)],
            out_specs=pl.BlockSpec((tm * rt, LANES), lambda i, be, nu: (i, 0)),
            scratch_shapes=[pltpu.VMEM((d, hid), BF16), pltpu.VMEM((d, hid), BF16),
                            pltpu.VMEM((hid, d), BF16)]),
        out_shape=jax.ShapeDtypeStruct((n_rows * rt, LANES), F32),
        compiler_params=_cparams("arbitrary"),
        name="moe_expert_ffn",
    )(blk_e, n_used, xs, w1, w3, w2)


def _combine_kernel(dest_ref, next_dest_ref, ys_ref, gate_ref, x_ref, mod_ref, *refs, outs):
    buf_ref, sem = refs[-2:]
    o_ref = refs[-3 - len(outs)]
    tm, d = x_ref.shape[1], x_ref.shape[2]
    rt = d // LANES
    step = pl.program_id(0) * pl.num_programs(1) + pl.program_id(1)
    n_steps = pl.num_programs(0) * pl.num_programs(1)
    slot = step % 2

    def gather(rows_ref, s):
        def start(t, _):
            for k in range(MOE_TOPK):
                _row_copy(ys_ref, buf_ref.at[s, k], sem.at[s], rows_ref[k, t], t, rt).start(priority=k)
            return 0
        lax.fori_loop(0, tm, start, 0, unroll=DMA_ISSUE_UNROLL)

    @pl.when(step == 0)
    def _():
        gather(dest_ref, 0)

    @pl.when(step + 1 < n_steps)
    def _():
        gather(next_dest_ref, 1 - slot)

    for k in range(MOE_TOPK):
        pltpu.make_async_copy(ys_ref.at[pl.ds(0, tm * rt)], buf_ref.at[slot, k], sem.at[slot]).wait()
    g = gate_ref[...]
    y = (g[:, 0:1] * _from_row_tiles(buf_ref, tm, d, (slot, 0))
         + g[:, 1:2] * _from_row_tiles(buf_ref, tm, d, (slot, 1)))
    x = x_ref[0] + mod_ref[0, 5:6, :] * y
    o_ref[0] = x
    if outs:
        nw_ref, next_mod_ref, w_ref = refs[0:3]
        n_norm = len(refs) - 6 - len(outs)
        _project_rows(x, nw_ref, next_mod_ref, w_ref, refs[3:3 + n_norm], refs[-2 - len(outs):-2], outs)


def _combine(ys, dest, gates, x, mod, proj=None):
    bsz, s, d = x.shape
    tm = min(COMBINE_ROWS, s)
    nt = s // tm
    last = bsz * nt - 1
    tile = lambda b, i: (b, i, 0)
    const = lambda b, i: (0, 0)
    in_specs = [pl.BlockSpec((MOE_TOPK, tm), lambda b, i: (0, b * nt + i), memory_space=pltpu.SMEM),
                pl.BlockSpec((MOE_TOPK, tm), lambda b, i: (0, jnp.minimum(b * nt + i + 1, last)),
                             memory_space=pltpu.SMEM),
                pl.BlockSpec(memory_space=pl.ANY),
                pl.BlockSpec((tm, LANES), lambda b, i: (b * nt + i, 0)),
                pl.BlockSpec((1, tm, d), tile),
                pl.BlockSpec((1, 6, d), lambda b, i: (b, 0, 0))]
    args = [dest, dest, ys, gates, x, mod]
    out_specs = [pl.BlockSpec((1, tm, d), tile)]
    out_shape = [jax.ShapeDtypeStruct((bsz, s, d), F32)]
    outs = ()
    if proj is not None:
        norm_w, next_mod, w_bf16, outs, dtypes, head_norms = proj
        in_specs += [pl.BlockSpec((1, d), const), pl.BlockSpec((1, 6, d), lambda b, i: (b, 0, 0)),
                     pl.BlockSpec(w_bf16.shape, const)]
        in_specs += [pl.BlockSpec((1, hn.shape[0]), const) for hn in head_norms]
        args += [norm_w.reshape(1, d), next_mod, w_bf16] + [hn.reshape(1, -1) for hn in head_norms]
        out_specs += [pl.BlockSpec((1, tm, o[0]), tile) for o in outs]
        out_shape += [jax.ShapeDtypeStruct((bsz, s, o[0]), dt) for o, dt in zip(outs, dtypes)]
    res = pl.pallas_call(
        functools.partial(_combine_kernel, outs=tuple(outs)),
        grid=(bsz, nt),
        in_specs=in_specs,
        out_specs=out_specs,
        out_shape=out_shape,
        scratch_shapes=[pltpu.VMEM((2, MOE_TOPK, tm * d // LANES, LANES), F32), pltpu.SemaphoreType.DMA((2,))],
        compiler_params=_cparams("arbitrary", "arbitrary"),
        name="moe_combine" if proj is None else "moe_combine_project",
    )(*args)
    return res[0] if proj is None else res


def _mixer_out_moe_residual(acts, w_outs, x, norm_w, mod, w_group, b_group, w_expert, b_expert, w1, w3, w2, layer,
                            next_proj):
    bsz, s, d = x.shape
    n = bsz * s
    tm = MOE_BLOCK_ROWS
    wr_t = jnp.zeros((ROUTER_LOGIT_ROWS, d), F32).at[0:MOE_GROUPS].set(w_group.T).at[8:].set(w_expert.T)
    br = jnp.zeros((ROUTER_LOGIT_ROWS, 1), F32).at[0:MOE_GROUPS, 0].set(b_group).at[8:, 0].set(b_expert)
    x_mid, h2, ids, gates, rank, counts = _out_project_route(acts, w_outs, x, mod, norm_w, wr_t.astype(BF16), br)
    counts = counts[:, 0]
    padded = ((counts + tm - 1) // tm) * tm
    pend = jnp.cumsum(padded).astype(jnp.int32)
    start = pend - padded
    n_rows = ((n * MOE_TOPK + tm - 1) // tm) * tm + MOE_EXPERTS * tm
    blk_start = jnp.arange(n_rows // tm, dtype=jnp.int32) * tm
    blk_e = jnp.minimum(jnp.sum((pend[None, :] <= blk_start[:, None]).astype(jnp.int32), axis=1),
                        MOE_EXPERTS - 1)
    n_used = pend[-1:] // tm
    dest = _dest_rows(ids, rank, start.reshape(MOE_EXPERTS, 1))
    xs = _dispatch(h2, dest, pend, padded, n_rows, d // LANES)
    ys = _expert_ffn(xs, blk_e, n_used, w1, w3, w2, layer)
    return _combine(ys, dest, gates, x_mid, mod, next_proj)


def _even_proj_spec(w_in, q_norm, k_norm):
    d, n_in = w_in.shape
    aq = A_HEADS * A_HEAD_DIM
    akv = A_KV_HEADS * A_HEAD_DIM
    bk = B_HEADS * B_KEY_DIM
    bv = B_HEADS * B_VAL_DIM
    w_pad = jnp.zeros((d, n_in - B_GATE_RANK + LANES), F32).at[:, :n_in].set(w_in).astype(BF16)
    outs = ((aq, A_HEAD_DIM, 0, A_HEAD_DIM ** -0.5 * LOG2E), (akv, A_HEAD_DIM, 1, 1.0), (akv, 0, 0, 1.0),
            (2 * bk, 0, 0, 1.0), (bv, 0, 0, 1.0), (bv, 0, 0, 1.0), (LANES, 0, 0, 1.0))
    dtypes = (BF16, BF16, BF16, F32, BF16, F32, F32)
    return w_pad, outs, dtypes, (q_norm, k_norm)


def _odd_proj_spec(w_in, q_norm, k_norm):
    mix = C_HEADS * C_HEAD_DIM
    outs = ((mix, C_HEAD_DIM, 0, C_HEAD_DIM ** -0.5 * LOG2E), (mix, C_HEAD_DIM, 1, 1.0), (mix, 0, 0, 1.0))
    return w_in.astype(BF16), outs, (BF16, BF16, BF16), (q_norm, k_norm)


def _even_mixer(groups, w_out, sinks, gate_up, gate_bias, out_norm, band_bias):
    qa, ka, va, qk, vb, rb, ab = groups
    aq = A_HEADS * A_HEAD_DIM
    bk = B_HEADS * B_KEY_DIM
    oa = _swa_attention(qa, ka, va, sinks, band_bias)
    gu_pad = jnp.zeros((LANES, bk), F32).at[:B_GATE_RANK].set(gate_up).astype(BF16)
    ob = _gla(qk, vb, rb, ab, gu_pad, gate_bias, out_norm)
    w_out = w_out.astype(BF16)
    return [oa, ob], [w_out[:aq], w_out[aq:]]


def _odd_mixer(groups, w_out, bias_tiles):
    q, k, v = groups
    return [_moba(q, k, v, bias_tiles)], [w_out.astype(BF16)]


def kernel(x, c, rel_bias, ada_w, ada_b, norm1_w, norm2_w, even_w_in, even_w_out, a_q_norm, a_k_norm, a_sinks, b_gate_up, b_gate_bias, b_out_norm, odd_w_in, odd_w_out, c_q_norm, c_k_norm, moe_w_group, moe_b_group, moe_w_expert, moe_b_expert, moe_w1, moe_w3, moe_w2):
    depth = ada_w.shape[0]
    bsz, _, d = x.shape
    moba_bias, band_bias = _bias_tiles(rel_bias)
    mod_all = _adaln(c, ada_w, ada_b).reshape(depth, bsz, 6, d)

    def proj_spec(layer):
        j = layer // 2
        if layer % 2 == 0:
            return _even_proj_spec(even_w_in[j], a_q_norm[j], a_k_norm[j])
        return _odd_proj_spec(odd_w_in[j], c_q_norm[j], c_k_norm[j])

    w_in, outs, dtypes, head_norms = proj_spec(0)
    groups = _norm_mod_project(x, norm1_w[0], mod_all[0], w_in, outs, dtypes, head_norms)
    for layer in range(depth):
        mod = mod_all[layer]
        j = layer // 2
        if layer % 2 == 0:
            acts, w_outs = _even_mixer(groups, even_w_out[j], a_sinks[j], b_gate_up[j], b_gate_bias[j],
                                       b_out_norm[j], band_bias)
        else:
            acts, w_outs = _odd_mixer(groups, odd_w_out[j], moba_bias)
        next_proj = None
        if layer + 1 < depth:
            w_in, outs, dtypes, head_norms = proj_spec(layer + 1)
            next_proj = (norm1_w[layer + 1], mod_all[layer + 1], w_in, outs, dtypes, head_norms)
        res = _mixer_out_moe_residual(acts, w_outs, x, norm2_w[layer], mod, moe_w_group[layer],
                                      moe_b_group[layer], moe_w_expert[layer], moe_b_expert[layer],
                                      moe_w1, moe_w3, moe_w2, layer, next_proj)
        x, groups = (res, None) if next_proj is None else (res[0], res[1:])
    return x
```

```python
import functools
import math

import jax
import jax.numpy as jnp
import numpy as np
from jax import lax
from jax.experimental import pallas as pl
from jax.experimental.pallas import tpu as pltpu

RMS_EPS = 1e-6
A_HEADS, A_KV_HEADS, A_HEAD_DIM, A_WINDOW = 8, 2, 64, 128
B_HEADS, B_KEY_DIM, B_VAL_DIM, B_GATE_RANK, B_GATE_TAU, B_CHUNK = 4, 64, 128, 16, 16.0, 64
C_HEADS, C_HEAD_DIM, C_BLOCK, C_TOPK = 8, 128, 256, 3
REL_BUCKETS, REL_MAX_DIST, REL_HEADS = 32, 128, 8
MOE_GROUPS, MOE_EXPERTS_PER_GROUP, MOE_TOPK = 4, 8, 2
MOE_EXPERTS = MOE_GROUPS * MOE_EXPERTS_PER_GROUP

LANES = 128
V7X_VMEM_LIMIT_BYTES = 56 * 1024 * 1024
NEG_BIG = -1e30
PROJ_ROWS = 512
GLA_ROWS = 512
ROUTER_ROWS = 512
MOE_BLOCK_ROWS = 512
DISPATCH_ROWS = 1024
COMBINE_ROWS = 512
DMA_ISSUE_UNROLL = 8
ROUTER_LOGIT_ROWS = 8 + MOE_EXPERTS
MOBA_GROUP = 4
MOBA_HEADS_PER_STEP = 4
LOG2E = math.log2(math.e)
SWA_VT_ROWS = A_HEAD_DIM + 16
MOBA_VT_ROWS = C_HEAD_DIM + 16

F32 = jnp.float32
BF16 = jnp.bfloat16


def _cparams(*sem):
    return pltpu.CompilerParams(dimension_semantics=sem, vmem_limit_bytes=V7X_VMEM_LIMIT_BYTES)


def _rel_bucket_np(dist):
    exact = REL_BUCKETS // 2
    d = np.maximum(dist, 0)
    logd = np.log(np.maximum(d, 1).astype(np.float64) / exact) / math.log(REL_MAX_DIST / exact)
    far = np.minimum(exact + (logd * (REL_BUCKETS - exact)).astype(np.int64), REL_BUCKETS - 1)
    return np.where(d < exact, d, far).astype(np.int32)


def _bucket_tiles():
    li = np.arange(C_BLOCK)
    own = _rel_bucket_np(li[None, :] - li[:, None])
    prev = _rel_bucket_np(li[None, :] - li[:, None] + C_BLOCK)
    band = _rel_bucket_np(np.arange(A_WINDOW)[None, :] + A_WINDOW - np.arange(2 * A_WINDOW)[:, None])
    return np.concatenate([own, prev], axis=0), band


def _rms_rows(x, w):
    return x * lax.rsqrt(jnp.mean(x * x, axis=-1, keepdims=True) + RMS_EPS) * w


def _dot(a, b):
    return jnp.dot(a.astype(BF16), b.astype(BF16), preferred_element_type=F32)


def _dot_nt(a, b):
    return lax.dot_general(a.astype(BF16), b.astype(BF16), (((1,), (1,)), ((), ())),
                           preferred_element_type=F32)


def _to_row_tiles(ref, x, lead=()):
    rows, d = x.shape
    chunks = d // LANES
    for c in range(chunks):
        ref[lead + (pl.ds(c, rows, stride=chunks), slice(None))] = x[:, c * LANES:(c + 1) * LANES]


def _from_row_tiles(ref, rows, d, lead=()):
    chunks = d // LANES
    return jnp.concatenate([ref[lead + (pl.ds(c, rows, stride=chunks), slice(None))] for c in range(chunks)],
                           axis=1)


def _dot_tn(a, b):
    return lax.dot_general(a.astype(BF16), b.astype(BF16), (((0,), (0,)), ((), ())),
                           preferred_element_type=F32)


def _bias_kernel(rb_ref, bkt_ref, o_ref, *, relative_to_last):
    h = pl.program_id(0)
    bkt = bkt_ref[...]
    acc = jnp.zeros(bkt.shape, F32)
    for b in range(REL_BUCKETS):
        acc = jnp.where(bkt == b, rb_ref[b, h], acc)
    if relative_to_last:
        acc = acc - rb_ref[REL_BUCKETS - 1, h]
    o_ref[0] = acc * LOG2E


def _bias_tile(rel_bias, bkt, relative_to_last):
    rows, cols = bkt.shape
    return pl.pallas_call(
        functools.partial(_bias_kernel, relative_to_last=relative_to_last),
        grid=(REL_HEADS,),
        in_specs=[pl.BlockSpec(memory_space=pltpu.SMEM),
                  pl.BlockSpec((rows, cols), lambda h: (0, 0))],
        out_specs=pl.BlockSpec((1, rows, cols), lambda h: (h, 0, 0)),
        out_shape=jax.ShapeDtypeStruct((REL_HEADS, rows, cols), F32),
        compiler_params=_cparams("arbitrary"),
        name="rel_bias_tiles",
    )(rel_bias, jnp.asarray(bkt))


def _bias_tiles(rel_bias):
    moba_bkt, band_bkt = _bucket_tiles()
    return _bias_tile(rel_bias, moba_bkt, True), _bias_tile(rel_bias, band_bkt, False)


def _adaln_kernel(c_ref, w_ref, b_ref, o_ref):
    c = c_ref[...]
    cond = c * jax.nn.sigmoid(c)
    o_ref[0] = _dot(cond, w_ref[0]) + b_ref[0]


def _adaln(c, ada_w, ada_b):
    depth, d, n6 = ada_w.shape
    bsz = c.shape[0]
    tn = 1536 if n6 % 1536 == 0 else n6
    return pl.pallas_call(
        _adaln_kernel,
        grid=(depth, n6 // tn),
        in_specs=[pl.BlockSpec((bsz, d), lambda l, j: (0, 0)),
                  pl.BlockSpec((1, d, tn), lambda l, j: (l, 0, j)),
                  pl.BlockSpec((1, 1, tn), lambda l, j: (l, 0, j))],
        out_specs=pl.BlockSpec((1, bsz, tn), lambda l, j: (l, 0, j)),
        out_shape=jax.ShapeDtypeStruct((depth, bsz, n6), F32),
        compiler_params=_cparams("arbitrary", "arbitrary"),
        name="adaln_mod",
    )(c, ada_w, ada_b.reshape(depth, 1, n6))


def _head_rms(y, w_ref, hd, post_scale):
    cols = y.shape[1]
    lane = lax.broadcasted_iota(jnp.int32, (1, LANES), 1)
    w = w_ref[...] * post_scale
    out = []
    for c in range(cols // LANES):
        t = y[:, c * LANES:(c + 1) * LANES]
        sq = t * t
        if hd == LANES:
            r = lax.rsqrt(jnp.sum(sq, axis=-1, keepdims=True) * (1.0 / hd) + RMS_EPS)
            out.append(t * r * w)
        else:
            low = lane < hd
            s_lo = jnp.sum(jnp.where(low, sq, 0.0), axis=-1, keepdims=True)
            s_hi = jnp.sum(jnp.where(low, 0.0, sq), axis=-1, keepdims=True)
            r = lax.rsqrt(jnp.where(low, s_lo, s_hi) * (1.0 / hd) + RMS_EPS)
            out.append(t * r * jnp.concatenate([w, w], axis=1))
    return jnp.concatenate(out, axis=1) if len(out) > 1 else out[0]


def _project_rows(x, nw_ref, mod_ref, w_ref, norm_refs, o_refs, outs):
    h = _rms_rows(x, nw_ref[...])
    h = h * (1.0 + mod_ref[0, 1:2, :]) + mod_ref[0, 0:1, :]
    y = _dot(h, w_ref[...])
    off = 0
    for o_ref, (wd, hd, widx, post) in zip(o_refs, outs):
        t = y[:, off:off + wd]
        if hd:
            t = _head_rms(t, norm_refs[widx], hd, post)
        o_ref[0] = t.astype(o_ref.dtype)
        off += wd


def _proj_kernel(x_ref, nw_ref, mod_ref, w_ref, *refs, outs):
    n_norm = len(refs) - len(outs)
    _project_rows(x_ref[0], nw_ref, mod_ref, w_ref, refs[:n_norm], refs[n_norm:], outs)


def _norm_mod_project(x, norm_w, mod, w_bf16, outs, dtypes, head_norms):
    bsz, s, d = x.shape
    tm = min(PROJ_ROWS, s)
    n = w_bf16.shape[1]
    widths = [o[0] for o in outs]
    assert sum(widths) == n and all(wd % LANES == 0 for wd in widths)
    return pl.pallas_call(
        functools.partial(_proj_kernel, outs=tuple(outs)),
        grid=(bsz, s // tm),
        in_specs=[pl.BlockSpec((1, tm, d), lambda b, i: (b, i, 0)),
                  pl.BlockSpec((1, d), lambda b, i: (0, 0)),
                  pl.BlockSpec((1, 6, d), lambda b, i: (b, 0, 0)),
                  pl.BlockSpec((d, n), lambda b, i: (0, 0))]
                 + [pl.BlockSpec((1, hn.shape[0]), lambda b, i: (0, 0)) for hn in head_norms],
        out_specs=[pl.BlockSpec((1, tm, wd), lambda b, i: (b, i, 0)) for wd in widths],
        out_shape=[jax.ShapeDtypeStruct((bsz, s, wd), dt) for wd, dt in zip(widths, dtypes)],
        compiler_params=_cparams("arbitrary", "arbitrary"),
        name="norm_mod_project",
    )(x, norm_w.reshape(1, d), mod, w_bf16, *[hn.reshape(1, -1) for hn in head_norms])


def _swa_kernel(sink_ref, q_ref, kc_ref, kp_ref, vc_ref, vp_ref, bias_ref, o_ref):
    n = pl.program_id(1)
    w = A_WINDOW
    hd = A_HEAD_DIM
    group = A_HEADS // A_KV_HEADS
    q = q_ref[0]
    k = jnp.concatenate([kp_ref[0], kc_ref[0]], axis=0)
    v = jnp.concatenate([vp_ref[0], vc_ref[0]], axis=0)
    v_t = v.astype(F32).T
    ones_rows = (lax.broadcasted_iota(jnp.int32, (SWA_VT_ROWS - hd, 2 * w), 0) == 0).astype(BF16)
    key = lax.broadcasted_iota(jnp.int32, (2 * w, group * w), 0)
    qry = lax.broadcasted_iota(jnp.int32, (2 * w, group * w), 1) % w
    dist = qry + w - key
    mask = (dist >= 0) & (dist < w) & ((n > 0) | (key >= w))
    head_of_lane = lax.broadcasted_iota(jnp.int32, (1, group * w), 1) // w
    logits, sinks, vts = [], [], []
    for kv in range(A_KV_HEADS):
        h0 = kv * group
        k_g = k[:, kv * hd:(kv + 1) * hd]
        vts.append(jnp.concatenate([v_t[kv * hd:(kv + 1) * hd, :].astype(BF16), ones_rows], axis=0))
        q_g = jnp.concatenate([q[:, (h0 + g) * hd:(h0 + g + 1) * hd] for g in range(group)], axis=0)
        bias = jnp.concatenate([bias_ref[h0 + g] for g in range(group)], axis=1)
        logits.append(jnp.where(mask, _dot_nt(k_g, q_g) + bias, NEG_BIG))
        sink = jnp.full((1, group * w), sink_ref[h0 + group - 1] * LOG2E, F32)
        for g in range(group - 2, -1, -1):
            sink = jnp.where(head_of_lane == g, sink_ref[h0 + g] * LOG2E, sink)
        sinks.append(sink)
    maxima = [jnp.maximum(jnp.max(lg, axis=0, keepdims=True), sink) for lg, sink in zip(logits, sinks)]
    accs = [jnp.dot(vt_g, jnp.exp2(lg - m).astype(BF16), preferred_element_type=F32)
            for vt_g, lg, m in zip(vts, logits, maxima)]
    for kv in range(A_KV_HEADS):
        h0 = kv * group
        acc = accs[kv]
        o_t = acc[0:hd] / (acc[hd:hd + 1] + jnp.exp2(sinks[kv] - maxima[kv]))
        for pair in range(group // 2):
            two = jnp.concatenate([o_t[:, (2 * pair) * w:(2 * pair + 1) * w],
                                   o_t[:, (2 * pair + 1) * w:(2 * pair + 2) * w]], axis=0)
            c0 = (h0 + 2 * pair) * hd
            o_ref[0, :, c0:c0 + 2 * hd] = two.T.astype(o_ref.dtype)


def _swa_attention(qa, ka, va, sinks, band_bias):
    bsz, s, _ = qa.shape
    w = A_WINDOW
    nb = s // w
    kvw = A_KV_HEADS * A_HEAD_DIM
    cur = lambda b, n: (b, n, 0)
    prev = lambda b, n: (b, jnp.maximum(n - 1, 0), 0)
    return pl.pallas_call(
        _swa_kernel,
        grid=(bsz, nb),
        in_specs=[pl.BlockSpec(memory_space=pltpu.SMEM),
                  pl.BlockSpec((1, w, A_HEADS * A_HEAD_DIM), cur),
                  pl.BlockSpec((1, w, kvw), cur),
                  pl.BlockSpec((1, w, kvw), prev),
                  pl.BlockSpec((1, w, kvw), cur),
                  pl.BlockSpec((1, w, kvw), prev),
                  pl.BlockSpec((REL_HEADS, 2 * w, w), lambda b, n: (0, 0, 0))],
        out_specs=pl.BlockSpec((1, w, A_HEADS * A_HEAD_DIM), cur),
        out_shape=jax.ShapeDtypeStruct((bsz, s, A_HEADS * A_HEAD_DIM), BF16),
        compiler_params=_cparams("arbitrary", "arbitrary"),
        name="swa_sink_attention",
    )(sinks, qa, ka, ka, va, va, band_bias)


def _gla_kernel(qk_ref, v_ref, r_ref, ab_ref, gu_ref, gb_ref, on_ref, o_ref, state_ref):
    s_idx = pl.program_id(1)
    c_len = B_CHUNK
    dk, dv = B_KEY_DIM, B_VAL_DIM
    hk = B_HEADS * dk

    @pl.when(s_idx == 0)
    def _():
        state_ref[...] = jnp.zeros(state_ref.shape, F32)

    z = _dot(ab_ref[0], gu_ref[...]) + gb_ref[...]
    log_a = (jnp.minimum(z, 0.0) - jnp.log(1.0 + jnp.exp(-jnp.abs(z)))) / B_GATE_TAU
    rows = qk_ref.shape[1]
    chunks = range(rows // c_len)
    heads = range(B_HEADS)
    ri = lax.broadcasted_iota(jnp.int32, (c_len, c_len), 0)
    ci = lax.broadcasted_iota(jnp.int32, (c_len, c_len), 1)
    tril = ri >= ci
    tri = tril.astype(BF16)
    g_hi = log_a.astype(BF16)
    g_lo = (log_a - g_hi.astype(F32)).astype(BF16)
    b_c = [jnp.dot(tri, g_hi[c * c_len:(c + 1) * c_len], preferred_element_type=F32)
           + jnp.dot(tri, g_lo[c * c_len:(c + 1) * c_len], preferred_element_type=F32) for c in chunks]
    b = jnp.concatenate(b_c, axis=0)
    last_c = [bc[c_len - 1:c_len, :] for bc in b_c]
    b_last = jnp.concatenate([jnp.broadcast_to(l, (c_len, hk)) for l in last_c], axis=0)
    q = qk_ref[0, :, 0:hk] * (dk ** -0.5)
    k = qk_ref[0, :, hk:2 * hk]
    q_dec = (q * jnp.exp(b)).astype(BF16)
    k_dec = (k * jnp.exp(-b)).astype(BF16)
    k_upd = (k * jnp.exp(b_last - b)).astype(BF16)
    v = v_ref[0]

    def rows_of(x, c, cols):
        return x[c * c_len:(c + 1) * c_len, cols]

    att = [[jnp.where(tril, _dot_nt(rows_of(q_dec, c, slice(h * dk, (h + 1) * dk)),
                                    rows_of(k_dec, c, slice(h * dk, (h + 1) * dk))), 0.0).astype(BF16)
            for h in heads] for c in chunks]
    upd = [[_dot_tn(rows_of(v, c, slice(h * dv, (h + 1) * dv)), rows_of(k_upd, c, slice(h * dk, (h + 1) * dk)))
            for h in heads] for c in chunks]
    state_in = []
    st = [state_ref[h] for h in heads]
    for c in chunks:
        state_in.append(st)
        decay = jnp.exp(last_c[c])
        st = [st[h] * decay[:, h * dk:(h + 1) * dk] + upd[c][h] for h in heads]
    for h in heads:
        state_ref[h] = st[h]
    for c in chunks:
        outs = []
        for h in heads:
            o_h = (_dot(att[c][h], rows_of(v, c, slice(h * dv, (h + 1) * dv)))
                   + _dot_nt(rows_of(q_dec, c, slice(h * dk, (h + 1) * dk)), state_in[c][h]))
            o_h = _rms_rows(o_h, on_ref[...])
            r_h = r_ref[0, c * c_len:(c + 1) * c_len, h * dv:(h + 1) * dv]
            outs.append(o_h * (r_h * jax.nn.sigmoid(r_h)))
        o_ref[0, c * c_len:(c + 1) * c_len, :] = jnp.concatenate(outs, axis=-1).astype(o_ref.dtype)


def _gla(qk, vb, rb, ab, gate_up_pad, gate_bias, out_norm):
    bsz, s, _ = qk.shape
    tm = min(GLA_ROWS, s)
    hv = B_HEADS * B_VAL_DIM
    hk = B_HEADS * B_KEY_DIM
    blk = lambda b, i: (b, i, 0)
    const = lambda b, i: (0, 0)
    return pl.pallas_call(
        _gla_kernel,
        grid=(bsz, s // tm),
        in_specs=[pl.BlockSpec((1, tm, 2 * hk), blk),
                  pl.BlockSpec((1, tm, hv), blk),
                  pl.BlockSpec((1, tm, hv), blk),
                  pl.BlockSpec((1, tm, LANES), blk),
                  pl.BlockSpec((LANES, hk), const),
                  pl.BlockSpec((1, hk), const),
                  pl.BlockSpec((1, B_VAL_DIM), const)],
        out_specs=pl.BlockSpec((1, tm, hv), blk),
        out_shape=jax.ShapeDtypeStruct((bsz, s, hv), BF16),
        scratch_shapes=[pltpu.VMEM((B_HEADS, B_VAL_DIM, B_KEY_DIM), F32)],
        compiler_params=_cparams("arbitrary", "arbitrary"),
        name="gated_linear_attention",
    )(qk, vb, rb, ab, gate_up_pad, gate_bias.reshape(1, hk), out_norm.reshape(1, -1))


def _moba_kernel(q_ref, k_ref, v_ref, bias_ref, o_ref, vt_ref, kmean_ref, pick_ref):
    i = pl.program_id(2)
    blk = C_BLOCK
    hd = C_HEAD_DIM
    nb = k_ref.shape[1] // blk
    grp = MOBA_GROUP
    heads = range(MOBA_HEADS_PER_STEP)

    def lanes(hh):
        return slice(hh * hd, (hh + 1) * hd)

    @pl.when(i == 0)
    def _():
        ones_rows = (lax.broadcasted_iota(jnp.int32, (MOBA_VT_ROWS - hd, blk), 0) == 0).astype(BF16)
        for hh in heads:
            kn = k_ref[0, :, lanes(hh)].astype(F32)
            kmean_ref[hh] = jnp.mean(kn.reshape(nb, blk, hd), axis=1)
            for j in range(nb):
                vt = v_ref[0, j * blk:(j + 1) * blk, lanes(hh)].astype(F32).T.astype(BF16)
                vt_ref[hh, j] = jnp.concatenate([vt, ones_rows], axis=0)

    def logits(hh, q_h, j):
        j0 = pl.multiple_of(j * blk, blk)
        return _dot_nt(k_ref[0, pl.ds(j0, blk), lanes(hh)], q_h)

    def col_max(lg, picked):
        return jnp.where(picked, jnp.max(lg, axis=0, keepdims=True), NEG_BIG)

    def weighted_v(hh, j, lg, m, picked):
        p = jnp.exp2(lg - m).astype(BF16)
        return jnp.where(picked, jnp.dot(vt_ref[hh, j], p, preferred_element_type=F32), 0.0)

    def softmax_step(ms, accs, tiles):
        m_new = []
        for hh in heads:
            m_h = ms[hh]
            for _, lg, picked in tiles[hh]:
                m_h = jnp.maximum(m_h, col_max(lg, picked))
            m_new.append(m_h)
        out = []
        for hh in heads:
            acc = accs[hh] * jnp.exp2(ms[hh] - m_new[hh])
            for j, lg, picked in tiles[hh]:
                acc = acc + weighted_v(hh, j, lg, m_new[hh], picked)
            out.append(acc)
        return tuple(m_new), tuple(out)

    jrow = lax.broadcasted_iota(jnp.int32, (nb, blk), 0)
    key = lax.broadcasted_iota(jnp.int32, (blk, blk), 0)
    qry = lax.broadcasted_iota(jnp.int32, (blk, blk), 1)
    j_prev = jnp.maximum(i - 1, 0)
    qs = []
    for hh in heads:
        q_h = q_ref[0, :, lanes(hh)]
        qs.append(q_h)
        score = jnp.where(jrow < i, _dot_nt(kmean_ref[hh], q_h), -jnp.inf)
        beats = jnp.zeros((nb, blk), jnp.int32)
        for j2 in range(nb):
            s2 = score[j2:j2 + 1, :]
            beats = beats + ((s2 > score) | ((s2 == score) & (j2 < jrow))).astype(jnp.int32)
        pick_ref[hh] = ((beats < C_TOPK) & (jrow < i)).astype(F32)

    first = []
    for hh in heads:
        lg_own = jnp.where(key <= qry, logits(hh, qs[hh], i) + bias_ref[hh, 0:blk, :], NEG_BIG)
        lg_prev = logits(hh, qs[hh], j_prev) + bias_ref[hh, blk:2 * blk, :]
        pick_prev = pick_ref[hh, pl.ds(j_prev, 1), :] > 0.5
        first.append([(i, lg_own, True), (j_prev, lg_prev, pick_prev)])
    start = (tuple(jnp.full((1, blk), NEG_BIG, F32) for _ in heads),
             tuple(jnp.zeros((MOBA_VT_ROWS, blk), F32) for _ in heads))
    carry0 = softmax_step(start[0], start[1], first)

    n_far = jnp.maximum(i - 1, 0)

    def body(g, carry):
        tiles = []
        for hh in heads:
            row = []
            for u in range(grp):
                j = g * grp + u
                jc = jnp.minimum(j, nb - 1)
                picked = (pick_ref[hh, pl.ds(jc, 1), :] > 0.5) & (j < n_far)
                row.append((jc, logits(hh, qs[hh], jc), picked))
            tiles.append(row)
        return softmax_step(carry[0], carry[1], tiles)

    _, accs = lax.fori_loop(0, (n_far + grp - 1) // grp, body, carry0)
    for hh in heads:
        o_ref[0, :, lanes(hh)] = (accs[hh][0:hd] / accs[hh][hd:hd + 1]).T.astype(o_ref.dtype)


def _moba(q, k, v, bias_tiles):
    bsz, s, _ = q.shape
    blk, hd = C_BLOCK, C_HEAD_DIM
    assert s % blk == 0
    nb = s // blk
    hb = MOBA_HEADS_PER_STEP
    assert C_HEADS % hb == 0
    return pl.pallas_call(
        _moba_kernel,
        grid=(bsz, C_HEADS // hb, nb),
        in_specs=[pl.BlockSpec((1, blk, hb * hd), lambda b, h, i: (b, i, h)),
                  pl.BlockSpec((1, s, hb * hd), lambda b, h, i: (b, 0, h)),
                  pl.BlockSpec((1, s, hb * hd), lambda b, h, i: (b, 0, h)),
                  pl.BlockSpec((hb, 2 * blk, blk), lambda b, h, i: (h, 0, 0))],
        out_specs=pl.BlockSpec((1, blk, hb * hd), lambda b, h, i: (b, i, h)),
        out_shape=jax.ShapeDtypeStruct((bsz, s, C_HEADS * hd), BF16),
        scratch_shapes=[pltpu.VMEM((hb, nb, MOBA_VT_ROWS, blk), BF16),
                        pltpu.VMEM((hb, nb, hd), F32), pltpu.VMEM((hb, nb, blk), F32)],
        compiler_params=_cparams("arbitrary", "arbitrary", "arbitrary"),
        name="moba_attention",
    )(q, k, v, bias_tiles)


def _out_route_kernel(*refs, n_in):
    a_refs = refs[:n_in]
    w_refs = refs[n_in:2 * n_in]
    (x_ref, mod_ref, nw_ref, wr_ref, br_ref,
     xo_ref, h_ref, ids_ref, gate_ref, rank_ref, cnt_ref, base_ref) = refs[2 * n_in:]
    first = (pl.program_id(0) == 0) & (pl.program_id(1) == 0)

    @pl.when(first)
    def _():
        base_ref[...] = jnp.zeros(base_ref.shape, F32)

    y = _dot(a_refs[0][0], w_refs[0][...])
    for a_ref, w_ref in zip(a_refs[1:], w_refs[1:]):
        y = y + _dot(a_ref[0], w_ref[...])
    x = x_ref[0] + mod_ref[0, 2:3, :] * y
    xo_ref[0] = x

    tm = x.shape[0]
    h = _rms_rows(x, nw_ref[...])
    h = h * (1.0 + mod_ref[0, 4:5, :]) + mod_ref[0, 3:4, :]
    _to_row_tiles(h_ref, h)
    lt = _dot_nt(wr_ref[...], h) + br_ref[...]
    g = [lt[r:r + 1, :] for r in range(MOE_GROUPS)]
    gmax = functools.reduce(jnp.maximum, g)
    gsel = jnp.full(gmax.shape, MOE_GROUPS - 1, jnp.int32)
    for r in range(MOE_GROUPS - 2, -1, -1):
        gsel = jnp.where(g[r] == gmax, r, gsel)
    p_g = 1.0 / functools.reduce(jnp.add, [jnp.exp(gr - gmax) for gr in g])
    epg = MOE_EXPERTS_PER_GROUP
    e_in = lt[8 + (MOE_GROUPS - 1) * epg:8 + MOE_GROUPS * epg, :]
    for r in range(MOE_GROUPS - 2, -1, -1):
        e_in = jnp.where(gsel == r, lt[8 + r * epg:8 + (r + 1) * epg, :], e_in)
    sub = lax.broadcasted_iota(jnp.int32, (epg, tm), 0)
    v1 = jnp.max(e_in, axis=0, keepdims=True)
    i1 = jnp.min(jnp.where(e_in == v1, sub, epg), axis=0, keepdims=True)
    e2 = jnp.where(sub == i1, -jnp.inf, e_in)
    v2 = jnp.max(e2, axis=0, keepdims=True)
    i2 = jnp.min(jnp.where(e2 == v2, sub, epg), axis=0, keepdims=True)
    t = jnp.exp(v2 - v1)
    w1 = p_g / (1.0 + t)
    w2 = p_g * t / (1.0 + t)
    id1 = gsel * epg + i1
    id2 = gsel * epg + i2
    ids_ref[...] = jnp.concatenate([id1, id2], axis=0)
    gate_ref[...] = jnp.concatenate([w1, w2, jnp.zeros((LANES - 2, tm), F32)], axis=0).T

    eidx = lax.broadcasted_iota(jnp.int32, (MOE_EXPERTS, tm), 0)
    oh1 = eidx == id1
    oh2 = eidx == id2
    onehot = (oh1 | oh2).astype(BF16)
    tr = lax.broadcasted_iota(jnp.int32, (tm, tm), 0)
    tc = lax.broadcasted_iota(jnp.int32, (tm, tm), 1)
    before = (tr < tc).astype(BF16)
    prefix = jnp.dot(onehot, before, preferred_element_type=F32) + base_ref[...]
    r1 = jnp.sum(jnp.where(oh1, prefix, 0.0), axis=0, keepdims=True)
    r2 = jnp.sum(jnp.where(oh2, prefix, 0.0), axis=0, keepdims=True)
    rank_ref[...] = jnp.concatenate([r1, r2], axis=0).astype(jnp.int32)
    base_ref[...] = base_ref[...] + jnp.sum(onehot.astype(F32), axis=1, keepdims=True)
    cnt_ref[...] = base_ref[...].astype(jnp.int32)


def _out_project_route(acts, weights, x, mod, norm_w, wr_t, br):
    bsz, s, d = x.shape
    tm = min(ROUTER_ROWS, s)
    n = bsz * s
    nt = s // tm
    n_in = len(acts)
    tok = lambda b, i: (0, b * nt + i)
    const = lambda b, i: (0, 0)
    in_specs = [pl.BlockSpec((1, tm, a.shape[2]), lambda b, i: (b, i, 0)) for a in acts]
    in_specs += [pl.BlockSpec(w.shape, const) for w in weights]
    in_specs += [pl.BlockSpec((1, tm, d), lambda b, i: (b, i, 0)),
                 pl.BlockSpec((1, 6, d), lambda b, i: (b, 0, 0)),
                 pl.BlockSpec((1, d), const),
                 pl.BlockSpec((ROUTER_LOGIT_ROWS, d), const),
                 pl.BlockSpec((ROUTER_LOGIT_ROWS, 1), const)]
    return pl.pallas_call(
        functools.partial(_out_route_kernel, n_in=n_in),
        grid=(bsz, nt),
        in_specs=in_specs,
        out_specs=[pl.BlockSpec((1, tm, d), lambda b, i: (b, i, 0)),
                   pl.BlockSpec((tm * d // LANES, LANES), lambda b, i: (b * nt + i, 0)),
                   pl.BlockSpec((MOE_TOPK, tm), tok),
                   pl.BlockSpec((tm, LANES), lambda b, i: (b * nt + i, 0)),
                   pl.BlockSpec((MOE_TOPK, tm), tok),
                   pl.BlockSpec((MOE_EXPERTS, 1), const)],
        out_shape=[jax.ShapeDtypeStruct((bsz, s, d), F32),
                   jax.ShapeDtypeStruct((n * d // LANES, LANES), F32),
                   jax.ShapeDtypeStruct((MOE_TOPK, n), jnp.int32),
                   jax.ShapeDtypeStruct((n, LANES), F32),
                   jax.ShapeDtypeStruct((MOE_TOPK, n), jnp.int32),
                   jax.ShapeDtypeStruct((MOE_EXPERTS, 1), jnp.int32)],
        scratch_shapes=[pltpu.VMEM((MOE_EXPERTS, 1), F32)],
        compiler_params=_cparams("arbitrary", "arbitrary"),
        name="out_project_route",
    )(*acts, *weights, x, mod, norm_w.reshape(1, d), wr_t, br)


def _dest_kernel(ids_ref, rank_ref, start_ref, o_ref):
    ids = ids_ref[...]
    tm = ids.shape[1]
    eidx = lax.broadcasted_iota(jnp.int32, (MOE_EXPERTS, tm), 0)
    rows = []
    for k in range(MOE_TOPK):
        base = jnp.sum(jnp.where(eidx == ids[k:k + 1, :], start_ref[...], 0), axis=0, keepdims=True)
        rows.append(base + rank_ref[k:k + 1, :])
    o_ref[...] = jnp.concatenate(rows, axis=0)


def _dest_rows(ids, rank, start):
    n = ids.shape[1]
    tm = min(2048, n)
    return pl.pallas_call(
        _dest_kernel,
        grid=(n // tm,),
        in_specs=[pl.BlockSpec((MOE_TOPK, tm), lambda i: (0, i)),
                  pl.BlockSpec((MOE_TOPK, tm), lambda i: (0, i)),
                  pl.BlockSpec((MOE_EXPERTS, 1), lambda i: (0, 0))],
        out_specs=pl.BlockSpec((MOE_TOPK, tm), lambda i: (0, i)),
        out_shape=jax.ShapeDtypeStruct((MOE_TOPK, n), jnp.int32),
        compiler_params=_cparams("arbitrary"),
        name="moe_dest_rows",
    )(ids, rank, start)


def _row_copy(src, dst, sem, src_row, dst_row, rt):
    return pltpu.make_async_copy(src.at[pl.ds(pl.multiple_of(src_row * rt, rt), rt)],
                                 dst.at[pl.ds(pl.multiple_of(dst_row * rt, rt), rt)], sem)


def _dispatch_kernel(pend_ref, padded_ref, dest_ref, h_ref, xs_ref, zero_ref, sem, *, rt):
    tm = h_ref.shape[0] // rt
    zrows = zero_ref.shape[0] // rt

    @pl.when(pl.program_id(0) == 0)
    def _():
        zero_ref[...] = jnp.zeros(zero_ref.shape, F32)
        for e in range(MOE_EXPERTS):
            @pl.when(padded_ref[e] > 0)
            def _():
                row0 = pl.multiple_of((pend_ref[e] - zrows) * rt, zrows * rt)
                cp = pltpu.make_async_copy(zero_ref, xs_ref.at[pl.ds(row0, zrows * rt)], sem)
                cp.start()
                cp.wait()

        def zero_tail(b, _):
            row0 = pl.multiple_of(b * zrows * rt, zrows * rt)
            cp = pltpu.make_async_copy(zero_ref, xs_ref.at[pl.ds(row0, zrows * rt)], sem)
            cp.start()
            cp.wait()
            return 0

        lax.fori_loop(pend_ref[MOE_EXPERTS - 1] // zrows, xs_ref.shape[0] // (zrows * rt), zero_tail, 0)

    def start(t, _):
        for k in range(MOE_TOPK):
            _row_copy(h_ref, xs_ref, sem, t, dest_ref[k, t], rt).start(priority=k)
        return 0

    lax.fori_loop(0, tm, start, 0, unroll=DMA_ISSUE_UNROLL)
    for k in range(MOE_TOPK):
        pltpu.make_async_copy(h_ref, xs_ref.at[pl.ds(0, tm * rt)], sem).wait()


def _dispatch(h_tiles, dest, pend, padded, n_rows, rt):
    n = h_tiles.shape[0] // rt
    tm = min(DISPATCH_ROWS, n)
    return pl.pallas_call(
        functools.partial(_dispatch_kernel, rt=rt),
        grid=(n // tm,),
        in_specs=[pl.BlockSpec(memory_space=pltpu.SMEM),
                  pl.BlockSpec(memory_space=pltpu.SMEM),
                  pl.BlockSpec((MOE_TOPK, tm), lambda i: (0, i), memory_space=pltpu.SMEM),
                  pl.BlockSpec((tm * rt, LANES), lambda i: (i, 0))],
        out_specs=pl.BlockSpec(memory_space=pl.ANY),
        out_shape=jax.ShapeDtypeStruct((n_rows * rt, LANES), F32),
        scratch_shapes=[pltpu.VMEM((MOE_BLOCK_ROWS * rt, LANES), F32), pltpu.SemaphoreType.DMA(())],
        compiler_params=_cparams("arbitrary"),
        name="moe_dispatch",
    )(pend, padded, dest, h_tiles)


def _expert_kernel(blk_e_ref, n_used_ref, x_ref, w1_ref, w3_ref, w2_ref, o_ref, w1b_ref, w3b_ref, w2b_ref):
    i = pl.program_id(0)
    used = i < n_used_ref[0]
    new_expert = (i == 0) | (blk_e_ref[i] != blk_e_ref[jnp.maximum(i - 1, 0)])

    @pl.when(used & new_expert)
    def _():
        w1b_ref[...] = w1_ref[0, 0].astype(BF16)
        w3b_ref[...] = w3_ref[0, 0].astype(BF16)
        w2b_ref[...] = w2_ref[0, 0].astype(BF16)

    @pl.when(used)
    def _():
        d = w1b_ref.shape[0]
        rows = x_ref.shape[0] * LANES // d
        x = _from_row_tiles(x_ref, rows, d).astype(BF16)
        a = jnp.dot(x, w1b_ref[...], preferred_element_type=F32)
        g = jnp.dot(x, w3b_ref[...], preferred_element_type=F32)
        y = jnp.dot((a * jax.nn.sigmoid(a) * g).astype(BF16), w2b_ref[...], preferred_element_type=F32)
        _to_row_tiles(o_ref, y)

    @pl.when(jnp.logical_not(used))
    def _():
        o_ref[...] = jnp.zeros(o_ref.shape, F32)


def _expert_ffn(xs, blk_e, n_used, w1, w3, w2, layer):
    d, hid = w1.shape[2], w1.shape[3]
    rt = d // LANES
    n_rows = xs.shape[0] // rt
    tm = MOE_BLOCK_ROWS
    row = lambda i, be, nu: (jnp.minimum(i, nu[0] - 1), 0)
    wsel = lambda i, be, nu: (layer, be[i], 0, 0)
    return pl.pallas_call(
        _expert_kernel,
        grid_spec=pltpu.PrefetchScalarGridSpec(
            num_scalar_prefetch=2,
            grid=(n_rows // tm,),
            in_specs=[pl.BlockSpec((tm * rt, LANES), row),
                      pl.BlockSpec((1, 1, d, hid), wsel),
                      pl.BlockSpec((1, 1, d, hid), wsel),
                      pl.BlockSpec((1, 1, hid, d), wsel)],
            out_specs=pl.BlockSpec((tm * rt, LANES), lambda i, be, nu: (i, 0)),
            scratch_shapes=[pltpu.VMEM((d, hid), BF16), pltpu.VMEM((d, hid), BF16),
                            pltpu.VMEM((hid, d), BF16)]),
        out_shape=jax.ShapeDtypeStruct((n_rows * rt, LANES), F32),
        compiler_params=_cparams("arbitrary"),
        name="moe_expert_ffn",
    )(blk_e, n_used, xs, w1, w3, w2)


def _combine_kernel(dest_ref, next_dest_ref, ys_ref, gate_ref, x_ref, mod_ref, *refs, outs):
    buf_ref, sem = refs[-2:]
    o_ref = refs[-3 - len(outs)]
    tm, d = x_ref.shape[1], x_ref.shape[2]
    rt = d // LANES
    step = pl.program_id(0) * pl.num_programs(1) + pl.program_id(1)
    n_steps = pl.num_programs(0) * pl.num_programs(1)
    slot = step % 2

    def gather(rows_ref, s):
        def start(t, _):
            for k in range(MOE_TOPK):
                _row_copy(ys_ref, buf_ref.at[s, k], sem.at[s], rows_ref[k, t], t, rt).start(priority=k)
            return 0
        lax.fori_loop(0, tm, start, 0, unroll=DMA_ISSUE_UNROLL)

    @pl.when(step == 0)
    def _():
        gather(dest_ref, 0)

    @pl.when(step + 1 < n_steps)
    def _():
        gather(next_dest_ref, 1 - slot)

    for k in range(MOE_TOPK):
        pltpu.make_async_copy(ys_ref.at[pl.ds(0, tm * rt)], buf_ref.at[slot, k], sem.at[slot]).wait()
    g = gate_ref[...]
    y = (g[:, 0:1] * _from_row_tiles(buf_ref, tm, d, (slot, 0))
         + g[:, 1:2] * _from_row_tiles(buf_ref, tm, d, (slot, 1)))
    x = x_ref[0] + mod_ref[0, 5:6, :] * y
    o_ref[0] = x
    if outs:
        nw_ref, next_mod_ref, w_ref = refs[0:3]
        n_norm = len(refs) - 6 - len(outs)
        _project_rows(x, nw_ref, next_mod_ref, w_ref, refs[3:3 + n_norm], refs[-2 - len(outs):-2], outs)


def _combine(ys, dest, gates, x, mod, proj=None):
    bsz, s, d = x.shape
    tm = min(COMBINE_ROWS, s)
    nt = s // tm
    last = bsz * nt - 1
    tile = lambda b, i: (b, i, 0)
    const = lambda b, i: (0, 0)
    in_specs = [pl.BlockSpec((MOE_TOPK, tm), lambda b, i: (0, b * nt + i), memory_space=pltpu.SMEM),
                pl.BlockSpec((MOE_TOPK, tm), lambda b, i: (0, jnp.minimum(b * nt + i + 1, last)),
                             memory_space=pltpu.SMEM),
                pl.BlockSpec(memory_space=pl.ANY),
                pl.BlockSpec((tm, LANES), lambda b, i: (b * nt + i, 0)),
                pl.BlockSpec((1, tm, d), tile),
                pl.BlockSpec((1, 6, d), lambda b, i: (b, 0, 0))]
    args = [dest, dest, ys, gates, x, mod]
    out_specs = [pl.BlockSpec((1, tm, d), tile)]
    out_shape = [jax.ShapeDtypeStruct((bsz, s, d), F32)]
    outs = ()
    if proj is not None:
        norm_w, next_mod, w_bf16, outs, dtypes, head_norms = proj
        in_specs += [pl.BlockSpec((1, d), const), pl.BlockSpec((1, 6, d), lambda b, i: (b, 0, 0)),
                     pl.BlockSpec(w_bf16.shape, const)]
        in_specs += [pl.BlockSpec((1, hn.shape[0]), const) for hn in head_norms]
        args += [norm_w.reshape(1, d), next_mod, w_bf16] + [hn.reshape(1, -1) for hn in head_norms]
        out_specs += [pl.BlockSpec((1, tm, o[0]), tile) for o in outs]
        out_shape += [jax.ShapeDtypeStruct((bsz, s, o[0]), dt) for o, dt in zip(outs, dtypes)]
    res = pl.pallas_call(
        functools.partial(_combine_kernel, outs=tuple(outs)),
        grid=(bsz, nt),
        in_specs=in_specs,
        out_specs=out_specs,
        out_shape=out_shape,
        scratch_shapes=[pltpu.VMEM((2, MOE_TOPK, tm * d // LANES, LANES), F32), pltpu.SemaphoreType.DMA((2,))],
        compiler_params=_cparams("arbitrary", "arbitrary"),
        name="moe_combine" if proj is None else "moe_combine_project",
    )(*args)
    return res[0] if proj is None else res


def _mixer_out_moe_residual(acts, w_outs, x, norm_w, mod, w_group, b_group, w_expert, b_expert, w1, w3, w2, layer,
                            next_proj):
    bsz, s, d = x.shape
    n = bsz * s
    tm = MOE_BLOCK_ROWS
    wr_t = jnp.zeros((ROUTER_LOGIT_ROWS, d), F32).at[0:MOE_GROUPS].set(w_group.T).at[8:].set(w_expert.T)
    br = jnp.zeros((ROUTER_LOGIT_ROWS, 1), F32).at[0:MOE_GROUPS, 0].set(b_group).at[8:, 0].set(b_expert)
    x_mid, h2, ids, gates, rank, counts = _out_project_route(acts, w_outs, x, mod, norm_w, wr_t.astype(BF16), br)
    counts = counts[:, 0]
    padded = ((counts + tm - 1) // tm) * tm
    pend = jnp.cumsum(padded).astype(jnp.int32)
    start = pend - padded
    n_rows = ((n * MOE_TOPK + tm - 1) // tm) * tm + MOE_EXPERTS * tm
    blk_start = jnp.arange(n_rows // tm, dtype=jnp.int32) * tm
    blk_e = jnp.minimum(jnp.sum((pend[None, :] <= blk_start[:, None]).astype(jnp.int32), axis=1),
                        MOE_EXPERTS - 1)
    n_used = pend[-1:] // tm
    dest = _dest_rows(ids, rank, start.reshape(MOE_EXPERTS, 1))
    xs = _dispatch(h2, dest, pend, padded, n_rows, d // LANES)
    ys = _expert_ffn(xs, blk_e, n_used, w1, w3, w2, layer)
    return _combine(ys, dest, gates, x_mid, mod, next_proj)


def _even_proj_spec(w_in, q_norm, k_norm):
    d, n_in = w_in.shape
    aq = A_HEADS * A_HEAD_DIM
    akv = A_KV_HEADS * A_HEAD_DIM
    bk = B_HEADS * B_KEY_DIM
    bv = B_HEADS * B_VAL_DIM
    w_pad = jnp.zeros((d, n_in - B_GATE_RANK + LANES), F32).at[:, :n_in].set(w_in).astype(BF16)
    outs = ((aq, A_HEAD_DIM, 0, A_HEAD_DIM ** -0.5 * LOG2E), (akv, A_HEAD_DIM, 1, 1.0), (akv, 0, 0, 1.0),
            (2 * bk, 0, 0, 1.0), (bv, 0, 0, 1.0), (bv, 0, 0, 1.0), (LANES, 0, 0, 1.0))
    dtypes = (BF16, BF16, BF16, F32, BF16, F32, F32)
    return w_pad, outs, dtypes, (q_norm, k_norm)


def _odd_proj_spec(w_in, q_norm, k_norm):
    mix = C_HEADS * C_HEAD_DIM
    outs = ((mix, C_HEAD_DIM, 0, C_HEAD_DIM ** -0.5 * LOG2E), (mix, C_HEAD_DIM, 1, 1.0), (mix, 0, 0, 1.0))
    return w_in.astype(BF16), outs, (BF16, BF16, BF16), (q_norm, k_norm)


def _even_mixer(groups, w_out, sinks, gate_up, gate_bias, out_norm, band_bias):
    qa, ka, va, qk, vb, rb, ab = groups
    aq = A_HEADS * A_HEAD_DIM
    bk = B_HEADS * B_KEY_DIM
    oa = _swa_attention(qa, ka, va, sinks, band_bias)
    gu_pad = jnp.zeros((LANES, bk), F32).at[:B_GATE_RANK].set(gate_up).astype(BF16)
    ob = _gla(qk, vb, rb, ab, gu_pad, gate_bias, out_norm)
    w_out = w_out.astype(BF16)
    return [oa, ob], [w_out[:aq], w_out[aq:]]


def _odd_mixer(groups, w_out, bias_tiles):
    q, k, v = groups
    return [_moba(q, k, v, bias_tiles)], [w_out.astype(BF16)]


def kernel(x, c, rel_bias, ada_w, ada_b, norm1_w, norm2_w, even_w_in, even_w_out, a_q_norm, a_k_norm, a_sinks, b_gate_up, b_gate_bias, b_out_norm, odd_w_in, odd_w_out, c_q_norm, c_k_norm, moe_w_group, moe_b_group, moe_w_expert, moe_b_expert, moe_w1, moe_w3, moe_w2):
    depth = ada_w.shape[0]
    bsz, _, d = x.shape
    moba_bias, band_bias = _bias_tiles(rel_bias)
    mod_all = _adaln(c, ada_w, ada_b).reshape(depth, bsz, 6, d)

    def proj_spec(layer):
        j = layer // 2
        if layer % 2 == 0:
            return _even_proj_spec(even_w_in[j], a_q_norm[j], a_k_norm[j])
        return _odd_proj_spec(odd_w_in[j], c_q_norm[j], c_k_norm[j])

    w_in, outs, dtypes, head_norms = proj_spec(0)
    groups = _norm_mod_project(x, norm1_w[0], mod_all[0], w_in, outs, dtypes, head_norms)
    for layer in range(depth):
        mod = mod_all[layer]
        j = layer // 2
        if layer % 2 == 0:
            acts, w_outs = _even_mixer(groups, even_w_out[j], a_sinks[j], b_gate_up[j], b_gate_bias[j],
                                       b_out_norm[j], band_bias)
        else:
            acts, w_outs = _odd_mixer(groups, odd_w_out[j], moba_bias)
        next_proj = None
        if layer + 1 < depth:
            w_in, outs, dtypes, head_norms = proj_spec(layer + 1)
            next_proj = (norm1_w[layer + 1], mod_all[layer + 1], w_in, outs, dtypes, head_norms)
        res = _mixer_out_moe_residual(acts, w_outs, x, norm2_w[layer], mod, moe_w_group[layer],
                                      moe_b_group[layer], moe_w_expert[layer], moe_b_expert[layer],
                                      moe_w1, moe_w3, moe_w2, layer, next_proj)
        x, groups = (res, None) if next_proj is None else (res[0], res[1:])
    return x
```

```python
import functools
import math

import jax
import jax.numpy as jnp
import numpy as np
from jax import lax
from jax.experimental import pallas as pl
from jax.experimental.pallas import tpu as pltpu

RMS_EPS = 1e-6
A_HEADS, A_KV_HEADS, A_HEAD_DIM, A_WINDOW = 8, 2, 64, 128
B_HEADS, B_KEY_DIM, B_VAL_DIM, B_GATE_RANK, B_GATE_TAU, B_CHUNK = 4, 64, 128, 16, 16.0, 64
C_HEADS, C_HEAD_DIM, C_BLOCK, C_TOPK = 8, 128, 256, 3
REL_BUCKETS, REL_MAX_DIST, REL_HEADS = 32, 128, 8
MOE_GROUPS, MOE_EXPERTS_PER_GROUP, MOE_TOPK = 4, 8, 2
MOE_EXPERTS = MOE_GROUPS * MOE_EXPERTS_PER_GROUP

LANES = 128
V7X_VMEM_LIMIT_BYTES = 56 * 1024 * 1024
NEG_BIG = -1e30
PROJ_ROWS = 512
GLA_ROWS = 512
ROUTER_ROWS = 512
MOE_BLOCK_ROWS = 512
DISPATCH_ROWS = 1024
COMBINE_ROWS = 512
DMA_ISSUE_UNROLL = 8
ROUTER_LOGIT_ROWS = 8 + MOE_EXPERTS
MOBA_GROUP = 4
MOBA_HEADS_PER_STEP = 4
LOG2E = math.log2(math.e)
SWA_VT_ROWS = A_HEAD_DIM + 16
MOBA_VT_ROWS = C_HEAD_DIM + 16

F32 = jnp.float32
BF16 = jnp.bfloat16


def _cparams(*sem):
    return pltpu.CompilerParams(dimension_semantics=sem, vmem_limit_bytes=V7X_VMEM_LIMIT_BYTES)


def _rel_bucket_np(dist):
    exact = REL_BUCKETS // 2
    d = np.maximum(dist, 0)
    logd = np.log(np.maximum(d, 1).astype(np.float64) / exact) / math.log(REL_MAX_DIST / exact)
    far = np.minimum(exact + (logd * (REL_BUCKETS - exact)).astype(np.int64), REL_BUCKETS - 1)
    return np.where(d < exact, d, far).astype(np.int32)


def _bucket_tiles():
    li = np.arange(C_BLOCK)
    own = _rel_bucket_np(li[None, :] - li[:, None])
    prev = _rel_bucket_np(li[None, :] - li[:, None] + C_BLOCK)
    band = _rel_bucket_np(np.arange(A_WINDOW)[None, :] + A_WINDOW - np.arange(2 * A_WINDOW)[:, None])
    return np.concatenate([own, prev], axis=0), band


def _rms_rows(x, w):
    return x * lax.rsqrt(jnp.mean(x * x, axis=-1, keepdims=True) + RMS_EPS) * w


def _dot(a, b):
    return jnp.dot(a.astype(BF16), b.astype(BF16), preferred_element_type=F32)


def _dot_nt(a, b):
    return lax.dot_general(a.astype(BF16), b.astype(BF16), (((1,), (1,)), ((), ())),
                           preferred_element_type=F32)


def _to_row_tiles(ref, x, lead=()):
    rows, d = x.shape
    chunks = d // LANES
    for c in range(chunks):
        ref[lead + (pl.ds(c, rows, stride=chunks), slice(None))] = x[:, c * LANES:(c + 1) * LANES]


def _from_row_tiles(ref, rows, d, lead=()):
    chunks = d // LANES
    return jnp.concatenate([ref[lead + (pl.ds(c, rows, stride=chunks), slice(None))] for c in range(chunks)],
                           axis=1)


def _dot_tn(a, b):
    return lax.dot_general(a.astype(BF16), b.astype(BF16), (((0,), (0,)), ((), ())),
                           preferred_element_type=F32)


def _bias_kernel(rb_ref, bkt_ref, o_ref, *, relative_to_last):
    h = pl.program_id(0)
    bkt = bkt_ref[...]
    acc = jnp.zeros(bkt.shape, F32)
    for b in range(REL_BUCKETS):
        acc = jnp.where(bkt == b, rb_ref[b, h], acc)
    if relative_to_last:
        acc = acc - rb_ref[REL_BUCKETS - 1, h]
    o_ref[0] = acc * LOG2E


def _bias_tile(rel_bias, bkt, relative_to_last):
    rows, cols = bkt.shape
    return pl.pallas_call(
        functools.partial(_bias_kernel, relative_to_last=relative_to_last),
        grid=(REL_HEADS,),
        in_specs=[pl.BlockSpec(memory_space=pltpu.SMEM),
                  pl.BlockSpec((rows, cols), lambda h: (0, 0))],
        out_specs=pl.BlockSpec((1, rows, cols), lambda h: (h, 0, 0)),
        out_shape=jax.ShapeDtypeStruct((REL_HEADS, rows, cols), F32),
        compiler_params=_cparams("arbitrary"),
        name="rel_bias_tiles",
    )(rel_bias, jnp.asarray(bkt))


def _bias_tiles(rel_bias):
    moba_bkt, band_bkt = _bucket_tiles()
    return _bias_tile(rel_bias, moba_bkt, True), _bias_tile(rel_bias, band_bkt, False)


def _adaln_kernel(c_ref, w_ref, b_ref, o_ref):
    c = c_ref[...]
    cond = c * jax.nn.sigmoid(c)
    o_ref[0] = _dot(cond, w_ref[0]) + b_ref[0]


def _adaln(c, ada_w, ada_b):
    depth, d, n6 = ada_w.shape
    bsz = c.shape[0]
    tn = 1536 if n6 % 1536 == 0 else n6
    return pl.pallas_call(
        _adaln_kernel,
        grid=(depth, n6 // tn),
        in_specs=[pl.BlockSpec((bsz, d), lambda l, j: (0, 0)),
                  pl.BlockSpec((1, d, tn), lambda l, j: (l, 0, j)),
                  pl.BlockSpec((1, 1, tn), lambda l, j: (l, 0, j))],
        out_specs=pl.BlockSpec((1, bsz, tn), lambda l, j: (l, 0, j)),
        out_shape=jax.ShapeDtypeStruct((depth, bsz, n6), F32),
        compiler_params=_cparams("arbitrary", "arbitrary"),
        name="adaln_mod",
    )(c, ada_w, ada_b.reshape(depth, 1, n6))


def _head_rms(y, w_ref, hd, post_scale):
    cols = y.shape[1]
    lane = lax.broadcasted_iota(jnp.int32, (1, LANES), 1)
    w = w_ref[...] * post_scale
    out = []
    for c in range(cols // LANES):
        t = y[:, c * LANES:(c + 1) * LANES]
        sq = t * t
        if hd == LANES:
            r = lax.rsqrt(jnp.sum(sq, axis=-1, keepdims=True) * (1.0 / hd) + RMS_EPS)
            out.append(t * r * w)
        else:
            low = lane < hd
            s_lo = jnp.sum(jnp.where(low, sq, 0.0), axis=-1, keepdims=True)
            s_hi = jnp.sum(jnp.where(low, 0.0, sq), axis=-1, keepdims=True)
            r = lax.rsqrt(jnp.where(low, s_lo, s_hi) * (1.0 / hd) + RMS_EPS)
            out.append(t * r * jnp.concatenate([w, w], axis=1))
    return jnp.concatenate(out, axis=1) if len(out) > 1 else out[0]


def _project_rows(x, nw_ref, mod_ref, w_ref, norm_refs, o_refs, outs):
    h = _rms_rows(x, nw_ref[...])
    h = h * (1.0 + mod_ref[0, 1:2, :]) + mod_ref[0, 0:1, :]
    y = _dot(h, w_ref[...])
    off = 0
    for o_ref, (wd, hd, widx, post) in zip(o_refs, outs):
        t = y[:, off:off + wd]
        if hd:
            t = _head_rms(t, norm_refs[widx], hd, post)
        o_ref[0] = t.astype(o_ref.dtype)
        off += wd


def _proj_kernel(x_ref, nw_ref, mod_ref, w_ref, *refs, outs):
    n_norm = len(refs) - len(outs)
    _project_rows(x_ref[0], nw_ref, mod_ref, w_ref, refs[:n_norm], refs[n_norm:], outs)


def _norm_mod_project(x, norm_w, mod, w_bf16, outs, dtypes, head_norms):
    bsz, s, d = x.shape
    tm = min(PROJ_ROWS, s)
    n = w_bf16.shape[1]
    widths = [o[0] for o in outs]
    assert sum(widths) == n and all(wd % LANES == 0 for wd in widths)
    return pl.pallas_call(
        functools.partial(_proj_kernel, outs=tuple(outs)),
        grid=(bsz, s // tm),
        in_specs=[pl.BlockSpec((1, tm, d), lambda b, i: (b, i, 0)),
                  pl.BlockSpec((1, d), lambda b, i: (0, 0)),
                  pl.BlockSpec((1, 6, d), lambda b, i: (b, 0, 0)),
                  pl.BlockSpec((d, n), lambda b, i: (0, 0))]
                 + [pl.BlockSpec((1, hn.shape[0]), lambda b, i: (0, 0)) for hn in head_norms],
        out_specs=[pl.BlockSpec((1, tm, wd), lambda b, i: (b, i, 0)) for wd in widths],
        out_shape=[jax.ShapeDtypeStruct((bsz, s, wd), dt) for wd, dt in zip(widths, dtypes)],
        compiler_params=_cparams("arbitrary", "arbitrary"),
        name="norm_mod_project",
    )(x, norm_w.reshape(1, d), mod, w_bf16, *[hn.reshape(1, -1) for hn in head_norms])


def _swa_kernel(sink_ref, q_ref, kc_ref, kp_ref, vc_ref, vp_ref, bias_ref, o_ref):
    n = pl.program_id(1)
    w = A_WINDOW
    hd = A_HEAD_DIM
    group = A_HEADS // A_KV_HEADS
    q = q_ref[0]
    k = jnp.concatenate([kp_ref[0], kc_ref[0]], axis=0)
    v = jnp.concatenate([vp_ref[0], vc_ref[0]], axis=0)
    v_t = v.astype(F32).T
    ones_rows = (lax.broadcasted_iota(jnp.int32, (SWA_VT_ROWS - hd, 2 * w), 0) == 0).astype(BF16)
    key = lax.broadcasted_iota(jnp.int32, (2 * w, group * w), 0)
    qry = lax.broadcasted_iota(jnp.int32, (2 * w, group * w), 1) % w
    dist = qry + w - key
    mask = (dist >= 0) & (dist < w) & ((n > 0) | (key >= w))
    head_of_lane = lax.broadcasted_iota(jnp.int32, (1, group * w), 1) // w
    logits, sinks, vts = [], [], []
    for kv in range(A_KV_HEADS):
        h0 = kv * group
        k_g = k[:, kv * hd:(kv + 1) * hd]
        vts.append(jnp.concatenate([v_t[kv * hd:(kv + 1) * hd, :].astype(BF16), ones_rows], axis=0))
        q_g = jnp.concatenate([q[:, (h0 + g) * hd:(h0 + g + 1) * hd] for g in range(group)], axis=0)
        bias = jnp.concatenate([bias_ref[h0 + g] for g in range(group)], axis=1)
        logits.append(jnp.where(mask, _dot_nt(k_g, q_g) + bias, NEG_BIG))
        sink = jnp.full((1, group * w), sink_ref[h0 + group - 1] * LOG2E, F32)
        for g in range(group - 2, -1, -1):
            sink = jnp.where(head_of_lane == g, sink_ref[h0 + g] * LOG2E, sink)
        sinks.append(sink)
    maxima = [jnp.maximum(jnp.max(lg, axis=0, keepdims=True), sink) for lg, sink in zip(logits, sinks)]
    accs = [jnp.dot(vt_g, jnp.exp2(lg - m).astype(BF16), preferred_element_type=F32)
            for vt_g, lg, m in zip(vts, logits, maxima)]
    for kv in range(A_KV_HEADS):
        h0 = kv * group
        acc = accs[kv]
        o_t = acc[0:hd] / (acc[hd:hd + 1] + jnp.exp2(sinks[kv] - maxima[kv]))
        for pair in range(group // 2):
            two = jnp.concatenate([o_t[:, (2 * pair) * w:(2 * pair + 1) * w],
                                   o_t[:, (2 * pair + 1) * w:(2 * pair + 2) * w]], axis=0)
            c0 = (h0 + 2 * pair) * hd
            o_ref[0, :, c0:c0 + 2 * hd] = two.T.astype(o_ref.dtype)


def _swa_attention(qa, ka, va, sinks, band_bias):
    bsz, s, _ = qa.shape
    w = A_WINDOW
    nb = s // w
    kvw = A_KV_HEADS * A_HEAD_DIM
    cur = lambda b, n: (b, n, 0)
    prev = lambda b, n: (b, jnp.maximum(n - 1, 0), 0)
    return pl.pallas_call(
        _swa_kernel,
        grid=(bsz, nb),
        in_specs=[pl.BlockSpec(memory_space=pltpu.SMEM),
                  pl.BlockSpec((1, w, A_HEADS * A_HEAD_DIM), cur),
                  pl.BlockSpec((1, w, kvw), cur),
                  pl.BlockSpec((1, w, kvw), prev),
                  pl.BlockSpec((1, w, kvw), cur),
                  pl.BlockSpec((1, w, kvw), prev),
                  pl.BlockSpec((REL_HEADS, 2 * w, w), lambda b, n: (0, 0, 0))],
        out_specs=pl.BlockSpec((1, w, A_HEADS * A_HEAD_DIM), cur),
        out_shape=jax.ShapeDtypeStruct((bsz, s, A_HEADS * A_HEAD_DIM), BF16),
        compiler_params=_cparams("arbitrary", "arbitrary"),
        name="swa_sink_attention",
    )(sinks, qa, ka, ka, va, va, band_bias)


def _gla_kernel(qk_ref, v_ref, r_ref, ab_ref, gu_ref, gb_ref, on_ref, o_ref, state_ref):
    s_idx = pl.program_id(1)
    c_len = B_CHUNK
    dk, dv = B_KEY_DIM, B_VAL_DIM
    hk = B_HEADS * dk

    @pl.when(s_idx == 0)
    def _():
        state_ref[...] = jnp.zeros(state_ref.shape, F32)

    z = _dot(ab_ref[0], gu_ref[...]) + gb_ref[...]
    log_a = (jnp.minimum(z, 0.0) - jnp.log(1.0 + jnp.exp(-jnp.abs(z)))) / B_GATE_TAU
    rows = qk_ref.shape[1]
    chunks = range(rows // c_len)
    heads = range(B_HEADS)
    ri = lax.broadcasted_iota(jnp.int32, (c_len, c_len), 0)
    ci = lax.broadcasted_iota(jnp.int32, (c_len, c_len), 1)
    tril = ri >= ci
    tri = tril.astype(BF16)
    g_hi = log_a.astype(BF16)
    g_lo = (log_a - g_hi.astype(F32)).astype(BF16)
    b_c = [jnp.dot(tri, g_hi[c * c_len:(c + 1) * c_len], preferred_element_type=F32)
           + jnp.dot(tri, g_lo[c * c_len:(c + 1) * c_len], preferred_element_type=F32) for c in chunks]
    b = jnp.concatenate(b_c, axis=0)
    last_c = [bc[c_len - 1:c_len, :] for bc in b_c]
    b_last = jnp.concatenate([jnp.broadcast_to(l, (c_len, hk)) for l in last_c], axis=0)
    q = qk_ref[0, :, 0:hk] * (dk ** -0.5)
    k = qk_ref[0, :, hk:2 * hk]
    q_dec = (q * jnp.exp(b)).astype(BF16)
    k_dec = (k * jnp.exp(-b)).astype(BF16)
    k_upd = (k * jnp.exp(b_last - b)).astype(BF16)
    v = v_ref[0]

    def rows_of(x, c, cols):
        return x[c * c_len:(c + 1) * c_len, cols]

    att = [[jnp.where(tril, _dot_nt(rows_of(q_dec, c, slice(h * dk, (h + 1) * dk)),
                                    rows_of(k_dec, c, slice(h * dk, (h + 1) * dk))), 0.0).astype(BF16)
            for h in heads] for c in chunks]
    upd = [[_dot_tn(rows_of(v, c, slice(h * dv, (h + 1) * dv)), rows_of(k_upd, c, slice(h * dk, (h + 1) * dk)))
            for h in heads] for c in chunks]
    state_in = []
    st = [state_ref[h] for h in heads]
    for c in chunks:
        state_in.append(st)
        decay = jnp.exp(last_c[c])
        st = [st[h] * decay[:, h * dk:(h + 1) * dk] + upd[c][h] for h in heads]
    for h in heads:
        state_ref[h] = st[h]
    for c in chunks:
        outs = []
        for h in heads:
            o_h = (_dot(att[c][h], rows_of(v, c, slice(h * dv, (h + 1) * dv)))
                   + _dot_nt(rows_of(q_dec, c, slice(h * dk, (h + 1) * dk)), state_in[c][h]))
            o_h = _rms_rows(o_h, on_ref[...])
            r_h = r_ref[0, c * c_len:(c + 1) * c_len, h * dv:(h + 1) * dv]
            outs.append(o_h * (r_h * jax.nn.sigmoid(r_h)))
        o_ref[0, c * c_len:(c + 1) * c_len, :] = jnp.concatenate(outs, axis=-1).astype(o_ref.dtype)


def _gla(qk, vb, rb, ab, gate_up_pad, gate_bias, out_norm):
    bsz, s, _ = qk.shape
    tm = min(GLA_ROWS, s)
    hv = B_HEADS * B_VAL_DIM
    hk = B_HEADS * B_KEY_DIM
    blk = lambda b, i: (b, i, 0)
    const = lambda b, i: (0, 0)
    return pl.pallas_call(
        _gla_kernel,
        grid=(bsz, s // tm),
        in_specs=[pl.BlockSpec((1, tm, 2 * hk), blk),
                  pl.BlockSpec((1, tm, hv), blk),
                  pl.BlockSpec((1, tm, hv), blk),
                  pl.BlockSpec((1, tm, LANES), blk),
                  pl.BlockSpec((LANES, hk), const),
                  pl.BlockSpec((1, hk), const),
                  pl.BlockSpec((1, B_VAL_DIM), const)],
        out_specs=pl.BlockSpec((1, tm, hv), blk),
        out_shape=jax.ShapeDtypeStruct((bsz, s, hv), BF16),
        scratch_shapes=[pltpu.VMEM((B_HEADS, B_VAL_DIM, B_KEY_DIM), F32)],
        compiler_params=_cparams("arbitrary", "arbitrary"),
        name="gated_linear_attention",
    )(qk, vb, rb, ab, gate_up_pad, gate_bias.reshape(1, hk), out_norm.reshape(1, -1))


def _moba_kernel(q_ref, k_ref, v_ref, bias_ref, o_ref, vt_ref, kmean_ref, pick_ref):
    i = pl.program_id(2)
    blk = C_BLOCK
    hd = C_HEAD_DIM
    nb = k_ref.shape[1] // blk
    grp = MOBA_GROUP
    heads = range(MOBA_HEADS_PER_STEP)

    def lanes(hh):
        return slice(hh * hd, (hh + 1) * hd)

    @pl.when(i == 0)
    def _():
        ones_rows = (lax.broadcasted_iota(jnp.int32, (MOBA_VT_ROWS - hd, blk), 0) == 0).astype(BF16)
        for hh in heads:
            kn = k_ref[0, :, lanes(hh)].astype(F32)
            kmean_ref[hh] = jnp.mean(kn.reshape(nb, blk, hd), axis=1)
            for j in range(nb):
                vt = v_ref[0, j * blk:(j + 1) * blk, lanes(hh)].astype(F32).T.astype(BF16)
                vt_ref[hh, j] = jnp.concatenate([vt, ones_rows], axis=0)

    def logits(hh, q_h, j):
        j0 = pl.multiple_of(j * blk, blk)
        return _dot_nt(k_ref[0, pl.ds(j0, blk), lanes(hh)], q_h)

    def col_max(lg, picked):
        return jnp.where(picked, jnp.max(lg, axis=0, keepdims=True), NEG_BIG)

    def weighted_v(hh, j, lg, m, picked):
        p = jnp.exp2(lg - m).astype(BF16)
        return jnp.where(picked, jnp.dot(vt_ref[hh, j], p, preferred_element_type=F32), 0.0)

    def softmax_step(ms, accs, tiles):
        m_new = []
        for hh in heads:
            m_h = ms[hh]
            for _, lg, picked in tiles[hh]:
                m_h = jnp.maximum(m_h, col_max(lg, picked))
            m_new.append(m_h)
        out = []
        for hh in heads:
            acc = accs[hh] * jnp.exp2(ms[hh] - m_new[hh])
            for j, lg, picked in tiles[hh]:
                acc = acc + weighted_v(hh, j, lg, m_new[hh], picked)
            out.append(acc)
        return tuple(m_new), tuple(out)

    jrow = lax.broadcasted_iota(jnp.int32, (nb, blk), 0)
    key = lax.broadcasted_iota(jnp.int32, (blk, blk), 0)
    qry = lax.broadcasted_iota(jnp.int32, (blk, blk), 1)
    j_prev = jnp.maximum(i - 1, 0)
    qs = []
    for hh in heads:
        q_h = q_ref[0, :, lanes(hh)]
        qs.append(q_h)
        score = jnp.where(jrow < i, _dot_nt(kmean_ref[hh], q_h), -jnp.inf)
        beats = jnp.zeros((nb, blk), jnp.int32)
        for j2 in range(nb):
            s2 = score[j2:j2 + 1, :]
            beats = beats + ((s2 > score) | ((s2 == score) & (j2 < jrow))).astype(jnp.int32)
        pick_ref[hh] = ((beats < C_TOPK) & (jrow < i)).astype(F32)

    first = []
    for hh in heads:
        lg_own = jnp.where(key <= qry, logits(hh, qs[hh], i) + bias_ref[hh, 0:blk, :], NEG_BIG)
        lg_prev = logits(hh, qs[hh], j_prev) + bias_ref[hh, blk:2 * blk, :]
        pick_prev = pick_ref[hh, pl.ds(j_prev, 1), :] > 0.5
        first.append([(i, lg_own, True), (j_prev, lg_prev, pick_prev)])
    start = (tuple(jnp.full((1, blk), NEG_BIG, F32) for _ in heads),
             tuple(jnp.zeros((MOBA_VT_ROWS, blk), F32) for _ in heads))
    carry0 = softmax_step(start[0], start[1], first)

    n_far = jnp.maximum(i - 1, 0)

    def body(g, carry):
        tiles = []
        for hh in heads:
            row = []
            for u in range(grp):
                j = g * grp + u
                jc = jnp.minimum(j, nb - 1)
                picked = (pick_ref[hh, pl.ds(jc, 1), :] > 0.5) & (j < n_far)
                row.append((jc, logits(hh, qs[hh], jc), picked))
            tiles.append(row)
        return softmax_step(carry[0], carry[1], tiles)

    _, accs = lax.fori_loop(0, (n_far + grp - 1) // grp, body, carry0)
    for hh in heads:
        o_ref[0, :, lanes(hh)] = (accs[hh][0:hd] / accs[hh][hd:hd + 1]).T.astype(o_ref.dtype)


def _moba(q, k, v, bias_tiles):
    bsz, s, _ = q.shape
    blk, hd = C_BLOCK, C_HEAD_DIM
    assert s % blk == 0
    nb = s // blk
    hb = MOBA_HEADS_PER_STEP
    assert C_HEADS % hb == 0
    return pl.pallas_call(
        _moba_kernel,
        grid=(bsz, C_HEADS // hb, nb),
        in_specs=[pl.BlockSpec((1, blk, hb * hd), lambda b, h, i: (b, i, h)),
                  pl.BlockSpec((1, s, hb * hd), lambda b, h, i: (b, 0, h)),
                  pl.BlockSpec((1, s, hb * hd), lambda b, h, i: (b, 0, h)),
                  pl.BlockSpec((hb, 2 * blk, blk), lambda b, h, i: (h, 0, 0))],
        out_specs=pl.BlockSpec((1, blk, hb * hd), lambda b, h, i: (b, i, h)),
        out_shape=jax.ShapeDtypeStruct((bsz, s, C_HEADS * hd), BF16),
        scratch_shapes=[pltpu.VMEM((hb, nb, MOBA_VT_ROWS, blk), BF16),
                        pltpu.VMEM((hb, nb, hd), F32), pltpu.VMEM((hb, nb, blk), F32)],
        compiler_params=_cparams("arbitrary", "arbitrary", "arbitrary"),
        name="moba_attention",
    )(q, k, v, bias_tiles)


def _moe_input(x, nw_ref, mod_ref):
    h = _rms_rows(x, nw_ref[...])
    return h * (1.0 + mod_ref[0, 4:5, :]) + mod_ref[0, 3:4, :]


def _out_route_kernel(*refs, n_in):
    a_refs = refs[:n_in]
    w_refs = refs[n_in:2 * n_in]
    (x_ref, mod_ref, nw_ref, wr_ref, br_ref,
     xo_ref, ids_ref, gate_ref, rank_ref, cnt_ref, base_ref) = refs[2 * n_in:]
    first = (pl.program_id(0) == 0) & (pl.program_id(1) == 0)

    @pl.when(first)
    def _():
        base_ref[...] = jnp.zeros(base_ref.shape, F32)

    y = _dot(a_refs[0][0], w_refs[0][...])
    for a_ref, w_ref in zip(a_refs[1:], w_refs[1:]):
        y = y + _dot(a_ref[0], w_ref[...])
    x = x_ref[0] + mod_ref[0, 2:3, :] * y
    xo_ref[0] = x

    tm = x.shape[0]
    lt = _dot_nt(wr_ref[...], _moe_input(x, nw_ref, mod_ref)) + br_ref[...]
    g = [lt[r:r + 1, :] for r in range(MOE_GROUPS)]
    gmax = functools.reduce(jnp.maximum, g)
    gsel = jnp.full(gmax.shape, MOE_GROUPS - 1, jnp.int32)
    for r in range(MOE_GROUPS - 2, -1, -1):
        gsel = jnp.where(g[r] == gmax, r, gsel)
    p_g = 1.0 / functools.reduce(jnp.add, [jnp.exp(gr - gmax) for gr in g])
    epg = MOE_EXPERTS_PER_GROUP
    e_in = lt[8 + (MOE_GROUPS - 1) * epg:8 + MOE_GROUPS * epg, :]
    for r in range(MOE_GROUPS - 2, -1, -1):
        e_in = jnp.where(gsel == r, lt[8 + r * epg:8 + (r + 1) * epg, :], e_in)
    sub = lax.broadcasted_iota(jnp.int32, (epg, tm), 0)
    v1 = jnp.max(e_in, axis=0, keepdims=True)
    i1 = jnp.min(jnp.where(e_in == v1, sub, epg), axis=0, keepdims=True)
    e2 = jnp.where(sub == i1, -jnp.inf, e_in)
    v2 = jnp.max(e2, axis=0, keepdims=True)
    i2 = jnp.min(jnp.where(e2 == v2, sub, epg), axis=0, keepdims=True)
    t = jnp.exp(v2 - v1)
    w1 = p_g / (1.0 + t)
    w2 = p_g * t / (1.0 + t)
    id1 = gsel * epg + i1
    id2 = gsel * epg + i2
    ids_ref[...] = jnp.concatenate([id1, id2], axis=0)
    gate_ref[...] = jnp.concatenate([w1, w2, jnp.zeros((LANES - 2, tm), F32)], axis=0).T

    eidx = lax.broadcasted_iota(jnp.int32, (MOE_EXPERTS, tm), 0)
    oh1 = eidx == id1
    oh2 = eidx == id2
    onehot = (oh1 | oh2).astype(BF16)
    tr = lax.broadcasted_iota(jnp.int32, (tm, tm), 0)
    tc = lax.broadcasted_iota(jnp.int32, (tm, tm), 1)
    before = (tr < tc).astype(BF16)
    prefix = jnp.dot(onehot, before, preferred_element_type=F32) + base_ref[...]
    r1 = jnp.sum(jnp.where(oh1, prefix, 0.0), axis=0, keepdims=True)
    r2 = jnp.sum(jnp.where(oh2, prefix, 0.0), axis=0, keepdims=True)
    rank_ref[...] = jnp.concatenate([r1, r2], axis=0).astype(jnp.int32)
    base_ref[...] = base_ref[...] + jnp.sum(onehot.astype(F32), axis=1, keepdims=True)
    cnt_ref[...] = base_ref[...].astype(jnp.int32)


def _out_project_route(acts, weights, x, mod, norm_w, wr_t, br):
    bsz, s, d = x.shape
    tm = min(ROUTER_ROWS, s)
    n = bsz * s
    nt = s // tm
    n_in = len(acts)
    tok = lambda b, i: (0, b * nt + i)
    const = lambda b, i: (0, 0)
    in_specs = [pl.BlockSpec((1, tm, a.shape[2]), lambda b, i: (b, i, 0)) for a in acts]
    in_specs += [pl.BlockSpec(w.shape, const) for w in weights]
    in_specs += [pl.BlockSpec((1, tm, d), lambda b, i: (b, i, 0)),
                 pl.BlockSpec((1, 6, d), lambda b, i: (b, 0, 0)),
                 pl.BlockSpec((1, d), const),
                 pl.BlockSpec((ROUTER_LOGIT_ROWS, d), const),
                 pl.BlockSpec((ROUTER_LOGIT_ROWS, 1), const)]
    return pl.pallas_call(
        functools.partial(_out_route_kernel, n_in=n_in),
        grid=(bsz, nt),
        in_specs=in_specs,
        out_specs=[pl.BlockSpec((1, tm, d), lambda b, i: (b, i, 0)),
                   pl.BlockSpec((MOE_TOPK, tm), tok),
                   pl.BlockSpec((tm, LANES), lambda b, i: (b * nt + i, 0)),
                   pl.BlockSpec((MOE_TOPK, tm), tok),
                   pl.BlockSpec((MOE_EXPERTS, 1), const)],
        out_shape=[jax.ShapeDtypeStruct((bsz, s, d), F32),
                   jax.ShapeDtypeStruct((MOE_TOPK, n), jnp.int32),
                   jax.ShapeDtypeStruct((n, LANES), F32),
                   jax.ShapeDtypeStruct((MOE_TOPK, n), jnp.int32),
                   jax.ShapeDtypeStruct((MOE_EXPERTS, 1), jnp.int32)],
        scratch_shapes=[pltpu.VMEM((MOE_EXPERTS, 1), F32)],
        compiler_params=_cparams("arbitrary", "arbitrary"),
        name="out_project_route",
    )(*acts, *weights, x, mod, norm_w.reshape(1, d), wr_t, br)


def _dest_kernel(ids_ref, rank_ref, start_ref, o_ref):
    ids = ids_ref[...]
    tm = ids.shape[1]
    eidx = lax.broadcasted_iota(jnp.int32, (MOE_EXPERTS, tm), 0)
    rows = []
    for k in range(MOE_TOPK):
        base = jnp.sum(jnp.where(eidx == ids[k:k + 1, :], start_ref[...], 0), axis=0, keepdims=True)
        rows.append(base + rank_ref[k:k + 1, :])
    o_ref[...] = jnp.concatenate(rows, axis=0)


def _dest_rows(ids, rank, start):
    n = ids.shape[1]
    tm = min(2048, n)
    return pl.pallas_call(
        _dest_kernel,
        grid=(n // tm,),
        in_specs=[pl.BlockSpec((MOE_TOPK, tm), lambda i: (0, i)),
                  pl.BlockSpec((MOE_TOPK, tm), lambda i: (0, i)),
                  pl.BlockSpec((MOE_EXPERTS, 1), lambda i: (0, 0))],
        out_specs=pl.BlockSpec((MOE_TOPK, tm), lambda i: (0, i)),
        out_shape=jax.ShapeDtypeStruct((MOE_TOPK, n), jnp.int32),
        compiler_params=_cparams("arbitrary"),
        name="moe_dest_rows",
    )(ids, rank, start)


def _row_copy(src, dst, sem, src_row, dst_row, rt):
    return pltpu.make_async_copy(src.at[pl.ds(pl.multiple_of(src_row * rt, rt), rt)],
                                 dst.at[pl.ds(pl.multiple_of(dst_row * rt, rt), rt)], sem)


def _dispatch_kernel(pend_ref, padded_ref, dest_ref, x_ref, mod_ref, nx_ref, nmod_ref, nw_ref, xs_ref,
                     h_ref, zero_ref, sem, *, rt):
    tm = x_ref.shape[1]
    zrows = zero_ref.shape[0] // rt
    step = pl.program_id(0) * pl.num_programs(1) + pl.program_id(1)
    first = step == 0
    slot = step % 2

    @pl.when(first)
    def _():
        zero_ref[...] = jnp.zeros(zero_ref.shape, F32)
        for e in range(MOE_EXPERTS):
            @pl.when(padded_ref[e] > 0)
            def _():
                row0 = pl.multiple_of((pend_ref[e] - zrows) * rt, zrows * rt)
                cp = pltpu.make_async_copy(zero_ref, xs_ref.at[pl.ds(row0, zrows * rt)], sem)
                cp.start()
                cp.wait()

        def zero_tail(b, _):
            row0 = pl.multiple_of(b * zrows * rt, zrows * rt)
            cp = pltpu.make_async_copy(zero_ref, xs_ref.at[pl.ds(row0, zrows * rt)], sem)
            cp.start()
            cp.wait()
            return 0

        lax.fori_loop(pend_ref[MOE_EXPERTS - 1] // zrows, xs_ref.shape[0] // (zrows * rt), zero_tail, 0)

    @pl.when(first)
    def _():
        _to_row_tiles(h_ref, _moe_input(x_ref[0], nw_ref, mod_ref), (0,))

    def start(t, _):
        for k in range(MOE_TOPK):
            _row_copy(h_ref.at[slot], xs_ref, sem, t, dest_ref[k, t], rt).start(priority=k)
        return 0

    lax.fori_loop(0, tm, start, 0, unroll=DMA_ISSUE_UNROLL)

    @pl.when(step + 1 < pl.num_programs(0) * pl.num_programs(1))
    def _():
        _to_row_tiles(h_ref, _moe_input(nx_ref[0], nw_ref, nmod_ref), (1 - slot,))

    for k in range(MOE_TOPK):
        pltpu.make_async_copy(h_ref.at[slot], xs_ref.at[pl.ds(0, tm * rt)], sem).wait()


def _dispatch(x, norm_w, mod, dest, pend, padded, n_rows):
    bsz, s, d = x.shape
    rt = d // LANES
    tm = min(DISPATCH_ROWS, s)
    nt = s // tm
    last = bsz * nt - 1

    def nxt(b, i):
        f = jnp.minimum(b * nt + i + 1, last)
        return f // nt, f % nt

    return pl.pallas_call(
        functools.partial(_dispatch_kernel, rt=rt),
        grid=(bsz, nt),
        in_specs=[pl.BlockSpec(memory_space=pltpu.SMEM),
                  pl.BlockSpec(memory_space=pltpu.SMEM),
                  pl.BlockSpec((MOE_TOPK, tm), lambda b, i: (0, b * nt + i), memory_space=pltpu.SMEM),
                  pl.BlockSpec((1, tm, d), lambda b, i: (b, i, 0)),
                  pl.BlockSpec((1, 6, d), lambda b, i: (b, 0, 0)),
                  pl.BlockSpec((1, tm, d), lambda b, i: (*nxt(b, i), 0)),
                  pl.BlockSpec((1, 6, d), lambda b, i: (nxt(b, i)[0], 0, 0)),
                  pl.BlockSpec((1, d), lambda b, i: (0, 0))],
        out_specs=pl.BlockSpec(memory_space=pl.ANY),
        out_shape=jax.ShapeDtypeStruct((n_rows * rt, LANES), F32),
        scratch_shapes=[pltpu.VMEM((2, tm * rt, LANES), F32), pltpu.VMEM((MOE_BLOCK_ROWS * rt, LANES), F32),
                        pltpu.SemaphoreType.DMA(())],
        compiler_params=_cparams("arbitrary", "arbitrary"),
        name="moe_dispatch",
    )(pend, padded, dest, x, mod, x, mod, norm_w.reshape(1, d))


def _expert_kernel(blk_e_ref, n_used_ref, x_ref, w1_ref, w3_ref, w2_ref, o_ref, w1b_ref, w3b_ref, w2b_ref):
    i = pl.program_id(0)
    used = i < n_used_ref[0]
    new_expert = (i == 0) | (blk_e_ref[i] != blk_e_ref[jnp.maximum(i - 1, 0)])

    @pl.when(used & new_expert)
    def _():
        w1b_ref[...] = w1_ref[0, 0].astype(BF16)
        w3b_ref[...] = w3_ref[0, 0].astype(BF16)
        w2b_ref[...] = w2_ref[0, 0].astype(BF16)

    @pl.when(used)
    def _():
        d = w1b_ref.shape[0]
        rows = x_ref.shape[0] * LANES // d
        x = _from_row_tiles(x_ref, rows, d).astype(BF16)
        a = jnp.dot(x, w1b_ref[...], preferred_element_type=F32)
        g = jnp.dot(x, w3b_ref[...], preferred_element_type=F32)
        y = jnp.dot((a * jax.nn.sigmoid(a) * g).astype(BF16), w2b_ref[...], preferred_element_type=F32)
        _to_row_tiles(o_ref, y)

    @pl.when(jnp.logical_not(used))
    def _():
        o_ref[...] = jnp.zeros(o_ref.shape, F32)


def _expert_ffn(xs, blk_e, n_used, w1, w3, w2, layer):
    d, hid = w1.shape[2], w1.shape[3]
    rt = d // LANES
    n_rows = xs.shape[0] // rt
    tm = MOE_BLOCK_ROWS
    row = lambda i, be, nu: (jnp.minimum(i, nu[0] - 1), 0)
    wsel = lambda i, be, nu: (layer, be[i], 0, 0)
    return pl.pallas_call(
        _expert_kernel,
        grid_spec=pltpu.PrefetchScalarGridSpec(
            num_scalar_prefetch=2,
            grid=(n_rows // tm,),
            in_specs=[pl.BlockSpec((tm * rt, LANES), row),
                      pl.BlockSpec((1, 1, d, hid), wsel),
                      pl.BlockSpec((1, 1, d, hid), wsel),
                      pl.BlockSpec((1, 1, hid, d), wsel)],
            out_specs=pl.BlockSpec((tm * rt, LANES), lambda i, be, nu: (i, 0)),
            scratch_shapes=[pltpu.VMEM((d, hid), BF16), pltpu.VMEM((d, hid), BF16),
                            pltpu.VMEM((hid, d), BF16)]),
        out_shape=jax.ShapeDtypeStruct((n_rows * rt, LANES), F32),
        compiler_params=_cparams("arbitrary"),
        name="moe_expert_ffn",
    )(blk_e, n_used, xs, w1, w3, w2)


def _combine_kernel(dest_ref, next_dest_ref, ys_ref, gate_ref, x_ref, mod_ref, *refs, outs):
    buf_ref, sem = refs[-2:]
    o_ref = refs[-3 - len(outs)]
    tm, d = x_ref.shape[1], x_ref.shape[2]
    rt = d // LANES
    step = pl.program_id(0) * pl.num_programs(1) + pl.program_id(1)
    n_steps = pl.num_programs(0) * pl.num_programs(1)
    slot = step % 2

    def gather(rows_ref, s):
        def start(t, _):
            for k in range(MOE_TOPK):
                _row_copy(ys_ref, buf_ref.at[s, k], sem.at[s], rows_ref[k, t], t, rt).start(priority=k)
            return 0
        lax.fori_loop(0, tm, start, 0, unroll=DMA_ISSUE_UNROLL)

    @pl.when(step == 0)
    def _():
        gather(dest_ref, 0)

    @pl.when(step + 1 < n_steps)
    def _():
        gather(next_dest_ref, 1 - slot)

    for k in range(MOE_TOPK):
        pltpu.make_async_copy(ys_ref.at[pl.ds(0, tm * rt)], buf_ref.at[slot, k], sem.at[slot]).wait()
    g = gate_ref[...]
    y = (g[:, 0:1] * _from_row_tiles(buf_ref, tm, d, (slot, 0))
         + g[:, 1:2] * _from_row_tiles(buf_ref, tm, d, (slot, 1)))
    x = x_ref[0] + mod_ref[0, 5:6, :] * y
    o_ref[0] = x
    if outs:
        nw_ref, next_mod_ref, w_ref = refs[0:3]
        n_norm = len(refs) - 6 - len(outs)
        _project_rows(x, nw_ref, next_mod_ref, w_ref, refs[3:3 + n_norm], refs[-2 - len(outs):-2], outs)


def _combine(ys, dest, gates, x, mod, proj=None):
    bsz, s, d = x.shape
    tm = min(COMBINE_ROWS, s)
    nt = s // tm
    last = bsz * nt - 1
    tile = lambda b, i: (b, i, 0)
    const = lambda b, i: (0, 0)
    in_specs = [pl.BlockSpec((MOE_TOPK, tm), lambda b, i: (0, b * nt + i), memory_space=pltpu.SMEM),
                pl.BlockSpec((MOE_TOPK, tm), lambda b, i: (0, jnp.minimum(b * nt + i + 1, last)),
                             memory_space=pltpu.SMEM),
                pl.BlockSpec(memory_space=pl.ANY),
                pl.BlockSpec((tm, LANES), lambda b, i: (b * nt + i, 0)),
                pl.BlockSpec((1, tm, d), tile),
                pl.BlockSpec((1, 6, d), lambda b, i: (b, 0, 0))]
    args = [dest, dest, ys, gates, x, mod]
    out_specs = [pl.BlockSpec((1, tm, d), tile)]
    out_shape = [jax.ShapeDtypeStruct((bsz, s, d), F32)]
    outs = ()
    if proj is not None:
        norm_w, next_mod, w_bf16, outs, dtypes, head_norms = proj
        in_specs += [pl.BlockSpec((1, d), const), pl.BlockSpec((1, 6, d), lambda b, i: (b, 0, 0)),
                     pl.BlockSpec(w_bf16.shape, const)]
        in_specs += [pl.BlockSpec((1, hn.shape[0]), const) for hn in head_norms]
        args += [norm_w.reshape(1, d), next_mod, w_bf16] + [hn.reshape(1, -1) for hn in head_norms]
        out_specs += [pl.BlockSpec((1, tm, o[0]), tile) for o in outs]
        out_shape += [jax.ShapeDtypeStruct((bsz, s, o[0]), dt) for o, dt in zip(outs, dtypes)]
    res = pl.pallas_call(
        functools.partial(_combine_kernel, outs=tuple(outs)),
        grid=(bsz, nt),
        in_specs=in_specs,
        out_specs=out_specs,
        out_shape=out_shape,
        scratch_shapes=[pltpu.VMEM((2, MOE_TOPK, tm * d // LANES, LANES), F32), pltpu.SemaphoreType.DMA((2,))],
        compiler_params=_cparams("arbitrary", "arbitrary"),
        name="moe_combine" if proj is None else "moe_combine_project",
    )(*args)
    return res[0] if proj is None else res


def _mixer_out_moe_residual(acts, w_outs, x, norm_w, mod, w_group, b_group, w_expert, b_expert, w1, w3, w2, layer,
                            next_proj):
    bsz, s, d = x.shape
    n = bsz * s
    tm = MOE_BLOCK_ROWS
    wr_t = jnp.zeros((ROUTER_LOGIT_ROWS, d), F32).at[0:MOE_GROUPS].set(w_group.T).at[8:].set(w_expert.T)
    br = jnp.zeros((ROUTER_LOGIT_ROWS, 1), F32).at[0:MOE_GROUPS, 0].set(b_group).at[8:, 0].set(b_expert)
    x_mid, ids, gates, rank, counts = _out_project_route(acts, w_outs, x, mod, norm_w, wr_t.astype(BF16), br)
    counts = counts[:, 0]
    padded = ((counts + tm - 1) // tm) * tm
    pend = jnp.cumsum(padded).astype(jnp.int32)
    start = pend - padded
    n_rows = ((n * MOE_TOPK + tm - 1) // tm) * tm + MOE_EXPERTS * tm
    blk_start = jnp.arange(n_rows // tm, dtype=jnp.int32) * tm
    blk_e = jnp.minimum(jnp.sum((pend[None, :] <= blk_start[:, None]).astype(jnp.int32), axis=1),
                        MOE_EXPERTS - 1)
    n_used = pend[-1:] // tm
    dest = _dest_rows(ids, rank, start.reshape(MOE_EXPERTS, 1))
    xs = _dispatch(x_mid, norm_w, mod, dest, pend, padded, n_rows)
    ys = _expert_ffn(xs, blk_e, n_used, w1, w3, w2, layer)
    return _combine(ys, dest, gates, x_mid, mod, next_proj)


def _even_proj_spec(w_in, q_norm, k_norm):
    d, n_in = w_in.shape
    aq = A_HEADS * A_HEAD_DIM
    akv = A_KV_HEADS * A_HEAD_DIM
    bk = B_HEADS * B_KEY_DIM
    bv = B_HEADS * B_VAL_DIM
    w_pad = jnp.zeros((d, n_in - B_GATE_RANK + LANES), F32).at[:, :n_in].set(w_in).astype(BF16)
    outs = ((aq, A_HEAD_DIM, 0, A_HEAD_DIM ** -0.5 * LOG2E), (akv, A_HEAD_DIM, 1, 1.0), (akv, 0, 0, 1.0),
            (2 * bk, 0, 0, 1.0), (bv, 0, 0, 1.0), (bv, 0, 0, 1.0), (LANES, 0, 0, 1.0))
    dtypes = (BF16, BF16, BF16, F32, BF16, F32, F32)
    return w_pad, outs, dtypes, (q_norm, k_norm)


def _odd_proj_spec(w_in, q_norm, k_norm):
    mix = C_HEADS * C_HEAD_DIM
    outs = ((mix, C_HEAD_DIM, 0, C_HEAD_DIM ** -0.5 * LOG2E), (mix, C_HEAD_DIM, 1, 1.0), (mix, 0, 0, 1.0))
    return w_in.astype(BF16), outs, (BF16, BF16, BF16), (q_norm, k_norm)


def _even_mixer(groups, w_out, sinks, gate_up, gate_bias, out_norm, band_bias):
    qa, ka, va, qk, vb, rb, ab = groups
    aq = A_HEADS * A_HEAD_DIM
    bk = B_HEADS * B_KEY_DIM
    oa = _swa_attention(qa, ka, va, sinks, band_bias)
    gu_pad = jnp.zeros((LANES, bk), F32).at[:B_GATE_RANK].set(gate_up).astype(BF16)
    ob = _gla(qk, vb, rb, ab, gu_pad, gate_bias, out_norm)
    w_out = w_out.astype(BF16)
    return [oa, ob], [w_out[:aq], w_out[aq:]]


def _odd_mixer(groups, w_out, bias_tiles):
    q, k, v = groups
    return [_moba(q, k, v, bias_tiles)], [w_out.astype(BF16)]


def kernel(x, c, rel_bias, ada_w, ada_b, norm1_w, norm2_w, even_w_in, even_w_out, a_q_norm, a_k_norm, a_sinks, b_gate_up, b_gate_bias, b_out_norm, odd_w_in, odd_w_out, c_q_norm, c_k_norm, moe_w_group, moe_b_group, moe_w_expert, moe_b_expert, moe_w1, moe_w3, moe_w2):
    depth = ada_w.shape[0]
    bsz, _, d = x.shape
    moba_bias, band_bias = _bias_tiles(rel_bias)
    mod_all = _adaln(c, ada_w, ada_b).reshape(depth, bsz, 6, d)

    def proj_spec(layer):
        j = layer // 2
        if layer % 2 == 0:
            return _even_proj_spec(even_w_in[j], a_q_norm[j], a_k_norm[j])
        return _odd_proj_spec(odd_w_in[j], c_q_norm[j], c_k_norm[j])

    w_in, outs, dtypes, head_norms = proj_spec(0)
    groups = _norm_mod_project(x, norm1_w[0], mod_all[0], w_in, outs, dtypes, head_norms)
    for layer in range(depth):
        mod = mod_all[layer]
        j = layer // 2
        if layer % 2 == 0:
            acts, w_outs = _even_mixer(groups, even_w_out[j], a_sinks[j], b_gate_up[j], b_gate_bias[j],
                                       b_out_norm[j], band_bias)
        else:
            acts, w_outs = _odd_mixer(groups, odd_w_out[j], moba_bias)
        next_proj = None
        if layer + 1 < depth:
            w_in, outs, dtypes, head_norms = proj_spec(layer + 1)
            next_proj = (norm1_w[layer + 1], mod_all[layer + 1], w_in, outs, dtypes, head_norms)
        res = _mixer_out_moe_residual(acts, w_outs, x, norm2_w[layer], mod, moe_w_group[layer],
                                      moe_b_group[layer], moe_w_expert[layer], moe_b_expert[layer],
                                      moe_w1, moe_w3, moe_w2, layer, next_proj)
        x, groups = (res, None) if next_proj is None else (res[0], res[1:])
    return x
```

```python
import functools
import math

import jax
import jax.numpy as jnp
import numpy as np
from jax import lax
from jax.experimental import pallas as pl
from jax.experimental.pallas import tpu as pltpu

RMS_EPS = 1e-6
A_HEADS, A_KV_HEADS, A_HEAD_DIM, A_WINDOW = 8, 2, 64, 128
B_HEADS, B_KEY_DIM, B_VAL_DIM, B_GATE_RANK, B_GATE_TAU, B_CHUNK = 4, 64, 128, 16, 16.0, 64
C_HEADS, C_HEAD_DIM, C_BLOCK, C_TOPK = 8, 128, 256, 3
REL_BUCKETS, REL_MAX_DIST, REL_HEADS = 32, 128, 8
MOE_GROUPS, MOE_EXPERTS_PER_GROUP, MOE_TOPK = 4, 8, 2
MOE_EXPERTS = MOE_GROUPS * MOE_EXPERTS_PER_GROUP

LANES = 128
V7X_VMEM_LIMIT_BYTES = 56 * 1024 * 1024
NEG_BIG = -1e30
INT32_MIN = -2 ** 31
PROJ_ROWS = 512
GLA_ROWS = 512
ROUTER_ROWS = 512
MOE_BLOCK_ROWS = 512
DISPATCH_ROWS = 1024
COMBINE_ROWS = 512
DMA_ISSUE_UNROLL = 8
ROUTER_LOGIT_ROWS = 8 + MOE_EXPERTS
MOBA_GROUP = 2
MOBA_HEADS_PER_STEP = 4
LOG2E = math.log2(math.e)
SWA_VT_ROWS = A_HEAD_DIM + 16
MOBA_VT_ROWS = C_HEAD_DIM + 16

F32 = jnp.float32
BF16 = jnp.bfloat16


def _cparams(*sem):
    return pltpu.CompilerParams(dimension_semantics=sem, vmem_limit_bytes=V7X_VMEM_LIMIT_BYTES)


def _rel_bucket_np(dist):
    exact = REL_BUCKETS // 2
    d = np.maximum(dist, 0)
    logd = np.log(np.maximum(d, 1).astype(np.float64) / exact) / math.log(REL_MAX_DIST / exact)
    far = np.minimum(exact + (logd * (REL_BUCKETS - exact)).astype(np.int64), REL_BUCKETS - 1)
    return np.where(d < exact, d, far).astype(np.int32)


def _bucket_tiles():
    li = np.arange(C_BLOCK)
    own = _rel_bucket_np(li[None, :] - li[:, None])
    prev = _rel_bucket_np(li[None, :] - li[:, None] + C_BLOCK)
    band = _rel_bucket_np(np.arange(A_WINDOW)[None, :] + A_WINDOW - np.arange(2 * A_WINDOW)[:, None])
    return np.concatenate([own, prev], axis=0), band


def _rms_rows(x, w):
    return x * lax.rsqrt(jnp.mean(x * x, axis=-1, keepdims=True) + RMS_EPS) * w


def _dot(a, b):
    return jnp.dot(a.astype(BF16), b.astype(BF16), preferred_element_type=F32)


def _dot_nt(a, b):
    return lax.dot_general(a.astype(BF16), b.astype(BF16), (((1,), (1,)), ((), ())),
                           preferred_element_type=F32)


def _order_key(x):
    bits = lax.bitcast_convert_type(x, jnp.int32)
    return jnp.where(bits < 0, bits ^ jnp.int32(0x7FFFFFFF), bits)


def _to_row_tiles(ref, x, lead=()):
    rows, d = x.shape
    chunks = d // LANES
    for c in range(chunks):
        ref[lead + (pl.ds(c, rows, stride=chunks), slice(None))] = x[:, c * LANES:(c + 1) * LANES]


def _from_row_tiles(ref, rows, d, lead=()):
    chunks = d // LANES
    return jnp.concatenate([ref[lead + (pl.ds(c, rows, stride=chunks), slice(None))] for c in range(chunks)],
                           axis=1)


def _dot_tn(a, b):
    return lax.dot_general(a.astype(BF16), b.astype(BF16), (((0,), (0,)), ((), ())),
                           preferred_element_type=F32)


def _bias_kernel(rb_ref, bkt_ref, o_ref, *, relative_to_last):
    h = pl.program_id(0)
    bkt = bkt_ref[...]
    acc = jnp.zeros(bkt.shape, F32)
    for b in range(REL_BUCKETS):
        acc = jnp.where(bkt == b, rb_ref[b, h], acc)
    if relative_to_last:
        acc = acc - rb_ref[REL_BUCKETS - 1, h]
    o_ref[0] = acc * LOG2E


def _bias_tile(rel_bias, bkt, relative_to_last):
    rows, cols = bkt.shape
    return pl.pallas_call(
        functools.partial(_bias_kernel, relative_to_last=relative_to_last),
        grid=(REL_HEADS,),
        in_specs=[pl.BlockSpec(memory_space=pltpu.SMEM),
                  pl.BlockSpec((rows, cols), lambda h: (0, 0))],
        out_specs=pl.BlockSpec((1, rows, cols), lambda h: (h, 0, 0)),
        out_shape=jax.ShapeDtypeStruct((REL_HEADS, rows, cols), F32),
        compiler_params=_cparams("arbitrary"),
        name="rel_bias_tiles",
    )(rel_bias, jnp.asarray(bkt))


def _bias_tiles(rel_bias):
    moba_bkt, band_bkt = _bucket_tiles()
    return _bias_tile(rel_bias, moba_bkt, True), _bias_tile(rel_bias, band_bkt, False)


def _adaln_kernel(c_ref, w_ref, b_ref, o_ref):
    c = c_ref[...]
    cond = c * jax.nn.sigmoid(c)
    o_ref[0] = _dot(cond, w_ref[0]) + b_ref[0]


def _adaln(c, ada_w, ada_b):
    depth, d, n6 = ada_w.shape
    bsz = c.shape[0]
    tn = 1536 if n6 % 1536 == 0 else n6
    return pl.pallas_call(
        _adaln_kernel,
        grid=(depth, n6 // tn),
        in_specs=[pl.BlockSpec((bsz, d), lambda l, j: (0, 0)),
                  pl.BlockSpec((1, d, tn), lambda l, j: (l, 0, j)),
                  pl.BlockSpec((1, 1, tn), lambda l, j: (l, 0, j))],
        out_specs=pl.BlockSpec((1, bsz, tn), lambda l, j: (l, 0, j)),
        out_shape=jax.ShapeDtypeStruct((depth, bsz, n6), F32),
        compiler_params=_cparams("arbitrary", "arbitrary"),
        name="adaln_mod",
    )(c, ada_w, ada_b.reshape(depth, 1, n6))


def _head_rms(y, w_ref, hd, post_scale):
    cols = y.shape[1]
    lane = lax.broadcasted_iota(jnp.int32, (1, LANES), 1)
    w = w_ref[...] * post_scale
    out = []
    for c in range(cols // LANES):
        t = y[:, c * LANES:(c + 1) * LANES]
        sq = t * t
        if hd == LANES:
            r = lax.rsqrt(jnp.sum(sq, axis=-1, keepdims=True) * (1.0 / hd) + RMS_EPS)
            out.append(t * r * w)
        else:
            low = lane < hd
            s_lo = jnp.sum(jnp.where(low, sq, 0.0), axis=-1, keepdims=True)
            s_hi = jnp.sum(jnp.where(low, 0.0, sq), axis=-1, keepdims=True)
            r = lax.rsqrt(jnp.where(low, s_lo, s_hi) * (1.0 / hd) + RMS_EPS)
            out.append(t * r * jnp.concatenate([w, w], axis=1))
    return jnp.concatenate(out, axis=1) if len(out) > 1 else out[0]


def _project_rows(x, nw_ref, mod_ref, w_ref, norm_refs, o_refs, outs):
    h = _rms_rows(x, nw_ref[...])
    h = h * (1.0 + mod_ref[0, 1:2, :]) + mod_ref[0, 0:1, :]
    y = _dot(h, w_ref[...])
    off = 0
    for o_ref, (wd, hd, widx, post) in zip(o_refs, outs):
        t = y[:, off:off + wd]
        if hd:
            t = _head_rms(t, norm_refs[widx], hd, post)
        o_ref[0] = t.astype(o_ref.dtype)
        off += wd


def _proj_kernel(x_ref, nw_ref, mod_ref, w_ref, *refs, outs):
    n_norm = len(refs) - len(outs)
    _project_rows(x_ref[0], nw_ref, mod_ref, w_ref, refs[:n_norm], refs[n_norm:], outs)


def _norm_mod_project(x, norm_w, mod, w_bf16, outs, dtypes, head_norms):
    bsz, s, d = x.shape
    tm = min(PROJ_ROWS, s)
    n = w_bf16.shape[1]
    widths = [o[0] for o in outs]
    assert sum(widths) == n and all(wd % LANES == 0 for wd in widths)
    return pl.pallas_call(
        functools.partial(_proj_kernel, outs=tuple(outs)),
        grid=(bsz, s // tm),
        in_specs=[pl.BlockSpec((1, tm, d), lambda b, i: (b, i, 0)),
                  pl.BlockSpec((1, d), lambda b, i: (0, 0)),
                  pl.BlockSpec((1, 6, d), lambda b, i: (b, 0, 0)),
                  pl.BlockSpec((d, n), lambda b, i: (0, 0))]
                 + [pl.BlockSpec((1, hn.shape[0]), lambda b, i: (0, 0)) for hn in head_norms],
        out_specs=[pl.BlockSpec((1, tm, wd), lambda b, i: (b, i, 0)) for wd in widths],
        out_shape=[jax.ShapeDtypeStruct((bsz, s, wd), dt) for wd, dt in zip(widths, dtypes)],
        compiler_params=_cparams("arbitrary", "arbitrary"),
        name="norm_mod_project",
    )(x, norm_w.reshape(1, d), mod, w_bf16, *[hn.reshape(1, -1) for hn in head_norms])


def _swa_kernel(sink_ref, q_ref, kc_ref, kp_ref, vc_ref, vp_ref, bias_ref, o_ref):
    n = pl.program_id(1)
    w = A_WINDOW
    hd = A_HEAD_DIM
    group = A_HEADS // A_KV_HEADS
    q = q_ref[0]
    k = jnp.concatenate([kp_ref[0], kc_ref[0]], axis=0)
    v = jnp.concatenate([vp_ref[0], vc_ref[0]], axis=0)
    v_t = v.astype(F32).T
    ones_rows = (lax.broadcasted_iota(jnp.int32, (SWA_VT_ROWS - hd, 2 * w), 0) == 0).astype(BF16)
    key = lax.broadcasted_iota(jnp.int32, (2 * w, group * w), 0)
    qry = lax.broadcasted_iota(jnp.int32, (2 * w, group * w), 1) % w
    dist = qry + w - key
    mask = (dist >= 0) & (dist < w) & ((n > 0) | (key >= w))
    head_of_lane = lax.broadcasted_iota(jnp.int32, (1, group * w), 1) // w
    logits, sinks, vts = [], [], []
    for kv in range(A_KV_HEADS):
        h0 = kv * group
        k_g = k[:, kv * hd:(kv + 1) * hd]
        vts.append(jnp.concatenate([v_t[kv * hd:(kv + 1) * hd, :].astype(BF16), ones_rows], axis=0))
        q_g = jnp.concatenate([q[:, (h0 + g) * hd:(h0 + g + 1) * hd] for g in range(group)], axis=0)
        bias = jnp.concatenate([bias_ref[h0 + g] for g in range(group)], axis=1)
        logits.append(jnp.where(mask, _dot_nt(k_g, q_g) + bias, NEG_BIG))
        sink = jnp.full((1, group * w), sink_ref[h0 + group - 1] * LOG2E, F32)
        for g in range(group - 2, -1, -1):
            sink = jnp.where(head_of_lane == g, sink_ref[h0 + g] * LOG2E, sink)
        sinks.append(sink)
    maxima = [jnp.maximum(jnp.max(lg, axis=0, keepdims=True), sink) for lg, sink in zip(logits, sinks)]
    accs = [jnp.dot(vt_g, jnp.exp2(lg - m).astype(BF16), preferred_element_type=F32)
            for vt_g, lg, m in zip(vts, logits, maxima)]
    for kv in range(A_KV_HEADS):
        h0 = kv * group
        acc = accs[kv]
        o_t = acc[0:hd] / (acc[hd:hd + 1] + jnp.exp2(sinks[kv] - maxima[kv]))
        for pair in range(group // 2):
            two = jnp.concatenate([o_t[:, (2 * pair) * w:(2 * pair + 1) * w],
                                   o_t[:, (2 * pair + 1) * w:(2 * pair + 2) * w]], axis=0)
            c0 = (h0 + 2 * pair) * hd
            o_ref[0, :, c0:c0 + 2 * hd] = two.T.astype(o_ref.dtype)


def _swa_attention(qa, ka, va, sinks, band_bias):
    bsz, s, _ = qa.shape
    w = A_WINDOW
    nb = s // w
    kvw = A_KV_HEADS * A_HEAD_DIM
    cur = lambda b, n: (b, n, 0)
    prev = lambda b, n: (b, jnp.maximum(n - 1, 0), 0)
    return pl.pallas_call(
        _swa_kernel,
        grid=(bsz, nb),
        in_specs=[pl.BlockSpec(memory_space=pltpu.SMEM),
                  pl.BlockSpec((1, w, A_HEADS * A_HEAD_DIM), cur),
                  pl.BlockSpec((1, w, kvw), cur),
                  pl.BlockSpec((1, w, kvw), prev),
                  pl.BlockSpec((1, w, kvw), cur),
                  pl.BlockSpec((1, w, kvw), prev),
                  pl.BlockSpec((REL_HEADS, 2 * w, w), lambda b, n: (0, 0, 0))],
        out_specs=pl.BlockSpec((1, w, A_HEADS * A_HEAD_DIM), cur),
        out_shape=jax.ShapeDtypeStruct((bsz, s, A_HEADS * A_HEAD_DIM), BF16),
        compiler_params=_cparams("arbitrary", "arbitrary"),
        name="swa_sink_attention",
    )(sinks, qa, ka, ka, va, va, band_bias)


def _gla_kernel(qk_ref, v_ref, r_ref, ab_ref, gu_ref, gb_ref, on_ref, o_ref, state_ref):
    s_idx = pl.program_id(1)
    c_len = B_CHUNK
    dk, dv = B_KEY_DIM, B_VAL_DIM
    hk = B_HEADS * dk

    @pl.when(s_idx == 0)
    def _():
        state_ref[...] = jnp.zeros(state_ref.shape, F32)

    z = _dot(ab_ref[0], gu_ref[...]) + gb_ref[...]
    log_a = (jnp.minimum(z, 0.0) - jnp.log(1.0 + jnp.exp(-jnp.abs(z)))) / B_GATE_TAU
    rows = qk_ref.shape[1]
    chunks = range(rows // c_len)
    heads = range(B_HEADS)
    ri = lax.broadcasted_iota(jnp.int32, (c_len, c_len), 0)
    ci = lax.broadcasted_iota(jnp.int32, (c_len, c_len), 1)
    tril = ri >= ci
    tri = tril.astype(BF16)
    g_hi = log_a.astype(BF16)
    g_lo = (log_a - g_hi.astype(F32)).astype(BF16)
    b_c = [jnp.dot(tri, g_hi[c * c_len:(c + 1) * c_len], preferred_element_type=F32)
           + jnp.dot(tri, g_lo[c * c_len:(c + 1) * c_len], preferred_element_type=F32) for c in chunks]
    b = jnp.concatenate(b_c, axis=0)
    last_c = [bc[c_len - 1:c_len, :] for bc in b_c]
    b_last = jnp.concatenate([jnp.broadcast_to(l, (c_len, hk)) for l in last_c], axis=0)
    q = qk_ref[0, :, 0:hk] * (dk ** -0.5)
    k = qk_ref[0, :, hk:2 * hk]
    q_dec = (q * jnp.exp(b)).astype(BF16)
    k_dec = (k * jnp.exp(-b)).astype(BF16)
    k_upd = (k * jnp.exp(b_last - b)).astype(BF16)
    v = v_ref[0]

    def rows_of(x, c, cols):
        return x[c * c_len:(c + 1) * c_len, cols]

    att = [[jnp.where(tril, _dot_nt(rows_of(q_dec, c, slice(h * dk, (h + 1) * dk)),
                                    rows_of(k_dec, c, slice(h * dk, (h + 1) * dk))), 0.0).astype(BF16)
            for h in heads] for c in chunks]
    upd = [[_dot_tn(rows_of(v, c, slice(h * dv, (h + 1) * dv)), rows_of(k_upd, c, slice(h * dk, (h + 1) * dk)))
            for h in heads] for c in chunks]
    state_in = []
    st = [state_ref[h] for h in heads]
    for c in chunks:
        state_in.append(st)
        decay = jnp.exp(last_c[c])
        st = [st[h] * decay[:, h * dk:(h + 1) * dk] + upd[c][h] for h in heads]
    for h in heads:
        state_ref[h] = st[h]
    for c in chunks:
        outs = []
        for h in heads:
            o_h = (_dot(att[c][h], rows_of(v, c, slice(h * dv, (h + 1) * dv)))
                   + _dot_nt(rows_of(q_dec, c, slice(h * dk, (h + 1) * dk)), state_in[c][h]))
            o_h = _rms_rows(o_h, on_ref[...])
            r_h = r_ref[0, c * c_len:(c + 1) * c_len, h * dv:(h + 1) * dv]
            outs.append(o_h * (r_h * jax.nn.sigmoid(r_h)))
        o_ref[0, c * c_len:(c + 1) * c_len, :] = jnp.concatenate(outs, axis=-1).astype(o_ref.dtype)


def _gla(qk, vb, rb, ab, gate_up_pad, gate_bias, out_norm):
    bsz, s, _ = qk.shape
    tm = min(GLA_ROWS, s)
    hv = B_HEADS * B_VAL_DIM
    hk = B_HEADS * B_KEY_DIM
    blk = lambda b, i: (b, i, 0)
    const = lambda b, i: (0, 0)
    return pl.pallas_call(
        _gla_kernel,
        grid=(bsz, s // tm),
        in_specs=[pl.BlockSpec((1, tm, 2 * hk), blk),
                  pl.BlockSpec((1, tm, hv), blk),
                  pl.BlockSpec((1, tm, hv), blk),
                  pl.BlockSpec((1, tm, LANES), blk),
                  pl.BlockSpec((LANES, hk), const),
                  pl.BlockSpec((1, hk), const),
                  pl.BlockSpec((1, B_VAL_DIM), const)],
        out_specs=pl.BlockSpec((1, tm, hv), blk),
        out_shape=jax.ShapeDtypeStruct((bsz, s, hv), BF16),
        scratch_shapes=[pltpu.VMEM((B_HEADS, B_VAL_DIM, B_KEY_DIM), F32)],
        compiler_params=_cparams("arbitrary", "arbitrary"),
        name="gated_linear_attention",
    )(qk, vb, rb, ab, gate_up_pad, gate_bias.reshape(1, hk), out_norm.reshape(1, -1))


def _moba_kernel(q_ref, k_ref, v_ref, bias_ref, o_ref, vt_ref, kmean_ref, pick_ref):
    i = pl.program_id(2)
    blk = C_BLOCK
    hd = C_HEAD_DIM
    nb = k_ref.shape[1] // blk
    grp = MOBA_GROUP
    heads = range(MOBA_HEADS_PER_STEP)

    def lanes(hh):
        return slice(hh * hd, (hh + 1) * hd)

    @pl.when(i == 0)
    def _():
        ones_rows = (lax.broadcasted_iota(jnp.int32, (MOBA_VT_ROWS - hd, blk), 0) == 0).astype(BF16)
        for hh in heads:
            kn = k_ref[0, :, lanes(hh)].astype(F32)
            kmean_ref[hh] = jnp.mean(kn.reshape(nb, blk, hd), axis=1)
            for j in range(nb):
                vt = v_ref[0, j * blk:(j + 1) * blk, lanes(hh)].astype(F32).T.astype(BF16)
                vt_ref[hh, j] = jnp.concatenate([vt, ones_rows], axis=0)

    def logits(hh, q_h, j):
        j0 = pl.multiple_of(j * blk, blk)
        return _dot_nt(k_ref[0, pl.ds(j0, blk), lanes(hh)], q_h)

    def col_max(lg, picked):
        return jnp.where(picked, jnp.max(lg, axis=0, keepdims=True), NEG_BIG)

    def weighted_v(hh, j, lg, m, picked):
        p = jnp.exp2(lg - m).astype(BF16)
        return jnp.where(picked, jnp.dot(vt_ref[hh, j], p, preferred_element_type=F32), 0.0)

    def softmax_step(ms, accs, tiles):
        m_new = []
        for hh in heads:
            m_h = ms[hh]
            for _, lg, picked in tiles[hh]:
                m_h = jnp.maximum(m_h, col_max(lg, picked))
            m_new.append(m_h)
        out = []
        for hh in heads:
            acc = accs[hh] * jnp.exp2(ms[hh] - m_new[hh])
            for j, lg, picked in tiles[hh]:
                acc = acc + weighted_v(hh, j, lg, m_new[hh], picked)
            out.append(acc)
        return tuple(m_new), tuple(out)

    jrow = lax.broadcasted_iota(jnp.int32, (nb, blk), 0)
    key = lax.broadcasted_iota(jnp.int32, (blk, blk), 0)
    qry = lax.broadcasted_iota(jnp.int32, (blk, blk), 1)
    j_prev = jnp.maximum(i - 1, 0)
    qs = []
    for hh in heads:
        q_h = q_ref[0, :, lanes(hh)]
        qs.append(q_h)
        key_h = jnp.where(jrow < i, _order_key(_dot_nt(kmean_ref[hh], q_h)), INT32_MIN)
        chosen = jnp.zeros((nb, blk), F32)
        for _ in range(C_TOPK):
            best = jnp.max(key_h, axis=0, keepdims=True)
            arg = jnp.min(jnp.where(key_h == best, jrow, nb), axis=0, keepdims=True)
            hit = jrow == arg
            chosen = jnp.where(hit, 1.0, chosen)
            key_h = jnp.where(hit, INT32_MIN, key_h)
        pick_ref[hh] = jnp.where(jrow < i, chosen, 0.0)

    first = []
    for hh in heads:
        lg_own = jnp.where(key <= qry, logits(hh, qs[hh], i) + bias_ref[hh, 0:blk, :], NEG_BIG)
        lg_prev = logits(hh, qs[hh], j_prev) + bias_ref[hh, blk:2 * blk, :]
        pick_prev = pick_ref[hh, pl.ds(j_prev, 1), :] > 0.5
        first.append([(i, lg_own, True), (j_prev, lg_prev, pick_prev)])
    start = (tuple(jnp.full((1, blk), NEG_BIG, F32) for _ in heads),
             tuple(jnp.zeros((MOBA_VT_ROWS, blk), F32) for _ in heads))
    carry0 = softmax_step(start[0], start[1], first)

    n_far = jnp.maximum(i - 1, 0)

    def body(g, carry):
        tiles = []
        for hh in heads:
            row = []
            for u in range(grp):
                j = g * grp + u
                jc = jnp.minimum(j, nb - 1)
                picked = (pick_ref[hh, pl.ds(jc, 1), :] > 0.5) & (j < n_far)
                row.append((jc, logits(hh, qs[hh], jc), picked))
            tiles.append(row)
        return softmax_step(carry[0], carry[1], tiles)

    _, accs = lax.fori_loop(0, (n_far + grp - 1) // grp, body, carry0)
    for hh in heads:
        o_ref[0, :, lanes(hh)] = (accs[hh][0:hd] / accs[hh][hd:hd + 1]).T.astype(o_ref.dtype)


def _moba(q, k, v, bias_tiles):
    bsz, s, _ = q.shape
    blk, hd = C_BLOCK, C_HEAD_DIM
    assert s % blk == 0
    nb = s // blk
    hb = MOBA_HEADS_PER_STEP
    assert C_HEADS % hb == 0
    return pl.pallas_call(
        _moba_kernel,
        grid=(bsz, C_HEADS // hb, nb),
        in_specs=[pl.BlockSpec((1, blk, hb * hd), lambda b, h, i: (b, i, h)),
                  pl.BlockSpec((1, s, hb * hd), lambda b, h, i: (b, 0, h)),
                  pl.BlockSpec((1, s, hb * hd), lambda b, h, i: (b, 0, h)),
                  pl.BlockSpec((hb, 2 * blk, blk), lambda b, h, i: (h, 0, 0))],
        out_specs=pl.BlockSpec((1, blk, hb * hd), lambda b, h, i: (b, i, h)),
        out_shape=jax.ShapeDtypeStruct((bsz, s, C_HEADS * hd), BF16),
        scratch_shapes=[pltpu.VMEM((hb, nb, MOBA_VT_ROWS, blk), BF16),
                        pltpu.VMEM((hb, nb, hd), F32), pltpu.VMEM((hb, nb, blk), F32)],
        compiler_params=_cparams("arbitrary", "arbitrary", "arbitrary"),
        name="moba_attention",
    )(q, k, v, bias_tiles)


def _moe_input(x, nw_ref, mod_ref):
    h = _rms_rows(x, nw_ref[...])
    return h * (1.0 + mod_ref[0, 4:5, :]) + mod_ref[0, 3:4, :]


def _out_route_kernel(*refs, n_in):
    a_refs = refs[:n_in]
    w_refs = refs[n_in:2 * n_in]
    (x_ref, mod_ref, nw_ref, wr_ref, br_ref,
     xo_ref, ids_ref, gate_ref, rank_ref, cnt_ref, base_ref) = refs[2 * n_in:]
    first = (pl.program_id(0) == 0) & (pl.program_id(1) == 0)

    @pl.when(first)
    def _():
        base_ref[...] = jnp.zeros(base_ref.shape, F32)

    y = _dot(a_refs[0][0], w_refs[0][...])
    for a_ref, w_ref in zip(a_refs[1:], w_refs[1:]):
        y = y + _dot(a_ref[0], w_ref[...])
    x = x_ref[0] + mod_ref[0, 2:3, :] * y
    xo_ref[0] = x

    tm = x.shape[0]
    lt = _dot_nt(wr_ref[...], _moe_input(x, nw_ref, mod_ref)) + br_ref[...]
    g = [lt[r:r + 1, :] for r in range(MOE_GROUPS)]
    gmax = functools.reduce(jnp.maximum, g)
    gsel = jnp.full(gmax.shape, MOE_GROUPS - 1, jnp.int32)
    for r in range(MOE_GROUPS - 2, -1, -1):
        gsel = jnp.where(g[r] == gmax, r, gsel)
    p_g = 1.0 / functools.reduce(jnp.add, [jnp.exp(gr - gmax) for gr in g])
    epg = MOE_EXPERTS_PER_GROUP
    e_in = lt[8 + (MOE_GROUPS - 1) * epg:8 + MOE_GROUPS * epg, :]
    for r in range(MOE_GROUPS - 2, -1, -1):
        e_in = jnp.where(gsel == r, lt[8 + r * epg:8 + (r + 1) * epg, :], e_in)
    sub = lax.broadcasted_iota(jnp.int32, (epg, tm), 0)
    key1 = _order_key(e_in)
    i1 = jnp.min(jnp.where(key1 == jnp.max(key1, axis=0, keepdims=True), sub, epg), axis=0, keepdims=True)
    v1 = jnp.max(e_in, axis=0, keepdims=True)
    key2 = jnp.where(sub == i1, INT32_MIN, key1)
    i2 = jnp.min(jnp.where(key2 == jnp.max(key2, axis=0, keepdims=True), sub, epg), axis=0, keepdims=True)
    v2 = jnp.max(jnp.where(sub == i1, -jnp.inf, e_in), axis=0, keepdims=True)
    t = jnp.exp(v2 - v1)
    w1 = p_g / (1.0 + t)
    w2 = p_g * t / (1.0 + t)
    id1 = gsel * epg + i1
    id2 = gsel * epg + i2
    ids_ref[...] = jnp.concatenate([id1, id2], axis=0)
    gate_ref[...] = jnp.concatenate([w1, w2, jnp.zeros((LANES - 2, tm), F32)], axis=0).T

    eidx = lax.broadcasted_iota(jnp.int32, (MOE_EXPERTS, tm), 0)
    oh1 = eidx == id1
    oh2 = eidx == id2
    onehot = (oh1 | oh2).astype(BF16)
    tr = lax.broadcasted_iota(jnp.int32, (tm, tm), 0)
    tc = lax.broadcasted_iota(jnp.int32, (tm, tm), 1)
    before = (tr < tc).astype(BF16)
    prefix = jnp.dot(onehot, before, preferred_element_type=F32) + base_ref[...]
    r1 = jnp.sum(jnp.where(oh1, prefix, 0.0), axis=0, keepdims=True)
    r2 = jnp.sum(jnp.where(oh2, prefix, 0.0), axis=0, keepdims=True)
    rank_ref[...] = jnp.concatenate([r1, r2], axis=0).astype(jnp.int32)
    base_ref[...] = base_ref[...] + jnp.sum(onehot.astype(F32), axis=1, keepdims=True)
    cnt_ref[...] = base_ref[...].astype(jnp.int32)


def _out_project_route(acts, weights, x, mod, norm_w, wr_t, br):
    bsz, s, d = x.shape
    tm = min(ROUTER_ROWS, s)
    n = bsz * s
    nt = s // tm
    n_in = len(acts)
    tok = lambda b, i: (0, b * nt + i)
    const = lambda b, i: (0, 0)
    in_specs = [pl.BlockSpec((1, tm, a.shape[2]), lambda b, i: (b, i, 0)) for a in acts]
    in_specs += [pl.BlockSpec(w.shape, const) for w in weights]
    in_specs += [pl.BlockSpec((1, tm, d), lambda b, i: (b, i, 0)),
                 pl.BlockSpec((1, 6, d), lambda b, i: (b, 0, 0)),
                 pl.BlockSpec((1, d), const),
                 pl.BlockSpec((ROUTER_LOGIT_ROWS, d), const),
                 pl.BlockSpec((ROUTER_LOGIT_ROWS, 1), const)]
    return pl.pallas_call(
        functools.partial(_out_route_kernel, n_in=n_in),
        grid=(bsz, nt),
        in_specs=in_specs,
        out_specs=[pl.BlockSpec((1, tm, d), lambda b, i: (b, i, 0)),
                   pl.BlockSpec((MOE_TOPK, tm), tok),
                   pl.BlockSpec((tm, LANES), lambda b, i: (b * nt + i, 0)),
                   pl.BlockSpec((MOE_TOPK, tm), tok),
                   pl.BlockSpec((MOE_EXPERTS, 1), const)],
        out_shape=[jax.ShapeDtypeStruct((bsz, s, d), F32),
                   jax.ShapeDtypeStruct((MOE_TOPK, n), jnp.int32),
                   jax.ShapeDtypeStruct((n, LANES), F32),
                   jax.ShapeDtypeStruct((MOE_TOPK, n), jnp.int32),
                   jax.ShapeDtypeStruct((MOE_EXPERTS, 1), jnp.int32)],
        scratch_shapes=[pltpu.VMEM((MOE_EXPERTS, 1), F32)],
        compiler_params=_cparams("arbitrary", "arbitrary"),
        name="out_project_route",
    )(*acts, *weights, x, mod, norm_w.reshape(1, d), wr_t, br)


def _dest_kernel(ids_ref, rank_ref, start_ref, o_ref):
    ids = ids_ref[...]
    tm = ids.shape[1]
    eidx = lax.broadcasted_iota(jnp.int32, (MOE_EXPERTS, tm), 0)
    rows = []
    for k in range(MOE_TOPK):
        base = jnp.sum(jnp.where(eidx == ids[k:k + 1, :], start_ref[...], 0), axis=0, keepdims=True)
        rows.append(base + rank_ref[k:k + 1, :])
    o_ref[...] = jnp.concatenate(rows, axis=0)


def _dest_rows(ids, rank, start):
    n = ids.shape[1]
    tm = min(2048, n)
    return pl.pallas_call(
        _dest_kernel,
        grid=(n // tm,),
        in_specs=[pl.BlockSpec((MOE_TOPK, tm), lambda i: (0, i)),
                  pl.BlockSpec((MOE_TOPK, tm), lambda i: (0, i)),
                  pl.BlockSpec((MOE_EXPERTS, 1), lambda i: (0, 0))],
        out_specs=pl.BlockSpec((MOE_TOPK, tm), lambda i: (0, i)),
        out_shape=jax.ShapeDtypeStruct((MOE_TOPK, n), jnp.int32),
        compiler_params=_cparams("arbitrary"),
        name="moe_dest_rows",
    )(ids, rank, start)


def _row_copy(src, dst, sem, src_row, dst_row, rt):
    return pltpu.make_async_copy(src.at[pl.ds(pl.multiple_of(src_row * rt, rt), rt)],
                                 dst.at[pl.ds(pl.multiple_of(dst_row * rt, rt), rt)], sem)


def _dispatch_kernel(pend_ref, padded_ref, dest_ref, x_ref, mod_ref, nx_ref, nmod_ref, nw_ref, xs_ref,
                     h_ref, zero_ref, sem, *, rt):
    tm = x_ref.shape[1]
    zrows = zero_ref.shape[0] // rt
    step = pl.program_id(0) * pl.num_programs(1) + pl.program_id(1)
    first = step == 0
    slot = step % 2

    @pl.when(first)
    def _():
        zero_ref[...] = jnp.zeros(zero_ref.shape, F32)
        for e in range(MOE_EXPERTS):
            @pl.when(padded_ref[e] > 0)
            def _():
                row0 = pl.multiple_of((pend_ref[e] - zrows) * rt, zrows * rt)
                cp = pltpu.make_async_copy(zero_ref, xs_ref.at[pl.ds(row0, zrows * rt)], sem)
                cp.start()
                cp.wait()

        def zero_tail(b, _):
            row0 = pl.multiple_of(b * zrows * rt, zrows * rt)
            cp = pltpu.make_async_copy(zero_ref, xs_ref.at[pl.ds(row0, zrows * rt)], sem)
            cp.start()
            cp.wait()
            return 0

        lax.fori_loop(pend_ref[MOE_EXPERTS - 1] // zrows, xs_ref.shape[0] // (zrows * rt), zero_tail, 0)

    @pl.when(first)
    def _():
        _to_row_tiles(h_ref, _moe_input(x_ref[0], nw_ref, mod_ref), (0,))

    def start(t, _):
        for k in range(MOE_TOPK):
            _row_copy(h_ref.at[slot], xs_ref, sem, t, dest_ref[k, t], rt).start(priority=k)
        return 0

    lax.fori_loop(0, tm, start, 0, unroll=DMA_ISSUE_UNROLL)

    @pl.when(step + 1 < pl.num_programs(0) * pl.num_programs(1))
    def _():
        _to_row_tiles(h_ref, _moe_input(nx_ref[0], nw_ref, nmod_ref), (1 - slot,))

    for k in range(MOE_TOPK):
        pltpu.make_async_copy(h_ref.at[slot], xs_ref.at[pl.ds(0, tm * rt)], sem).wait()


def _dispatch(x, norm_w, mod, dest, pend, padded, n_rows):
    bsz, s, d = x.shape
    rt = d // LANES
    tm = min(DISPATCH_ROWS, s)
    nt = s // tm
    last = bsz * nt - 1

    def nxt(b, i):
        f = jnp.minimum(b * nt + i + 1, last)
        return f // nt, f % nt

    return pl.pallas_call(
        functools.partial(_dispatch_kernel, rt=rt),
        grid=(bsz, nt),
        in_specs=[pl.BlockSpec(memory_space=pltpu.SMEM),
                  pl.BlockSpec(memory_space=pltpu.SMEM),
                  pl.BlockSpec((MOE_TOPK, tm), lambda b, i: (0, b * nt + i), memory_space=pltpu.SMEM),
                  pl.BlockSpec((1, tm, d), lambda b, i: (b, i, 0)),
                  pl.BlockSpec((1, 6, d), lambda b, i: (b, 0, 0)),
                  pl.BlockSpec((1, tm, d), lambda b, i: (*nxt(b, i), 0)),
                  pl.BlockSpec((1, 6, d), lambda b, i: (nxt(b, i)[0], 0, 0)),
                  pl.BlockSpec((1, d), lambda b, i: (0, 0))],
        out_specs=pl.BlockSpec(memory_space=pl.ANY),
        out_shape=jax.ShapeDtypeStruct((n_rows * rt, LANES), F32),
        scratch_shapes=[pltpu.VMEM((2, tm * rt, LANES), F32), pltpu.VMEM((MOE_BLOCK_ROWS * rt, LANES), F32),
                        pltpu.SemaphoreType.DMA(())],
        compiler_params=_cparams("arbitrary", "arbitrary"),
        name="moe_dispatch",
    )(pend, padded, dest, x, mod, x, mod, norm_w.reshape(1, d))


def _expert_kernel(blk_e_ref, n_used_ref, x_ref, w1_ref, w3_ref, w2_ref, o_ref, w1b_ref, w3b_ref, w2b_ref):
    i = pl.program_id(0)
    used = i < n_used_ref[0]
    new_expert = (i == 0) | (blk_e_ref[i] != blk_e_ref[jnp.maximum(i - 1, 0)])

    @pl.when(used & new_expert)
    def _():
        w1b_ref[...] = w1_ref[0, 0].astype(BF16)
        w3b_ref[...] = w3_ref[0, 0].astype(BF16)
        w2b_ref[...] = w2_ref[0, 0].astype(BF16)

    @pl.when(used)
    def _():
        d = w1b_ref.shape[0]
        rows = x_ref.shape[0] * LANES // d
        x = _from_row_tiles(x_ref, rows, d).astype(BF16)
        a = jnp.dot(x, w1b_ref[...], preferred_element_type=F32)
        g = jnp.dot(x, w3b_ref[...], preferred_element_type=F32)
        y = jnp.dot((a * jax.nn.sigmoid(a) * g).astype(BF16), w2b_ref[...], preferred_element_type=F32)
        _to_row_tiles(o_ref, y)

    @pl.when(jnp.logical_not(used))
    def _():
        o_ref[...] = jnp.zeros(o_ref.shape, F32)


def _expert_ffn(xs, blk_e, n_used, w1, w3, w2, layer):
    d, hid = w1.shape[2], w1.shape[3]
    rt = d // LANES
    n_rows = xs.shape[0] // rt
    tm = MOE_BLOCK_ROWS
    row = lambda i, be, nu: (jnp.minimum(i, nu[0] - 1), 0)
    wsel = lambda i, be, nu: (layer, be[i], 0, 0)
    return pl.pallas_call(
        _expert_kernel,
        grid_spec=pltpu.PrefetchScalarGridSpec(
            num_scalar_prefetch=2,
            grid=(n_rows // tm,),
            in_specs=[pl.BlockSpec((tm * rt, LANES), row),
                      pl.BlockSpec((1, 1, d, hid), wsel),
                      pl.BlockSpec((1, 1, d, hid), wsel),
                      pl.BlockSpec((1, 1, hid, d), wsel)],
            out_specs=pl.BlockSpec((tm * rt, LANES), lambda i, be, nu: (i, 0)),
            scratch_shapes=[pltpu.VMEM((d, hid), BF16), pltpu.VMEM((d, hid), BF16),
                            pltpu.VMEM((hid, d), BF16)]),
        out_shape=jax.ShapeDtypeStruct((n_rows * rt, LANES), F32),
        compiler_params=_cparams("arbitrary"),
        name="moe_expert_ffn",
    )(blk_e, n_used, xs, w1, w3, w2)


def _combine_kernel(dest_ref, next_dest_ref, ys_ref, gate_ref, x_ref, mod_ref, *refs, outs):
    buf_ref, sem = refs[-2:]
    o_ref = refs[-3 - len(outs)]
    tm, d = x_ref.shape[1], x_ref.shape[2]
    rt = d // LANES
    step = pl.program_id(0) * pl.num_programs(1) + pl.program_id(1)
    n_steps = pl.num_programs(0) * pl.num_programs(1)
    slot = step % 2

    def gather(rows_ref, s):
        def start(t, _):
            for k in range(MOE_TOPK):
                _row_copy(ys_ref, buf_ref.at[s, k], sem.at[s], rows_ref[k, t], t, rt).start(priority=k)
            return 0
        lax.fori_loop(0, tm, start, 0, unroll=DMA_ISSUE_UNROLL)

    @pl.when(step == 0)
    def _():
        gather(dest_ref, 0)

    @pl.when(step + 1 < n_steps)
    def _():
        gather(next_dest_ref, 1 - slot)

    for k in range(MOE_TOPK):
        pltpu.make_async_copy(ys_ref.at[pl.ds(0, tm * rt)], buf_ref.at[slot, k], sem.at[slot]).wait()
    g = gate_ref[...]
    y = (g[:, 0:1] * _from_row_tiles(buf_ref, tm, d, (slot, 0))
         + g[:, 1:2] * _from_row_tiles(buf_ref, tm, d, (slot, 1)))
    x = x_ref[0] + mod_ref[0, 5:6, :] * y
    o_ref[0] = x
    if outs:
        nw_ref, next_mod_ref, w_ref = refs[0:3]
        n_norm = len(refs) - 6 - len(outs)
        _project_rows(x, nw_ref, next_mod_ref, w_ref, refs[3:3 + n_norm], refs[-2 - len(outs):-2], outs)


def _combine(ys, dest, gates, x, mod, proj=None):
    bsz, s, d = x.shape
    tm = min(COMBINE_ROWS, s)
    nt = s // tm
    last = bsz * nt - 1
    tile = lambda b, i: (b, i, 0)
    const = lambda b, i: (0, 0)
    in_specs = [pl.BlockSpec((MOE_TOPK, tm), lambda b, i: (0, b * nt + i), memory_space=pltpu.SMEM),
                pl.BlockSpec((MOE_TOPK, tm), lambda b, i: (0, jnp.minimum(b * nt + i + 1, last)),
                             memory_space=pltpu.SMEM),
                pl.BlockSpec(memory_space=pl.ANY),
                pl.BlockSpec((tm, LANES), lambda b, i: (b * nt + i, 0)),
                pl.BlockSpec((1, tm, d), tile),
                pl.BlockSpec((1, 6, d), lambda b, i: (b, 0, 0))]
    args = [dest, dest, ys, gates, x, mod]
    out_specs = [pl.BlockSpec((1, tm, d), tile)]
    out_shape = [jax.ShapeDtypeStruct((bsz, s, d), F32)]
    outs = ()
    if proj is not None:
        norm_w, next_mod, w_bf16, outs, dtypes, head_norms = proj
        in_specs += [pl.BlockSpec((1, d), const), pl.BlockSpec((1, 6, d), lambda b, i: (b, 0, 0)),
                     pl.BlockSpec(w_bf16.shape, const)]
        in_specs += [pl.BlockSpec((1, hn.shape[0]), const) for hn in head_norms]
        args += [norm_w.reshape(1, d), next_mod, w_bf16] + [hn.reshape(1, -1) for hn in head_norms]
        out_specs += [pl.BlockSpec((1, tm, o[0]), tile) for o in outs]
        out_shape += [jax.ShapeDtypeStruct((bsz, s, o[0]), dt) for o, dt in zip(outs, dtypes)]
    res = pl.pallas_call(
        functools.partial(_combine_kernel, outs=tuple(outs)),
        grid=(bsz, nt),
        in_specs=in_specs,
        out_specs=out_specs,
        out_shape=out_shape,
        scratch_shapes=[pltpu.VMEM((2, MOE_TOPK, tm * d // LANES, LANES), F32), pltpu.SemaphoreType.DMA((2,))],
        compiler_params=_cparams("arbitrary", "arbitrary"),
        name="moe_combine" if proj is None else "moe_combine_project",
    )(*args)
    return res[0] if proj is None else res


def _mixer_out_moe_residual(acts, w_outs, x, norm_w, mod, w_group, b_group, w_expert, b_expert, w1, w3, w2, layer,
                            next_proj):
    bsz, s, d = x.shape
    n = bsz * s
    tm = MOE_BLOCK_ROWS
    wr_t = jnp.zeros((ROUTER_LOGIT_ROWS, d), F32).at[0:MOE_GROUPS].set(w_group.T).at[8:].set(w_expert.T)
    br = jnp.zeros((ROUTER_LOGIT_ROWS, 1), F32).at[0:MOE_GROUPS, 0].set(b_group).at[8:, 0].set(b_expert)
    x_mid, ids, gates, rank, counts = _out_project_route(acts, w_outs, x, mod, norm_w, wr_t.astype(BF16), br)
    counts = counts[:, 0]
    padded = ((counts + tm - 1) // tm) * tm
    pend = jnp.cumsum(padded).astype(jnp.int32)
    start = pend - padded
    n_rows = ((n * MOE_TOPK + tm - 1) // tm) * tm + MOE_EXPERTS * tm
    blk_start = jnp.arange(n_rows // tm, dtype=jnp.int32) * tm
    blk_e = jnp.minimum(jnp.sum((pend[None, :] <= blk_start[:, None]).astype(jnp.int32), axis=1),
                        MOE_EXPERTS - 1)
    n_used = pend[-1:] // tm
    dest = _dest_rows(ids, rank, start.reshape(MOE_EXPERTS, 1))
    xs = _dispatch(x_mid, norm_w, mod, dest, pend, padded, n_rows)
    ys = _expert_ffn(xs, blk_e, n_used, w1, w3, w2, layer)
    return _combine(ys, dest, gates, x_mid, mod, next_proj)


def _even_proj_spec(w_in, q_norm, k_norm):
    d, n_in = w_in.shape
    aq = A_HEADS * A_HEAD_DIM
    akv = A_KV_HEADS * A_HEAD_DIM
    bk = B_HEADS * B_KEY_DIM
    bv = B_HEADS * B_VAL_DIM
    w_pad = jnp.zeros((d, n_in - B_GATE_RANK + LANES), F32).at[:, :n_in].set(w_in).astype(BF16)
    outs = ((aq, A_HEAD_DIM, 0, A_HEAD_DIM ** -0.5 * LOG2E), (akv, A_HEAD_DIM, 1, 1.0), (akv, 0, 0, 1.0),
            (2 * bk, 0, 0, 1.0), (bv, 0, 0, 1.0), (bv, 0, 0, 1.0), (LANES, 0, 0, 1.0))
    dtypes = (BF16, BF16, BF16, F32, BF16, F32, F32)
    return w_pad, outs, dtypes, (q_norm, k_norm)


def _odd_proj_spec(w_in, q_norm, k_norm):
    mix = C_HEADS * C_HEAD_DIM
    outs = ((mix, C_HEAD_DIM, 0, C_HEAD_DIM ** -0.5 * LOG2E), (mix, C_HEAD_DIM, 1, 1.0), (mix, 0, 0, 1.0))
    return w_in.astype(BF16), outs, (BF16, BF16, BF16), (q_norm, k_norm)


def _even_mixer(groups, w_out, sinks, gate_up, gate_bias, out_norm, band_bias):
    qa, ka, va, qk, vb, rb, ab = groups
    aq = A_HEADS * A_HEAD_DIM
    bk = B_HEADS * B_KEY_DIM
    oa = _swa_attention(qa, ka, va, sinks, band_bias)
    gu_pad = jnp.zeros((LANES, bk), F32).at[:B_GATE_RANK].set(gate_up).astype(BF16)
    ob = _gla(qk, vb, rb, ab, gu_pad, gate_bias, out_norm)
    w_out = w_out.astype(BF16)
    return [oa, ob], [w_out[:aq], w_out[aq:]]


def _odd_mixer(groups, w_out, bias_tiles):
    q, k, v = groups
    return [_moba(q, k, v, bias_tiles)], [w_out.astype(BF16)]


def kernel(x, c, rel_bias, ada_w, ada_b, norm1_w, norm2_w, even_w_in, even_w_out, a_q_norm, a_k_norm, a_sinks, b_gate_up, b_gate_bias, b_out_norm, odd_w_in, odd_w_out, c_q_norm, c_k_norm, moe_w_group, moe_b_group, moe_w_expert, moe_b_expert, moe_w1, moe_w3, moe_w2):
    depth = ada_w.shape[0]
    bsz, _, d = x.shape
    moba_bias, band_bias = _bias_tiles(rel_bias)
    mod_all = _adaln(c, ada_w, ada_b).reshape(depth, bsz, 6, d)

    def proj_spec(layer):
        j = layer // 2
        if layer % 2 == 0:
            return _even_proj_spec(even_w_in[j], a_q_norm[j], a_k_norm[j])
        return _odd_proj_spec(odd_w_in[j], c_q_norm[j], c_k_norm[j])

    w_in, outs, dtypes, head_norms = proj_spec(0)
    groups = _norm_mod_project(x, norm1_w[0], mod_all[0], w_in, outs, dtypes, head_norms)
    for layer in range(depth):
        mod = mod_all[layer]
        j = layer // 2
        if layer % 2 == 0:
            acts, w_outs = _even_mixer(groups, even_w_out[j], a_sinks[j], b_gate_up[j], b_gate_bias[j],
                                       b_out_norm[j], band_bias)
        else:
            acts, w_outs = _odd_mixer(groups, odd_w_out[j], moba_bias)
        next_proj = None
        if layer + 1 < depth:
            w_in, outs, dtypes, head_norms = proj_spec(layer + 1)
            next_proj = (norm1_w[layer + 1], mod_all[layer + 1], w_in, outs, dtypes, head_norms)
        res = _mixer_out_moe_residual(acts, w_outs, x, norm2_w[layer], mod, moe_w_group[layer],
                                      moe_b_group[layer], moe_w_expert[layer], moe_b_expert[layer],
                                      moe_w1, moe_w3, moe_w2, layer, next_proj)
        x, groups = (res, None) if next_proj is None else (res[0], res[1:])
    return x
```

```python
import functools
import math

import jax
import jax.numpy as jnp
import numpy as np
from jax import lax
from jax.experimental import pallas as pl
from jax.experimental.pallas import tpu as pltpu

RMS_EPS = 1e-6
A_HEADS, A_KV_HEADS, A_HEAD_DIM, A_WINDOW = 8, 2, 64, 128
B_HEADS, B_KEY_DIM, B_VAL_DIM, B_GATE_RANK, B_GATE_TAU, B_CHUNK = 4, 64, 128, 16, 16.0, 64
C_HEADS, C_HEAD_DIM, C_BLOCK, C_TOPK = 8, 128, 256, 3
REL_BUCKETS, REL_MAX_DIST, REL_HEADS = 32, 128, 8
MOE_GROUPS, MOE_EXPERTS_PER_GROUP, MOE_TOPK = 4, 8, 2
MOE_EXPERTS = MOE_GROUPS * MOE_EXPERTS_PER_GROUP

LANES = 128
V7X_VMEM_LIMIT_BYTES = 56 * 1024 * 1024
NEG_BIG = -1e30
INT32_MIN = -2 ** 31
PROJ_ROWS = 512
GLA_ROWS = 512
ROUTER_ROWS = 512
MOE_BLOCK_ROWS = 512
DISPATCH_ROWS = 1024
COMBINE_ROWS = 512
DMA_ISSUE_UNROLL = 8
ROUTER_LOGIT_ROWS = 8 + MOE_EXPERTS
MOBA_GROUP = 2
MOBA_HEADS_PER_STEP = 4
LOG2E = math.log2(math.e)
SWA_VT_ROWS = A_HEAD_DIM + 16
MOBA_VT_ROWS = C_HEAD_DIM + 16

F32 = jnp.float32
BF16 = jnp.bfloat16


def _cparams(*sem):
    return pltpu.CompilerParams(dimension_semantics=sem, vmem_limit_bytes=V7X_VMEM_LIMIT_BYTES)


def _rel_bucket_np(dist):
    exact = REL_BUCKETS // 2
    d = np.maximum(dist, 0)
    logd = np.log(np.maximum(d, 1).astype(np.float64) / exact) / math.log(REL_MAX_DIST / exact)
    far = np.minimum(exact + (logd * (REL_BUCKETS - exact)).astype(np.int64), REL_BUCKETS - 1)
    return np.where(d < exact, d, far).astype(np.int32)


def _bucket_tiles():
    li = np.arange(C_BLOCK)
    own = _rel_bucket_np(li[None, :] - li[:, None])
    prev = _rel_bucket_np(li[None, :] - li[:, None] + C_BLOCK)
    band = _rel_bucket_np(np.arange(A_WINDOW)[None, :] + A_WINDOW - np.arange(2 * A_WINDOW)[:, None])
    return np.concatenate([own, prev], axis=0), band


def _rms_rows(x, w):
    return x * lax.rsqrt(jnp.mean(x * x, axis=-1, keepdims=True) + RMS_EPS) * w


def _dot(a, b):
    return jnp.dot(a.astype(BF16), b.astype(BF16), preferred_element_type=F32)


def _dot_nt(a, b):
    return lax.dot_general(a.astype(BF16), b.astype(BF16), (((1,), (1,)), ((), ())),
                           preferred_element_type=F32)


def _order_key(x):
    bits = lax.bitcast_convert_type(x, jnp.int32)
    return jnp.where(bits < 0, bits ^ jnp.int32(0x7FFFFFFF), bits)


def _to_row_tiles(ref, x, lead=()):
    rows, d = x.shape
    chunks = d // LANES
    for c in range(chunks):
        ref[lead + (pl.ds(c, rows, stride=chunks), slice(None))] = x[:, c * LANES:(c + 1) * LANES]


def _from_row_tiles(ref, rows, d, lead=()):
    chunks = d // LANES
    return jnp.concatenate([ref[lead + (pl.ds(c, rows, stride=chunks), slice(None))] for c in range(chunks)],
                           axis=1)


def _dot_tn(a, b):
    return lax.dot_general(a.astype(BF16), b.astype(BF16), (((0,), (0,)), ((), ())),
                           preferred_element_type=F32)


def _bias_kernel(rb_ref, bkt_ref, o_ref, *, relative_to_last):
    h = pl.program_id(0)
    bkt = bkt_ref[...]
    acc = jnp.zeros(bkt.shape, F32)
    for b in range(REL_BUCKETS):
        acc = jnp.where(bkt == b, rb_ref[b, h], acc)
    if relative_to_last:
        acc = acc - rb_ref[REL_BUCKETS - 1, h]
    o_ref[0] = acc * LOG2E


def _bias_tile(rel_bias, bkt, relative_to_last):
    rows, cols = bkt.shape
    return pl.pallas_call(
        functools.partial(_bias_kernel, relative_to_last=relative_to_last),
        grid=(REL_HEADS,),
        in_specs=[pl.BlockSpec(memory_space=pltpu.SMEM),
                  pl.BlockSpec((rows, cols), lambda h: (0, 0))],
        out_specs=pl.BlockSpec((1, rows, cols), lambda h: (h, 0, 0)),
        out_shape=jax.ShapeDtypeStruct((REL_HEADS, rows, cols), F32),
        compiler_params=_cparams("arbitrary"),
        name="rel_bias_tiles",
    )(rel_bias, jnp.asarray(bkt))


def _bias_tiles(rel_bias):
    moba_bkt, band_bkt = _bucket_tiles()
    return _bias_tile(rel_bias, moba_bkt, True), _bias_tile(rel_bias, band_bkt, False)


def _adaln_kernel(c_ref, w_ref, b_ref, o_ref):
    c = c_ref[...]
    cond = c * jax.nn.sigmoid(c)
    o_ref[0] = _dot(cond, w_ref[0]) + b_ref[0]


def _adaln(c, ada_w, ada_b):
    depth, d, n6 = ada_w.shape
    bsz = c.shape[0]
    tn = 1536 if n6 % 1536 == 0 else n6
    return pl.pallas_call(
        _adaln_kernel,
        grid=(depth, n6 // tn),
        in_specs=[pl.BlockSpec((bsz, d), lambda l, j: (0, 0)),
                  pl.BlockSpec((1, d, tn), lambda l, j: (l, 0, j)),
                  pl.BlockSpec((1, 1, tn), lambda l, j: (l, 0, j))],
        out_specs=pl.BlockSpec((1, bsz, tn), lambda l, j: (l, 0, j)),
        out_shape=jax.ShapeDtypeStruct((depth, bsz, n6), F32),
        compiler_params=_cparams("arbitrary", "arbitrary"),
        name="adaln_mod",
    )(c, ada_w, ada_b.reshape(depth, 1, n6))


def _head_rms(y, w_ref, hd, post_scale):
    cols = y.shape[1]
    lane = lax.broadcasted_iota(jnp.int32, (1, LANES), 1)
    w = w_ref[...] * post_scale
    out = []
    for c in range(cols // LANES):
        t = y[:, c * LANES:(c + 1) * LANES]
        sq = t * t
        if hd == LANES:
            r = lax.rsqrt(jnp.sum(sq, axis=-1, keepdims=True) * (1.0 / hd) + RMS_EPS)
            out.append(t * r * w)
        else:
            low = lane < hd
            s_lo = jnp.sum(jnp.where(low, sq, 0.0), axis=-1, keepdims=True)
            s_hi = jnp.sum(jnp.where(low, 0.0, sq), axis=-1, keepdims=True)
            r = lax.rsqrt(jnp.where(low, s_lo, s_hi) * (1.0 / hd) + RMS_EPS)
            out.append(t * r * jnp.concatenate([w, w], axis=1))
    return jnp.concatenate(out, axis=1) if len(out) > 1 else out[0]


def _project_rows(x, nw_ref, mod_ref, w_ref, norm_refs, o_refs, outs):
    h = _rms_rows(x, nw_ref[...])
    h = h * (1.0 + mod_ref[0, 1:2, :]) + mod_ref[0, 0:1, :]
    y = _dot(h, w_ref[...])
    off = 0
    for o_ref, (wd, hd, widx, post) in zip(o_refs, outs):
        t = y[:, off:off + wd]
        if hd:
            t = _head_rms(t, norm_refs[widx], hd, post)
        o_ref[0] = t.astype(o_ref.dtype)
        off += wd


def _proj_kernel(x_ref, nw_ref, mod_ref, w_ref, *refs, outs):
    n_norm = len(refs) - len(outs)
    _project_rows(x_ref[0], nw_ref, mod_ref, w_ref, refs[:n_norm], refs[n_norm:], outs)


def _norm_mod_project(x, norm_w, mod, w_bf16, outs, dtypes, head_norms):
    bsz, s, d = x.shape
    tm = min(PROJ_ROWS, s)
    n = w_bf16.shape[1]
    widths = [o[0] for o in outs]
    assert sum(widths) == n and all(wd % LANES == 0 for wd in widths)
    return pl.pallas_call(
        functools.partial(_proj_kernel, outs=tuple(outs)),
        grid=(bsz, s // tm),
        in_specs=[pl.BlockSpec((1, tm, d), lambda b, i: (b, i, 0)),
                  pl.BlockSpec((1, d), lambda b, i: (0, 0)),
                  pl.BlockSpec((1, 6, d), lambda b, i: (b, 0, 0)),
                  pl.BlockSpec((d, n), lambda b, i: (0, 0))]
                 + [pl.BlockSpec((1, hn.shape[0]), lambda b, i: (0, 0)) for hn in head_norms],
        out_specs=[pl.BlockSpec((1, tm, wd), lambda b, i: (b, i, 0)) for wd in widths],
        out_shape=[jax.ShapeDtypeStruct((bsz, s, wd), dt) for wd, dt in zip(widths, dtypes)],
        compiler_params=_cparams("arbitrary", "arbitrary"),
        name="norm_mod_project",
    )(x, norm_w.reshape(1, d), mod, w_bf16, *[hn.reshape(1, -1) for hn in head_norms])


def _swa_kernel(sink_ref, q_ref, kc_ref, kp_ref, vc_ref, vp_ref, bias_ref, o_ref):
    n = pl.program_id(1)
    w = A_WINDOW
    hd = A_HEAD_DIM
    group = A_HEADS // A_KV_HEADS
    ones_rows = (lax.broadcasted_iota(jnp.int32, (SWA_VT_ROWS - hd, 2 * w), 0) == 0).astype(BF16)
    key = lax.broadcasted_iota(jnp.int32, (2 * w, group * w), 0)
    qry = lax.broadcasted_iota(jnp.int32, (2 * w, group * w), 1) % w
    dist = qry + w - key
    in_window = (dist >= 0) & (dist < w)
    masks = [in_window & ((n > 0) | (key >= w)), in_window]
    head_of_lane = lax.broadcasted_iota(jnp.int32, (1, group * w), 1) // w
    k_bands = [jnp.concatenate([kp_ref[0], kc_ref[0, 0:w, :]], axis=0), kc_ref[0]]
    v_bands = [jnp.concatenate([vp_ref[0], vc_ref[0, 0:w, :]], axis=0), vc_ref[0]]
    v_ts = [v.astype(F32).T for v in v_bands]
    sinks = []
    for kv in range(A_KV_HEADS):
        h0 = kv * group
        sink = jnp.full((1, group * w), sink_ref[h0 + group - 1] * LOG2E, F32)
        for g in range(group - 2, -1, -1):
            sink = jnp.where(head_of_lane == g, sink_ref[h0 + g] * LOG2E, sink)
        sinks.append(sink)
    chains = [(blk, kv) for blk in range(2) for kv in range(A_KV_HEADS)]
    logits, vts = [], []
    for blk, kv in chains:
        h0 = kv * group
        k_g = k_bands[blk][:, kv * hd:(kv + 1) * hd]
        vts.append(jnp.concatenate([v_ts[blk][kv * hd:(kv + 1) * hd, :].astype(BF16), ones_rows], axis=0))
        q_g = jnp.concatenate([q_ref[0, blk * w:(blk + 1) * w, (h0 + g) * hd:(h0 + g + 1) * hd]
                               for g in range(group)], axis=0)
        bias = jnp.concatenate([bias_ref[h0 + g] for g in range(group)], axis=1)
        logits.append(jnp.where(masks[blk], _dot_nt(k_g, q_g) + bias, NEG_BIG))
    maxima = [jnp.maximum(jnp.max(lg, axis=0, keepdims=True), sinks[kv]) for lg, (_, kv) in zip(logits, chains)]
    accs = [jnp.dot(vt_g, jnp.exp2(lg - m).astype(BF16), preferred_element_type=F32)
            for vt_g, lg, m in zip(vts, logits, maxima)]
    for (blk, kv), acc, m in zip(chains, accs, maxima):
        h0 = kv * group
        o_t = acc[0:hd] / (acc[hd:hd + 1] + jnp.exp2(sinks[kv] - m))
        for pair in range(group // 2):
            two = jnp.concatenate([o_t[:, (2 * pair) * w:(2 * pair + 1) * w],
                                   o_t[:, (2 * pair + 1) * w:(2 * pair + 2) * w]], axis=0)
            c0 = (h0 + 2 * pair) * hd
            o_ref[0, blk * w:(blk + 1) * w, c0:c0 + 2 * hd] = two.T.astype(o_ref.dtype)


def _swa_attention(qa, ka, va, sinks, band_bias):
    bsz, s, _ = qa.shape
    w = A_WINDOW
    assert s % (2 * w) == 0
    kvw = A_KV_HEADS * A_HEAD_DIM
    cur = lambda b, n: (b, n, 0)
    prev = lambda b, n: (b, jnp.maximum(2 * n - 1, 0), 0)
    return pl.pallas_call(
        _swa_kernel,
        grid=(bsz, s // (2 * w)),
        in_specs=[pl.BlockSpec(memory_space=pltpu.SMEM),
                  pl.BlockSpec((1, 2 * w, A_HEADS * A_HEAD_DIM), cur),
                  pl.BlockSpec((1, 2 * w, kvw), cur),
                  pl.BlockSpec((1, w, kvw), prev),
                  pl.BlockSpec((1, 2 * w, kvw), cur),
                  pl.BlockSpec((1, w, kvw), prev),
                  pl.BlockSpec((REL_HEADS, 2 * w, w), lambda b, n: (0, 0, 0))],
        out_specs=pl.BlockSpec((1, 2 * w, A_HEADS * A_HEAD_DIM), cur),
        out_shape=jax.ShapeDtypeStruct((bsz, s, A_HEADS * A_HEAD_DIM), BF16),
        compiler_params=_cparams("arbitrary", "arbitrary"),
        name="swa_sink_attention",
    )(sinks, qa, ka, ka, va, va, band_bias)


def _gla_kernel(qk_ref, v_ref, r_ref, ab_ref, gu_ref, gb_ref, on_ref, o_ref, state_ref):
    s_idx = pl.program_id(1)
    c_len = B_CHUNK
    dk, dv = B_KEY_DIM, B_VAL_DIM
    hk = B_HEADS * dk

    @pl.when(s_idx == 0)
    def _():
        state_ref[...] = jnp.zeros(state_ref.shape, F32)

    z = _dot(ab_ref[0], gu_ref[...]) + gb_ref[...]
    log_a = (jnp.minimum(z, 0.0) - jnp.log(1.0 + jnp.exp(-jnp.abs(z)))) / B_GATE_TAU
    rows = qk_ref.shape[1]
    chunks = range(rows // c_len)
    heads = range(B_HEADS)
    ri = lax.broadcasted_iota(jnp.int32, (c_len, c_len), 0)
    ci = lax.broadcasted_iota(jnp.int32, (c_len, c_len), 1)
    tril = ri >= ci
    tri = tril.astype(BF16)
    g_hi = log_a.astype(BF16)
    g_lo = (log_a - g_hi.astype(F32)).astype(BF16)
    b_c = [jnp.dot(tri, g_hi[c * c_len:(c + 1) * c_len], preferred_element_type=F32)
           + jnp.dot(tri, g_lo[c * c_len:(c + 1) * c_len], preferred_element_type=F32) for c in chunks]
    b = jnp.concatenate(b_c, axis=0)
    last_c = [bc[c_len - 1:c_len, :] for bc in b_c]
    b_last = jnp.concatenate([jnp.broadcast_to(l, (c_len, hk)) for l in last_c], axis=0)
    q = qk_ref[0, :, 0:hk] * (dk ** -0.5)
    k = qk_ref[0, :, hk:2 * hk]
    q_dec = (q * jnp.exp(b)).astype(BF16)
    k_dec = (k * jnp.exp(-b)).astype(BF16)
    k_upd = (k * jnp.exp(b_last - b)).astype(BF16)
    v = v_ref[0]

    def rows_of(x, c, cols):
        return x[c * c_len:(c + 1) * c_len, cols]

    att = [[jnp.where(tril, _dot_nt(rows_of(q_dec, c, slice(h * dk, (h + 1) * dk)),
                                    rows_of(k_dec, c, slice(h * dk, (h + 1) * dk))), 0.0).astype(BF16)
            for h in heads] for c in chunks]
    upd = [[_dot_tn(rows_of(v, c, slice(h * dv, (h + 1) * dv)), rows_of(k_upd, c, slice(h * dk, (h + 1) * dk)))
            for h in heads] for c in chunks]
    state_in = []
    st = [state_ref[h] for h in heads]
    for c in chunks:
        state_in.append(st)
        decay = jnp.exp(last_c[c])
        st = [st[h] * decay[:, h * dk:(h + 1) * dk] + upd[c][h] for h in heads]
    for h in heads:
        state_ref[h] = st[h]
    for c in chunks:
        outs = []
        for h in heads:
            o_h = (_dot(att[c][h], rows_of(v, c, slice(h * dv, (h + 1) * dv)))
                   + _dot_nt(rows_of(q_dec, c, slice(h * dk, (h + 1) * dk)), state_in[c][h]))
            o_h = _rms_rows(o_h, on_ref[...])
            r_h = r_ref[0, c * c_len:(c + 1) * c_len, h * dv:(h + 1) * dv]
            outs.append(o_h * (r_h * jax.nn.sigmoid(r_h)))
        o_ref[0, c * c_len:(c + 1) * c_len, :] = jnp.concatenate(outs, axis=-1).astype(o_ref.dtype)


def _gla(qk, vb, rb, ab, gate_up_pad, gate_bias, out_norm):
    bsz, s, _ = qk.shape
    tm = min(GLA_ROWS, s)
    hv = B_HEADS * B_VAL_DIM
    hk = B_HEADS * B_KEY_DIM
    blk = lambda b, i: (b, i, 0)
    const = lambda b, i: (0, 0)
    return pl.pallas_call(
        _gla_kernel,
        grid=(bsz, s // tm),
        in_specs=[pl.BlockSpec((1, tm, 2 * hk), blk),
                  pl.BlockSpec((1, tm, hv), blk),
                  pl.BlockSpec((1, tm, hv), blk),
                  pl.BlockSpec((1, tm, LANES), blk),
                  pl.BlockSpec((LANES, hk), const),
                  pl.BlockSpec((1, hk), const),
                  pl.BlockSpec((1, B_VAL_DIM), const)],
        out_specs=pl.BlockSpec((1, tm, hv), blk),
        out_shape=jax.ShapeDtypeStruct((bsz, s, hv), BF16),
        scratch_shapes=[pltpu.VMEM((B_HEADS, B_VAL_DIM, B_KEY_DIM), F32)],
        compiler_params=_cparams("arbitrary", "arbitrary"),
        name="gated_linear_attention",
    )(qk, vb, rb, ab, gate_up_pad, gate_bias.reshape(1, hk), out_norm.reshape(1, -1))


def _moba_kernel(q_ref, k_ref, v_ref, bias_ref, o_ref, vt_ref, kmean_ref, pick_ref):
    i = pl.program_id(2)
    blk = C_BLOCK
    hd = C_HEAD_DIM
    nb = k_ref.shape[1] // blk
    grp = MOBA_GROUP
    heads = range(MOBA_HEADS_PER_STEP)

    def lanes(hh):
        return slice(hh * hd, (hh + 1) * hd)

    @pl.when(i == 0)
    def _():
        ones_rows = (lax.broadcasted_iota(jnp.int32, (MOBA_VT_ROWS - hd, blk), 0) == 0).astype(BF16)
        for hh in heads:
            kn = k_ref[0, :, lanes(hh)].astype(F32)
            kmean_ref[hh] = jnp.mean(kn.reshape(nb, blk, hd), axis=1)
            for j in range(nb):
                vt = v_ref[0, j * blk:(j + 1) * blk, lanes(hh)].astype(F32).T.astype(BF16)
                vt_ref[hh, j] = jnp.concatenate([vt, ones_rows], axis=0)

    def logits(hh, q_h, j):
        j0 = pl.multiple_of(j * blk, blk)
        return _dot_nt(k_ref[0, pl.ds(j0, blk), lanes(hh)], q_h)

    def col_max(lg, picked):
        return jnp.where(picked, jnp.max(lg, axis=0, keepdims=True), NEG_BIG)

    def weighted_v(hh, j, lg, m, picked):
        p = jnp.exp2(lg - m).astype(BF16)
        return jnp.where(picked, jnp.dot(vt_ref[hh, j], p, preferred_element_type=F32), 0.0)

    def softmax_step(ms, accs, tiles):
        m_new = []
        for hh in heads:
            m_h = ms[hh]
            for _, lg, picked in tiles[hh]:
                m_h = jnp.maximum(m_h, col_max(lg, picked))
            m_new.append(m_h)
        out = []
        for hh in heads:
            acc = accs[hh] * jnp.exp2(ms[hh] - m_new[hh])
            for j, lg, picked in tiles[hh]:
                acc = acc + weighted_v(hh, j, lg, m_new[hh], picked)
            out.append(acc)
        return tuple(m_new), tuple(out)

    jrow = lax.broadcasted_iota(jnp.int32, (nb, blk), 0)
    key = lax.broadcasted_iota(jnp.int32, (blk, blk), 0)
    qry = lax.broadcasted_iota(jnp.int32, (blk, blk), 1)
    j_prev = jnp.maximum(i - 1, 0)
    qs = []
    for hh in heads:
        q_h = q_ref[0, :, lanes(hh)]
        qs.append(q_h)
        key_h = jnp.where(jrow < i, _order_key(_dot_nt(kmean_ref[hh], q_h)), INT32_MIN)
        chosen = jnp.zeros((nb, blk), F32)
        for _ in range(C_TOPK):
            best = jnp.max(key_h, axis=0, keepdims=True)
            arg = jnp.min(jnp.where(key_h == best, jrow, nb), axis=0, keepdims=True)
            hit = jrow == arg
            chosen = jnp.where(hit, 1.0, chosen)
            key_h = jnp.where(hit, INT32_MIN, key_h)
        pick_ref[hh] = jnp.where(jrow < i, chosen, 0.0)

    first = []
    for hh in heads:
        lg_own = jnp.where(key <= qry, logits(hh, qs[hh], i) + bias_ref[hh, 0:blk, :], NEG_BIG)
        lg_prev = logits(hh, qs[hh], j_prev) + bias_ref[hh, blk:2 * blk, :]
        pick_prev = pick_ref[hh, pl.ds(j_prev, 1), :] > 0.5
        first.append([(i, lg_own, True), (j_prev, lg_prev, pick_prev)])
    start = (tuple(jnp.full((1, blk), NEG_BIG, F32) for _ in heads),
             tuple(jnp.zeros((MOBA_VT_ROWS, blk), F32) for _ in heads))
    carry0 = softmax_step(start[0], start[1], first)

    n_far = jnp.maximum(i - 1, 0)

    def body(g, carry):
        tiles = []
        for hh in heads:
            row = []
            for u in range(grp):
                j = g * grp + u
                jc = jnp.minimum(j, nb - 1)
                picked = (pick_ref[hh, pl.ds(jc, 1), :] > 0.5) & (j < n_far)
                row.append((jc, logits(hh, qs[hh], jc), picked))
            tiles.append(row)
        return softmax_step(carry[0], carry[1], tiles)

    _, accs = lax.fori_loop(0, (n_far + grp - 1) // grp, body, carry0)
    for hh in heads:
        o_ref[0, :, lanes(hh)] = (accs[hh][0:hd] / accs[hh][hd:hd + 1]).T.astype(o_ref.dtype)


def _moba(q, k, v, bias_tiles):
    bsz, s, _ = q.shape
    blk, hd = C_BLOCK, C_HEAD_DIM
    assert s % blk == 0
    nb = s // blk
    hb = MOBA_HEADS_PER_STEP
    assert C_HEADS % hb == 0
    return pl.pallas_call(
        _moba_kernel,
        grid=(bsz, C_HEADS // hb, nb),
        in_specs=[pl.BlockSpec((1, blk, hb * hd), lambda b, h, i: (b, i, h)),
                  pl.BlockSpec((1, s, hb * hd), lambda b, h, i: (b, 0, h)),
                  pl.BlockSpec((1, s, hb * hd), lambda b, h, i: (b, 0, h)),
                  pl.BlockSpec((hb, 2 * blk, blk), lambda b, h, i: (h, 0, 0))],
        out_specs=pl.BlockSpec((1, blk, hb * hd), lambda b, h, i: (b, i, h)),
        out_shape=jax.ShapeDtypeStruct((bsz, s, C_HEADS * hd), BF16),
        scratch_shapes=[pltpu.VMEM((hb, nb, MOBA_VT_ROWS, blk), BF16),
                        pltpu.VMEM((hb, nb, hd), F32), pltpu.VMEM((hb, nb, blk), F32)],
        compiler_params=_cparams("arbitrary", "arbitrary", "arbitrary"),
        name="moba_attention",
    )(q, k, v, bias_tiles)


def _moe_input(x, nw_ref, mod_ref):
    h = _rms_rows(x, nw_ref[...])
    return h * (1.0 + mod_ref[0, 4:5, :]) + mod_ref[0, 3:4, :]


def _out_route_kernel(*refs, n_in):
    a_refs = refs[:n_in]
    w_refs = refs[n_in:2 * n_in]
    (x_ref, mod_ref, nw_ref, wr_ref, br_ref,
     xo_ref, ids_ref, gate_ref, rank_ref, cnt_ref, base_ref) = refs[2 * n_in:]
    first = (pl.program_id(0) == 0) & (pl.program_id(1) == 0)

    @pl.when(first)
    def _():
        base_ref[...] = jnp.zeros(base_ref.shape, F32)

    y = _dot(a_refs[0][0], w_refs[0][...])
    for a_ref, w_ref in zip(a_refs[1:], w_refs[1:]):
        y = y + _dot(a_ref[0], w_ref[...])
    x = x_ref[0] + mod_ref[0, 2:3, :] * y
    xo_ref[0] = x

    tm = x.shape[0]
    lt = _dot_nt(wr_ref[...], _moe_input(x, nw_ref, mod_ref)) + br_ref[...]
    g = [lt[r:r + 1, :] for r in range(MOE_GROUPS)]
    gmax = functools.reduce(jnp.maximum, g)
    gsel = jnp.full(gmax.shape, MOE_GROUPS - 1, jnp.int32)
    for r in range(MOE_GROUPS - 2, -1, -1):
        gsel = jnp.where(g[r] == gmax, r, gsel)
    p_g = 1.0 / functools.reduce(jnp.add, [jnp.exp(gr - gmax) for gr in g])
    epg = MOE_EXPERTS_PER_GROUP
    e_in = lt[8 + (MOE_GROUPS - 1) * epg:8 + MOE_GROUPS * epg, :]
    for r in range(MOE_GROUPS - 2, -1, -1):
        e_in = jnp.where(gsel == r, lt[8 + r * epg:8 + (r + 1) * epg, :], e_in)
    sub = lax.broadcasted_iota(jnp.int32, (epg, tm), 0)
    key1 = _order_key(e_in)
    i1 = jnp.min(jnp.where(key1 == jnp.max(key1, axis=0, keepdims=True), sub, epg), axis=0, keepdims=True)
    v1 = jnp.max(e_in, axis=0, keepdims=True)
    key2 = jnp.where(sub == i1, INT32_MIN, key1)
    i2 = jnp.min(jnp.where(key2 == jnp.max(key2, axis=0, keepdims=True), sub, epg), axis=0, keepdims=True)
    v2 = jnp.max(jnp.where(sub == i1, -jnp.inf, e_in), axis=0, keepdims=True)
    t = jnp.exp(v2 - v1)
    w1 = p_g / (1.0 + t)
    w2 = p_g * t / (1.0 + t)
    id1 = gsel * epg + i1
    id2 = gsel * epg + i2
    ids_ref[...] = jnp.concatenate([id1, id2], axis=0)
    gate_ref[...] = jnp.concatenate([w1, w2, jnp.zeros((LANES - 2, tm), F32)], axis=0).T

    eidx = lax.broadcasted_iota(jnp.int32, (MOE_EXPERTS, tm), 0)
    oh1 = eidx == id1
    oh2 = eidx == id2
    onehot = (oh1 | oh2).astype(BF16)
    tr = lax.broadcasted_iota(jnp.int32, (tm, tm), 0)
    tc = lax.broadcasted_iota(jnp.int32, (tm, tm), 1)
    before = (tr < tc).astype(BF16)
    prefix = jnp.dot(onehot, before, preferred_element_type=F32) + base_ref[...]
    r1 = jnp.sum(jnp.where(oh1, prefix, 0.0), axis=0, keepdims=True)
    r2 = jnp.sum(jnp.where(oh2, prefix, 0.0), axis=0, keepdims=True)
    rank_ref[...] = jnp.concatenate([r1, r2], axis=0).astype(jnp.int32)
    base_ref[...] = base_ref[...] + jnp.sum(onehot.astype(F32), axis=1, keepdims=True)
    cnt_ref[...] = base_ref[...].astype(jnp.int32)


def _out_project_route(acts, weights, x, mod, norm_w, wr_t, br):
    bsz, s, d = x.shape
    tm = min(ROUTER_ROWS, s)
    n = bsz * s
    nt = s // tm
    n_in = len(acts)
    tok = lambda b, i: (0, b * nt + i)
    const = lambda b, i: (0, 0)
    in_specs = [pl.BlockSpec((1, tm, a.shape[2]), lambda b, i: (b, i, 0)) for a in acts]
    in_specs += [pl.BlockSpec(w.shape, const) for w in weights]
    in_specs += [pl.BlockSpec((1, tm, d), lambda b, i: (b, i, 0)),
                 pl.BlockSpec((1, 6, d), lambda b, i: (b, 0, 0)),
                 pl.BlockSpec((1, d), const),
                 pl.BlockSpec((ROUTER_LOGIT_ROWS, d), const),
                 pl.BlockSpec((ROUTER_LOGIT_ROWS, 1), const)]
    return pl.pallas_call(
        functools.partial(_out_route_kernel, n_in=n_in),
        grid=(bsz, nt),
        in_specs=in_specs,
        out_specs=[pl.BlockSpec((1, tm, d), lambda b, i: (b, i, 0)),
                   pl.BlockSpec((MOE_TOPK, tm), tok),
                   pl.BlockSpec((tm, LANES), lambda b, i: (b * nt + i, 0)),
                   pl.BlockSpec((MOE_TOPK, tm), tok),
                   pl.BlockSpec((MOE_EXPERTS, 1), const)],
        out_shape=[jax.ShapeDtypeStruct((bsz, s, d), F32),
                   jax.ShapeDtypeStruct((MOE_TOPK, n), jnp.int32),
                   jax.ShapeDtypeStruct((n, LANES), F32),
                   jax.ShapeDtypeStruct((MOE_TOPK, n), jnp.int32),
                   jax.ShapeDtypeStruct((MOE_EXPERTS, 1), jnp.int32)],
        scratch_shapes=[pltpu.VMEM((MOE_EXPERTS, 1), F32)],
        compiler_params=_cparams("arbitrary", "arbitrary"),
        name="out_project_route",
    )(*acts, *weights, x, mod, norm_w.reshape(1, d), wr_t, br)


def _dest_kernel(ids_ref, rank_ref, start_ref, o_ref):
    ids = ids_ref[...]
    tm = ids.shape[1]
    eidx = lax.broadcasted_iota(jnp.int32, (MOE_EXPERTS, tm), 0)
    rows = []
    for k in range(MOE_TOPK):
        base = jnp.sum(jnp.where(eidx == ids[k:k + 1, :], start_ref[...], 0), axis=0, keepdims=True)
        rows.append(base + rank_ref[k:k + 1, :])
    o_ref[...] = jnp.concatenate(rows, axis=0)


def _dest_rows(ids, rank, start):
    n = ids.shape[1]
    tm = min(2048, n)
    return pl.pallas_call(
        _dest_kernel,
        grid=(n // tm,),
        in_specs=[pl.BlockSpec((MOE_TOPK, tm), lambda i: (0, i)),
                  pl.BlockSpec((MOE_TOPK, tm), lambda i: (0, i)),
                  pl.BlockSpec((MOE_EXPERTS, 1), lambda i: (0, 0))],
        out_specs=pl.BlockSpec((MOE_TOPK, tm), lambda i: (0, i)),
        out_shape=jax.ShapeDtypeStruct((MOE_TOPK, n), jnp.int32),
        compiler_params=_cparams("arbitrary"),
        name="moe_dest_rows",
    )(ids, rank, start)


def _row_copy(src, dst, sem, src_row, dst_row, rt):
    return pltpu.make_async_copy(src.at[pl.ds(pl.multiple_of(src_row * rt, rt), rt)],
                                 dst.at[pl.ds(pl.multiple_of(dst_row * rt, rt), rt)], sem)


def _dispatch_kernel(pend_ref, padded_ref, dest_ref, x_ref, mod_ref, nx_ref, nmod_ref, nw_ref, xs_ref,
                     h_ref, zero_ref, sem, *, rt):
    tm = x_ref.shape[1]
    zrows = zero_ref.shape[0] // rt
    step = pl.program_id(0) * pl.num_programs(1) + pl.program_id(1)
    first = step == 0
    slot = step % 2

    @pl.when(first)
    def _():
        zero_ref[...] = jnp.zeros(zero_ref.shape, F32)
        for e in range(MOE_EXPERTS):
            @pl.when(padded_ref[e] > 0)
            def _():
                row0 = pl.multiple_of((pend_ref[e] - zrows) * rt, zrows * rt)
                cp = pltpu.make_async_copy(zero_ref, xs_ref.at[pl.ds(row0, zrows * rt)], sem)
                cp.start()
                cp.wait()

        def zero_tail(b, _):
            row0 = pl.multiple_of(b * zrows * rt, zrows * rt)
            cp = pltpu.make_async_copy(zero_ref, xs_ref.at[pl.ds(row0, zrows * rt)], sem)
            cp.start()
            cp.wait()
            return 0

        lax.fori_loop(pend_ref[MOE_EXPERTS - 1] // zrows, xs_ref.shape[0] // (zrows * rt), zero_tail, 0)

    @pl.when(first)
    def _():
        _to_row_tiles(h_ref, _moe_input(x_ref[0], nw_ref, mod_ref), (0,))

    def start(t, _):
        for k in range(MOE_TOPK):
            _row_copy(h_ref.at[slot], xs_ref, sem, t, dest_ref[k, t], rt).start(priority=k)
        return 0

    lax.fori_loop(0, tm, start, 0, unroll=DMA_ISSUE_UNROLL)

    @pl.when(step + 1 < pl.num_programs(0) * pl.num_programs(1))
    def _():
        _to_row_tiles(h_ref, _moe_input(nx_ref[0], nw_ref, nmod_ref), (1 - slot,))

    for k in range(MOE_TOPK):
        pltpu.make_async_copy(h_ref.at[slot], xs_ref.at[pl.ds(0, tm * rt)], sem).wait()


def _dispatch(x, norm_w, mod, dest, pend, padded, n_rows):
    bsz, s, d = x.shape
    rt = d // LANES
    tm = min(DISPATCH_ROWS, s)
    nt = s // tm
    last = bsz * nt - 1

    def nxt(b, i):
        f = jnp.minimum(b * nt + i + 1, last)
        return f // nt, f % nt

    return pl.pallas_call(
        functools.partial(_dispatch_kernel, rt=rt),
        grid=(bsz, nt),
        in_specs=[pl.BlockSpec(memory_space=pltpu.SMEM),
                  pl.BlockSpec(memory_space=pltpu.SMEM),
                  pl.BlockSpec((MOE_TOPK, tm), lambda b, i: (0, b * nt + i), memory_space=pltpu.SMEM),
                  pl.BlockSpec((1, tm, d), lambda b, i: (b, i, 0)),
                  pl.BlockSpec((1, 6, d), lambda b, i: (b, 0, 0)),
                  pl.BlockSpec((1, tm, d), lambda b, i: (*nxt(b, i), 0)),
                  pl.BlockSpec((1, 6, d), lambda b, i: (nxt(b, i)[0], 0, 0)),
                  pl.BlockSpec((1, d), lambda b, i: (0, 0))],
        out_specs=pl.BlockSpec(memory_space=pl.ANY),
        out_shape=jax.ShapeDtypeStruct((n_rows * rt, LANES), F32),
        scratch_shapes=[pltpu.VMEM((2, tm * rt, LANES), F32), pltpu.VMEM((MOE_BLOCK_ROWS * rt, LANES), F32),
                        pltpu.SemaphoreType.DMA(())],
        compiler_params=_cparams("arbitrary", "arbitrary"),
        name="moe_dispatch",
    )(pend, padded, dest, x, mod, x, mod, norm_w.reshape(1, d))


def _expert_kernel(blk_e_ref, n_used_ref, x_ref, w1_ref, w3_ref, w2_ref, o_ref, w1b_ref, w3b_ref, w2b_ref):
    i = pl.program_id(0)
    used = i < n_used_ref[0]
    new_expert = (i == 0) | (blk_e_ref[i] != blk_e_ref[jnp.maximum(i - 1, 0)])

    @pl.when(used & new_expert)
    def _():
        w1b_ref[...] = w1_ref[0, 0].astype(BF16)
        w3b_ref[...] = w3_ref[0, 0].astype(BF16)
        w2b_ref[...] = w2_ref[0, 0].astype(BF16)

    @pl.when(used)
    def _():
        d = w1b_ref.shape[0]
        rows = x_ref.shape[0] * LANES // d
        x = _from_row_tiles(x_ref, rows, d).astype(BF16)
        a = jnp.dot(x, w1b_ref[...], preferred_element_type=F32)
        g = jnp.dot(x, w3b_ref[...], preferred_element_type=F32)
        y = jnp.dot((a * jax.nn.sigmoid(a) * g).astype(BF16), w2b_ref[...], preferred_element_type=F32)
        _to_row_tiles(o_ref, y)

    @pl.when(jnp.logical_not(used))
    def _():
        o_ref[...] = jnp.zeros(o_ref.shape, F32)


def _expert_ffn(xs, blk_e, n_used, w1, w3, w2, layer):
    d, hid = w1.shape[2], w1.shape[3]
    rt = d // LANES
    n_rows = xs.shape[0] // rt
    tm = MOE_BLOCK_ROWS
    row = lambda i, be, nu: (jnp.minimum(i, nu[0] - 1), 0)
    wsel = lambda i, be, nu: (layer, be[i], 0, 0)
    return pl.pallas_call(
        _expert_kernel,
        grid_spec=pltpu.PrefetchScalarGridSpec(
            num_scalar_prefetch=2,
            grid=(n_rows // tm,),
            in_specs=[pl.BlockSpec((tm * rt, LANES), row),
                      pl.BlockSpec((1, 1, d, hid), wsel),
                      pl.BlockSpec((1, 1, d, hid), wsel),
                      pl.BlockSpec((1, 1, hid, d), wsel)],
            out_specs=pl.BlockSpec((tm * rt, LANES), lambda i, be, nu: (i, 0)),
            scratch_shapes=[pltpu.VMEM((d, hid), BF16), pltpu.VMEM((d, hid), BF16),
                            pltpu.VMEM((hid, d), BF16)]),
        out_shape=jax.ShapeDtypeStruct((n_rows * rt, LANES), F32),
        compiler_params=_cparams("arbitrary"),
        name="moe_expert_ffn",
    )(blk_e, n_used, xs, w1, w3, w2)


def _combine_kernel(dest_ref, next_dest_ref, ys_ref, gate_ref, x_ref, mod_ref, *refs, outs):
    buf_ref, sem = refs[-2:]
    o_ref = refs[-3 - len(outs)]
    tm, d = x_ref.shape[1], x_ref.shape[2]
    rt = d // LANES
    step = pl.program_id(0) * pl.num_programs(1) + pl.program_id(1)
    n_steps = pl.num_programs(0) * pl.num_programs(1)
    slot = step % 2

    def gather(rows_ref, s):
        def start(t, _):
            for k in range(MOE_TOPK):
                _row_copy(ys_ref, buf_ref.at[s, k], sem.at[s], rows_ref[k, t], t, rt).start(priority=k)
            return 0
        lax.fori_loop(0, tm, start, 0, unroll=DMA_ISSUE_UNROLL)

    @pl.when(step == 0)
    def _():
        gather(dest_ref, 0)

    @pl.when(step + 1 < n_steps)
    def _():
        gather(next_dest_ref, 1 - slot)

    for k in range(MOE_TOPK):
        pltpu.make_async_copy(ys_ref.at[pl.ds(0, tm * rt)], buf_ref.at[slot, k], sem.at[slot]).wait()
    g = gate_ref[...]
    y = (g[:, 0:1] * _from_row_tiles(buf_ref, tm, d, (slot, 0))
         + g[:, 1:2] * _from_row_tiles(buf_ref, tm, d, (slot, 1)))
    x = x_ref[0] + mod_ref[0, 5:6, :] * y
    o_ref[0] = x
    if outs:
        nw_ref, next_mod_ref, w_ref = refs[0:3]
        n_norm = len(refs) - 6 - len(outs)
        _project_rows(x, nw_ref, next_mod_ref, w_ref, refs[3:3 + n_norm], refs[-2 - len(outs):-2], outs)


def _combine(ys, dest, gates, x, mod, proj=None):
    bsz, s, d = x.shape
    tm = min(COMBINE_ROWS, s)
    nt = s // tm
    last = bsz * nt - 1
    tile = lambda b, i: (b, i, 0)
    const = lambda b, i: (0, 0)
    in_specs = [pl.BlockSpec((MOE_TOPK, tm), lambda b, i: (0, b * nt + i), memory_space=pltpu.SMEM),
                pl.BlockSpec((MOE_TOPK, tm), lambda b, i: (0, jnp.minimum(b * nt + i + 1, last)),
                             memory_space=pltpu.SMEM),
                pl.BlockSpec(memory_space=pl.ANY),
                pl.BlockSpec((tm, LANES), lambda b, i: (b * nt + i, 0)),
                pl.BlockSpec((1, tm, d), tile),
                pl.BlockSpec((1, 6, d), lambda b, i: (b, 0, 0))]
    args = [dest, dest, ys, gates, x, mod]
    out_specs = [pl.BlockSpec((1, tm, d), tile)]
    out_shape = [jax.ShapeDtypeStruct((bsz, s, d), F32)]
    outs = ()
    if proj is not None:
        norm_w, next_mod, w_bf16, outs, dtypes, head_norms = proj
        in_specs += [pl.BlockSpec((1, d), const), pl.BlockSpec((1, 6, d), lambda b, i: (b, 0, 0)),
                     pl.BlockSpec(w_bf16.shape, const)]
        in_specs += [pl.BlockSpec((1, hn.shape[0]), const) for hn in head_norms]
        args += [norm_w.reshape(1, d), next_mod, w_bf16] + [hn.reshape(1, -1) for hn in head_norms]
        out_specs += [pl.BlockSpec((1, tm, o[0]), tile) for o in outs]
        out_shape += [jax.ShapeDtypeStruct((bsz, s, o[0]), dt) for o, dt in zip(outs, dtypes)]
    res = pl.pallas_call(
        functools.partial(_combine_kernel, outs=tuple(outs)),
        grid=(bsz, nt),
        in_specs=in_specs,
        out_specs=out_specs,
        out_shape=out_shape,
        scratch_shapes=[pltpu.VMEM((2, MOE_TOPK, tm * d // LANES, LANES), F32), pltpu.SemaphoreType.DMA((2,))],
        compiler_params=_cparams("arbitrary", "arbitrary"),
        name="moe_combine" if proj is None else "moe_combine_project",
    )(*args)
    return res[0] if proj is None else res


def _mixer_out_moe_residual(acts, w_outs, x, norm_w, mod, w_group, b_group, w_expert, b_expert, w1, w3, w2, layer,
                            next_proj):
    bsz, s, d = x.shape
    n = bsz * s
    tm = MOE_BLOCK_ROWS
    wr_t = jnp.zeros((ROUTER_LOGIT_ROWS, d), F32).at[0:MOE_GROUPS].set(w_group.T).at[8:].set(w_expert.T)
    br = jnp.zeros((ROUTER_LOGIT_ROWS, 1), F32).at[0:MOE_GROUPS, 0].set(b_group).at[8:, 0].set(b_expert)
    x_mid, ids, gates, rank, counts = _out_project_route(acts, w_outs, x, mod, norm_w, wr_t.astype(BF16), br)
    counts = counts[:, 0]
    padded = ((counts + tm - 1) // tm) * tm
    pend = jnp.cumsum(padded).astype(jnp.int32)
    start = pend - padded
    n_rows = ((n * MOE_TOPK + tm - 1) // tm) * tm + MOE_EXPERTS * tm
    blk_start = jnp.arange(n_rows // tm, dtype=jnp.int32) * tm
    blk_e = jnp.minimum(jnp.sum((pend[None, :] <= blk_start[:, None]).astype(jnp.int32), axis=1),
                        MOE_EXPERTS - 1)
    n_used = pend[-1:] // tm
    dest = _dest_rows(ids, rank, start.reshape(MOE_EXPERTS, 1))
    xs = _dispatch(x_mid, norm_w, mod, dest, pend, padded, n_rows)
    ys = _expert_ffn(xs, blk_e, n_used, w1, w3, w2, layer)
    return _combine(ys, dest, gates, x_mid, mod, next_proj)


def _even_proj_spec(w_in, q_norm, k_norm):
    d, n_in = w_in.shape
    aq = A_HEADS * A_HEAD_DIM
    akv = A_KV_HEADS * A_HEAD_DIM
    bk = B_HEADS * B_KEY_DIM
    bv = B_HEADS * B_VAL_DIM
    w_pad = jnp.zeros((d, n_in - B_GATE_RANK + LANES), F32).at[:, :n_in].set(w_in).astype(BF16)
    outs = ((aq, A_HEAD_DIM, 0, A_HEAD_DIM ** -0.5 * LOG2E), (akv, A_HEAD_DIM, 1, 1.0), (akv, 0, 0, 1.0),
            (2 * bk, 0, 0, 1.0), (bv, 0, 0, 1.0), (bv, 0, 0, 1.0), (LANES, 0, 0, 1.0))
    dtypes = (BF16, BF16, BF16, F32, BF16, F32, F32)
    return w_pad, outs, dtypes, (q_norm, k_norm)


def _odd_proj_spec(w_in, q_norm, k_norm):
    mix = C_HEADS * C_HEAD_DIM
    outs = ((mix, C_HEAD_DIM, 0, C_HEAD_DIM ** -0.5 * LOG2E), (mix, C_HEAD_DIM, 1, 1.0), (mix, 0, 0, 1.0))
    return w_in.astype(BF16), outs, (BF16, BF16, BF16), (q_norm, k_norm)


def _even_mixer(groups, w_out, sinks, gate_up, gate_bias, out_norm, band_bias):
    qa, ka, va, qk, vb, rb, ab = groups
    aq = A_HEADS * A_HEAD_DIM
    bk = B_HEADS * B_KEY_DIM
    oa = _swa_attention(qa, ka, va, sinks, band_bias)
    gu_pad = jnp.zeros((LANES, bk), F32).at[:B_GATE_RANK].set(gate_up).astype(BF16)
    ob = _gla(qk, vb, rb, ab, gu_pad, gate_bias, out_norm)
    w_out = w_out.astype(BF16)
    return [oa, ob], [w_out[:aq], w_out[aq:]]


def _odd_mixer(groups, w_out, bias_tiles):
    q, k, v = groups
    return [_moba(q, k, v, bias_tiles)], [w_out.astype(BF16)]


def kernel(x, c, rel_bias, ada_w, ada_b, norm1_w, norm2_w, even_w_in, even_w_out, a_q_norm, a_k_norm, a_sinks, b_gate_up, b_gate_bias, b_out_norm, odd_w_in, odd_w_out, c_q_norm, c_k_norm, moe_w_group, moe_b_group, moe_w_expert, moe_b_expert, moe_w1, moe_w3, moe_w2):
    depth = ada_w.shape[0]
    bsz, _, d = x.shape
    moba_bias, band_bias = _bias_tiles(rel_bias)
    mod_all = _adaln(c, ada_w, ada_b).reshape(depth, bsz, 6, d)

    def proj_spec(layer):
        j = layer // 2
        if layer % 2 == 0:
            return _even_proj_spec(even_w_in[j], a_q_norm[j], a_k_norm[j])
        return _odd_proj_spec(odd_w_in[j], c_q_norm[j], c_k_norm[j])

    w_in, outs, dtypes, head_norms = proj_spec(0)
    groups = _norm_mod_project(x, norm1_w[0], mod_all[0], w_in, outs, dtypes, head_norms)
    for layer in range(depth):
        mod = mod_all[layer]
        j = layer // 2
        if layer % 2 == 0:
            acts, w_outs = _even_mixer(groups, even_w_out[j], a_sinks[j], b_gate_up[j], b_gate_bias[j],
                                       b_out_norm[j], band_bias)
        else:
            acts, w_outs = _odd_mixer(groups, odd_w_out[j], moba_bias)
        next_proj = None
        if layer + 1 < depth:
            w_in, outs, dtypes, head_norms = proj_spec(layer + 1)
            next_proj = (norm1_w[layer + 1], mod_all[layer + 1], w_in, outs, dtypes, head_norms)
        res = _mixer_out_moe_residual(acts, w_outs, x, norm2_w[layer], mod, moe_w_group[layer],
                                      moe_b_group[layer], moe_w_expert[layer], moe_b_expert[layer],
                                      moe_w1, moe_w3, moe_w2, layer, next_proj)
        x, groups = (res, None) if next_proj is None else (res[0], res[1:])
    return x
```

```python
import functools
import math

import jax
import jax.numpy as jnp
import numpy as np
from jax import lax
from jax.experimental import pallas as pl
from jax.experimental.pallas import tpu as pltpu

RMS_EPS = 1e-6
A_HEADS, A_KV_HEADS, A_HEAD_DIM, A_WINDOW = 8, 2, 64, 128
B_HEADS, B_KEY_DIM, B_VAL_DIM, B_GATE_RANK, B_GATE_TAU, B_CHUNK = 4, 64, 128, 16, 16.0, 64
C_HEADS, C_HEAD_DIM, C_BLOCK, C_TOPK = 8, 128, 256, 3
REL_BUCKETS, REL_MAX_DIST, REL_HEADS = 32, 128, 8
MOE_GROUPS, MOE_EXPERTS_PER_GROUP, MOE_TOPK = 4, 8, 2
MOE_EXPERTS = MOE_GROUPS * MOE_EXPERTS_PER_GROUP

LANES = 128
V7X_VMEM_LIMIT_BYTES = 56 * 1024 * 1024
NEG_BIG = -1e30
INT32_MIN = -2 ** 31
PROJ_ROWS = 512
GLA_ROWS = 512
ROUTER_ROWS = 512
MOE_BLOCK_ROWS = 512
DISPATCH_ROWS = 1024
COMBINE_ROWS = 512
DMA_ISSUE_UNROLL = 8
ROUTER_LOGIT_ROWS = 8 + MOE_EXPERTS
MOBA_GROUP = 2
MOBA_HEADS_PER_STEP = 4
LOG2E = math.log2(math.e)
SWA_VT_ROWS = A_HEAD_DIM + 16
MOBA_VT_ROWS = C_HEAD_DIM + 16

F32 = jnp.float32
BF16 = jnp.bfloat16


def _cparams(*sem):
    return pltpu.CompilerParams(dimension_semantics=sem, vmem_limit_bytes=V7X_VMEM_LIMIT_BYTES)


def _rel_bucket_np(dist):
    exact = REL_BUCKETS // 2
    d = np.maximum(dist, 0)
    logd = np.log(np.maximum(d, 1).astype(np.float64) / exact) / math.log(REL_MAX_DIST / exact)
    far = np.minimum(exact + (logd * (REL_BUCKETS - exact)).astype(np.int64), REL_BUCKETS - 1)
    return np.where(d < exact, d, far).astype(np.int32)


def _bucket_tiles():
    li = np.arange(C_BLOCK)
    own = _rel_bucket_np(li[None, :] - li[:, None])
    prev = _rel_bucket_np(li[None, :] - li[:, None] + C_BLOCK)
    band = _rel_bucket_np(np.arange(A_WINDOW)[None, :] + A_WINDOW - np.arange(2 * A_WINDOW)[:, None])
    return np.concatenate([own, prev], axis=0), band


def _rms_rows(x, w):
    return x * lax.rsqrt(jnp.mean(x * x, axis=-1, keepdims=True) + RMS_EPS) * w


def _dot(a, b):
    return jnp.dot(a.astype(BF16), b.astype(BF16), preferred_element_type=F32)


def _dot_nt(a, b):
    return lax.dot_general(a.astype(BF16), b.astype(BF16), (((1,), (1,)), ((), ())),
                           preferred_element_type=F32)


def _order_key(x):
    bits = lax.bitcast_convert_type(x, jnp.int32)
    return jnp.where(bits < 0, bits ^ jnp.int32(0x7FFFFFFF), bits)


def _to_row_tiles(ref, x, lead=()):
    rows, d = x.shape
    chunks = d // LANES
    for c in range(chunks):
        ref[lead + (pl.ds(c, rows, stride=chunks), slice(None))] = x[:, c * LANES:(c + 1) * LANES]


def _from_row_tiles(ref, rows, d, lead=()):
    chunks = d // LANES
    return jnp.concatenate([ref[lead + (pl.ds(c, rows, stride=chunks), slice(None))] for c in range(chunks)],
                           axis=1)


def _dot_tn(a, b):
    return lax.dot_general(a.astype(BF16), b.astype(BF16), (((0,), (0,)), ((), ())),
                           preferred_element_type=F32)


def _bias_kernel(rb_ref, bkt_ref, o_ref, *, relative_to_last):
    h = pl.program_id(0)
    bkt = bkt_ref[...]
    acc = jnp.zeros(bkt.shape, F32)
    for b in range(REL_BUCKETS):
        acc = jnp.where(bkt == b, rb_ref[b, h], acc)
    if relative_to_last:
        acc = acc - rb_ref[REL_BUCKETS - 1, h]
    o_ref[0] = acc * LOG2E


def _bias_tile(rel_bias, bkt, relative_to_last):
    rows, cols = bkt.shape
    return pl.pallas_call(
        functools.partial(_bias_kernel, relative_to_last=relative_to_last),
        grid=(REL_HEADS,),
        in_specs=[pl.BlockSpec(memory_space=pltpu.SMEM),
                  pl.BlockSpec((rows, cols), lambda h: (0, 0))],
        out_specs=pl.BlockSpec((1, rows, cols), lambda h: (h, 0, 0)),
        out_shape=jax.ShapeDtypeStruct((REL_HEADS, rows, cols), F32),
        compiler_params=_cparams("arbitrary"),
        name="rel_bias_tiles",
    )(rel_bias, jnp.asarray(bkt))


def _bias_tiles(rel_bias):
    moba_bkt, band_bkt = _bucket_tiles()
    return _bias_tile(rel_bias, moba_bkt, True), _bias_tile(rel_bias, band_bkt, False)


def _adaln_kernel(c_ref, w_ref, b_ref, o_ref):
    c = c_ref[...]
    cond = c * jax.nn.sigmoid(c)
    o_ref[0] = _dot(cond, w_ref[0]) + b_ref[0]


def _adaln(c, ada_w, ada_b):
    depth, d, n6 = ada_w.shape
    bsz = c.shape[0]
    tn = 1536 if n6 % 1536 == 0 else n6
    return pl.pallas_call(
        _adaln_kernel,
        grid=(depth, n6 // tn),
        in_specs=[pl.BlockSpec((bsz, d), lambda l, j: (0, 0)),
                  pl.BlockSpec((1, d, tn), lambda l, j: (l, 0, j)),
                  pl.BlockSpec((1, 1, tn), lambda l, j: (l, 0, j))],
        out_specs=pl.BlockSpec((1, bsz, tn), lambda l, j: (l, 0, j)),
        out_shape=jax.ShapeDtypeStruct((depth, bsz, n6), F32),
        compiler_params=_cparams("arbitrary", "arbitrary"),
        name="adaln_mod",
    )(c, ada_w, ada_b.reshape(depth, 1, n6))


def _head_rms(y, w_ref, hd, post_scale):
    cols = y.shape[1]
    lane = lax.broadcasted_iota(jnp.int32, (1, LANES), 1)
    w = w_ref[...] * post_scale
    out = []
    for c in range(cols // LANES):
        t = y[:, c * LANES:(c + 1) * LANES]
        sq = t * t
        if hd == LANES:
            r = lax.rsqrt(jnp.sum(sq, axis=-1, keepdims=True) * (1.0 / hd) + RMS_EPS)
            out.append(t * r * w)
        else:
            low = lane < hd
            s_lo = jnp.sum(jnp.where(low, sq, 0.0), axis=-1, keepdims=True)
            s_hi = jnp.sum(jnp.where(low, 0.0, sq), axis=-1, keepdims=True)
            r = lax.rsqrt(jnp.where(low, s_lo, s_hi) * (1.0 / hd) + RMS_EPS)
            out.append(t * r * jnp.concatenate([w, w], axis=1))
    return jnp.concatenate(out, axis=1) if len(out) > 1 else out[0]


def _project_rows(x, nw_ref, mod_ref, w_ref, norm_refs, o_refs, outs):
    h = _rms_rows(x, nw_ref[...])
    h = h * (1.0 + mod_ref[0, 1:2, :]) + mod_ref[0, 0:1, :]
    y = _dot(h, w_ref[...])
    off = 0
    for o_ref, (wd, hd, widx, post) in zip(o_refs, outs):
        t = y[:, off:off + wd]
        if hd:
            t = _head_rms(t, norm_refs[widx], hd, post)
        o_ref[0] = t.astype(o_ref.dtype)
        off += wd


def _proj_kernel(x_ref, nw_ref, mod_ref, w_ref, *refs, outs):
    n_norm = len(refs) - len(outs)
    _project_rows(x_ref[0], nw_ref, mod_ref, w_ref, refs[:n_norm], refs[n_norm:], outs)


def _norm_mod_project(x, norm_w, mod, w_bf16, outs, dtypes, head_norms):
    bsz, s, d = x.shape
    tm = min(PROJ_ROWS, s)
    n = w_bf16.shape[1]
    widths = [o[0] for o in outs]
    assert sum(widths) == n and all(wd % LANES == 0 for wd in widths)
    return pl.pallas_call(
        functools.partial(_proj_kernel, outs=tuple(outs)),
        grid=(bsz, s // tm),
        in_specs=[pl.BlockSpec((1, tm, d), lambda b, i: (b, i, 0)),
                  pl.BlockSpec((1, d), lambda b, i: (0, 0)),
                  pl.BlockSpec((1, 6, d), lambda b, i: (b, 0, 0)),
                  pl.BlockSpec((d, n), lambda b, i: (0, 0))]
                 + [pl.BlockSpec((1, hn.shape[0]), lambda b, i: (0, 0)) for hn in head_norms],
        out_specs=[pl.BlockSpec((1, tm, wd), lambda b, i: (b, i, 0)) for wd in widths],
        out_shape=[jax.ShapeDtypeStruct((bsz, s, wd), dt) for wd, dt in zip(widths, dtypes)],
        compiler_params=_cparams("arbitrary", "arbitrary"),
        name="norm_mod_project",
    )(x, norm_w.reshape(1, d), mod, w_bf16, *[hn.reshape(1, -1) for hn in head_norms])


def _swa_kernel(sink_ref, q_ref, kc_ref, kp_ref, vc_ref, vp_ref, bias_ref, o_ref):
    n = pl.program_id(1)
    w = A_WINDOW
    hd = A_HEAD_DIM
    group = A_HEADS // A_KV_HEADS
    ones_rows = (lax.broadcasted_iota(jnp.int32, (SWA_VT_ROWS - hd, 2 * w), 0) == 0).astype(BF16)
    key = lax.broadcasted_iota(jnp.int32, (2 * w, group * w), 0)
    qry = lax.broadcasted_iota(jnp.int32, (2 * w, group * w), 1) % w
    dist = qry + w - key
    in_window = (dist >= 0) & (dist < w)
    masks = [in_window & ((n > 0) | (key >= w)), in_window]
    head_of_lane = lax.broadcasted_iota(jnp.int32, (1, group * w), 1) // w
    k_bands = [jnp.concatenate([kp_ref[0], kc_ref[0, 0:w, :]], axis=0), kc_ref[0]]
    v_bands = [jnp.concatenate([vp_ref[0], vc_ref[0, 0:w, :]], axis=0), vc_ref[0]]
    v_ts = [v.astype(F32).T for v in v_bands]
    sinks = []
    for kv in range(A_KV_HEADS):
        h0 = kv * group
        sink = jnp.full((1, group * w), sink_ref[h0 + group - 1] * LOG2E, F32)
        for g in range(group - 2, -1, -1):
            sink = jnp.where(head_of_lane == g, sink_ref[h0 + g] * LOG2E, sink)
        sinks.append(sink)
    chains = [(blk, kv) for blk in range(2) for kv in range(A_KV_HEADS)]
    logits, vts = [], []
    for blk, kv in chains:
        h0 = kv * group
        k_g = k_bands[blk][:, kv * hd:(kv + 1) * hd]
        vts.append(jnp.concatenate([v_ts[blk][kv * hd:(kv + 1) * hd, :].astype(BF16), ones_rows], axis=0))
        q_g = jnp.concatenate([q_ref[0, blk * w:(blk + 1) * w, (h0 + g) * hd:(h0 + g + 1) * hd]
                               for g in range(group)], axis=0)
        bias = jnp.concatenate([bias_ref[h0 + g] for g in range(group)], axis=1)
        logits.append(jnp.where(masks[blk], _dot_nt(k_g, q_g) + bias, NEG_BIG))
    maxima = [jnp.maximum(jnp.max(lg, axis=0, keepdims=True), sinks[kv]) for lg, (_, kv) in zip(logits, chains)]
    accs = [jnp.dot(vt_g, jnp.exp2(lg - m).astype(BF16), preferred_element_type=F32)
            for vt_g, lg, m in zip(vts, logits, maxima)]
    for (blk, kv), acc, m in zip(chains, accs, maxima):
        h0 = kv * group
        o_t = acc[0:hd] / (acc[hd:hd + 1] + jnp.exp2(sinks[kv] - m))
        for pair in range(group // 2):
            two = jnp.concatenate([o_t[:, (2 * pair) * w:(2 * pair + 1) * w],
                                   o_t[:, (2 * pair + 1) * w:(2 * pair + 2) * w]], axis=0)
            c0 = (h0 + 2 * pair) * hd
            o_ref[0, blk * w:(blk + 1) * w, c0:c0 + 2 * hd] = two.T.astype(o_ref.dtype)


def _swa_attention(qa, ka, va, sinks, band_bias):
    bsz, s, _ = qa.shape
    w = A_WINDOW
    assert s % (2 * w) == 0
    kvw = A_KV_HEADS * A_HEAD_DIM
    cur = lambda b, n: (b, n, 0)
    prev = lambda b, n: (b, jnp.maximum(2 * n - 1, 0), 0)
    return pl.pallas_call(
        _swa_kernel,
        grid=(bsz, s // (2 * w)),
        in_specs=[pl.BlockSpec(memory_space=pltpu.SMEM),
                  pl.BlockSpec((1, 2 * w, A_HEADS * A_HEAD_DIM), cur),
                  pl.BlockSpec((1, 2 * w, kvw), cur),
                  pl.BlockSpec((1, w, kvw), prev),
                  pl.BlockSpec((1, 2 * w, kvw), cur),
                  pl.BlockSpec((1, w, kvw), prev),
                  pl.BlockSpec((REL_HEADS, 2 * w, w), lambda b, n: (0, 0, 0))],
        out_specs=pl.BlockSpec((1, 2 * w, A_HEADS * A_HEAD_DIM), cur),
        out_shape=jax.ShapeDtypeStruct((bsz, s, A_HEADS * A_HEAD_DIM), BF16),
        compiler_params=_cparams("arbitrary", "arbitrary"),
        name="swa_sink_attention",
    )(sinks, qa, ka, ka, va, va, band_bias)


def _gla_kernel(qk_ref, v_ref, r_ref, ab_ref, gu_ref, gb_ref, on_ref, o_ref, state_ref):
    s_idx = pl.program_id(1)
    c_len = B_CHUNK
    dk, dv = B_KEY_DIM, B_VAL_DIM
    hk = B_HEADS * dk

    @pl.when(s_idx == 0)
    def _():
        state_ref[...] = jnp.zeros(state_ref.shape, F32)

    z = _dot(ab_ref[0], gu_ref[...]) + gb_ref[...]
    log_a = (jnp.minimum(z, 0.0) - jnp.log(1.0 + jnp.exp(-jnp.abs(z)))) / B_GATE_TAU
    rows = qk_ref.shape[1]
    chunks = range(rows // c_len)
    heads = range(B_HEADS)
    ri = lax.broadcasted_iota(jnp.int32, (c_len, c_len), 0)
    ci = lax.broadcasted_iota(jnp.int32, (c_len, c_len), 1)
    tril = ri >= ci
    tri = tril.astype(BF16)
    g_hi = log_a.astype(BF16)
    g_lo = (log_a - g_hi.astype(F32)).astype(BF16)
    b_c = [jnp.dot(tri, g_hi[c * c_len:(c + 1) * c_len], preferred_element_type=F32)
           + jnp.dot(tri, g_lo[c * c_len:(c + 1) * c_len], preferred_element_type=F32) for c in chunks]
    b = jnp.concatenate(b_c, axis=0)
    last_c = [bc[c_len - 1:c_len, :] for bc in b_c]
    b_last = jnp.concatenate([jnp.broadcast_to(l, (c_len, hk)) for l in last_c], axis=0)
    q = qk_ref[0, :, 0:hk] * (dk ** -0.5)
    k = qk_ref[0, :, hk:2 * hk]
    q_dec = (q * jnp.exp(b)).astype(BF16)
    k_dec = (k * jnp.exp(-b)).astype(BF16)
    k_upd = (k * jnp.exp(b_last - b)).astype(BF16)
    v = v_ref[0]

    def rows_of(x, c, cols):
        return x[c * c_len:(c + 1) * c_len, cols]

    att = [[jnp.where(tril, _dot_nt(rows_of(q_dec, c, slice(h * dk, (h + 1) * dk)),
                                    rows_of(k_dec, c, slice(h * dk, (h + 1) * dk))), 0.0).astype(BF16)
            for h in heads] for c in chunks]
    upd = [[_dot_tn(rows_of(v, c, slice(h * dv, (h + 1) * dv)), rows_of(k_upd, c, slice(h * dk, (h + 1) * dk)))
            for h in heads] for c in chunks]
    state_in = []
    st = [state_ref[h] for h in heads]
    for c in chunks:
        state_in.append(st)
        decay = jnp.exp(last_c[c])
        st = [st[h] * decay[:, h * dk:(h + 1) * dk] + upd[c][h] for h in heads]
    for h in heads:
        state_ref[h] = st[h]
    for c in chunks:
        outs = []
        for h in heads:
            o_h = (_dot(att[c][h], rows_of(v, c, slice(h * dv, (h + 1) * dv)))
                   + _dot_nt(rows_of(q_dec, c, slice(h * dk, (h + 1) * dk)), state_in[c][h]))
            o_h = _rms_rows(o_h, on_ref[...])
            r_h = r_ref[0, c * c_len:(c + 1) * c_len, h * dv:(h + 1) * dv]
            outs.append(o_h * (r_h * jax.nn.sigmoid(r_h)))
        o_ref[0, c * c_len:(c + 1) * c_len, :] = jnp.concatenate(outs, axis=-1).astype(o_ref.dtype)


def _gla(qk, vb, rb, ab, gate_up_pad, gate_bias, out_norm):
    bsz, s, _ = qk.shape
    tm = min(GLA_ROWS, s)
    hv = B_HEADS * B_VAL_DIM
    hk = B_HEADS * B_KEY_DIM
    blk = lambda b, i: (b, i, 0)
    const = lambda b, i: (0, 0)
    return pl.pallas_call(
        _gla_kernel,
        grid=(bsz, s // tm),
        in_specs=[pl.BlockSpec((1, tm, 2 * hk), blk),
                  pl.BlockSpec((1, tm, hv), blk),
                  pl.BlockSpec((1, tm, hv), blk),
                  pl.BlockSpec((1, tm, LANES), blk),
                  pl.BlockSpec((LANES, hk), const),
                  pl.BlockSpec((1, hk), const),
                  pl.BlockSpec((1, B_VAL_DIM), const)],
        out_specs=pl.BlockSpec((1, tm, hv), blk),
        out_shape=jax.ShapeDtypeStruct((bsz, s, hv), BF16),
        scratch_shapes=[pltpu.VMEM((B_HEADS, B_VAL_DIM, B_KEY_DIM), F32)],
        compiler_params=_cparams("arbitrary", "arbitrary"),
        name="gated_linear_attention",
    )(qk, vb, rb, ab, gate_up_pad, gate_bias.reshape(1, hk), out_norm.reshape(1, -1))


def _moba_kernel(q_ref, k_ref, v_ref, bias_ref, o_ref, vt_ref, kmean_ref, pick_ref):
    i = pl.program_id(2)
    blk = C_BLOCK
    hd = C_HEAD_DIM
    nb = k_ref.shape[1] // blk
    grp = MOBA_GROUP
    heads = range(MOBA_HEADS_PER_STEP)

    def lanes(hh):
        return slice(hh * hd, (hh + 1) * hd)

    @pl.when(i == 0)
    def _():
        ones_rows = (lax.broadcasted_iota(jnp.int32, (MOBA_VT_ROWS - hd, blk), 0) == 0).astype(BF16)
        for hh in heads:
            kn = k_ref[0, :, lanes(hh)].astype(F32)
            kmean_ref[hh] = jnp.mean(kn.reshape(nb, blk, hd), axis=1)
            for j in range(nb):
                vt = v_ref[0, j * blk:(j + 1) * blk, lanes(hh)].astype(F32).T.astype(BF16)
                vt_ref[hh, j] = jnp.concatenate([vt, ones_rows], axis=0)

    def logits(hh, q_h, j):
        j0 = pl.multiple_of(j * blk, blk)
        return _dot_nt(k_ref[0, pl.ds(j0, blk), lanes(hh)], q_h)

    def col_max(lg, picked):
        return jnp.where(picked, jnp.max(lg, axis=0, keepdims=True), NEG_BIG)

    def weighted_v(hh, j, lg, m, picked):
        p = jnp.exp2(lg - m).astype(BF16)
        return jnp.where(picked, jnp.dot(vt_ref[hh, j], p, preferred_element_type=F32), 0.0)

    def softmax_step(ms, accs, tiles):
        m_new = []
        for hh in heads:
            m_h = ms[hh]
            for _, lg, picked in tiles[hh]:
                m_h = jnp.maximum(m_h, col_max(lg, picked))
            m_new.append(m_h)
        out = []
        for hh in heads:
            acc = accs[hh] * jnp.exp2(ms[hh] - m_new[hh])
            for j, lg, picked in tiles[hh]:
                acc = acc + weighted_v(hh, j, lg, m_new[hh], picked)
            out.append(acc)
        return tuple(m_new), tuple(out)

    jrow = lax.broadcasted_iota(jnp.int32, (nb, blk), 0)
    key = lax.broadcasted_iota(jnp.int32, (blk, blk), 0)
    qry = lax.broadcasted_iota(jnp.int32, (blk, blk), 1)
    j_prev = jnp.maximum(i - 1, 0)
    qs = []
    for hh in heads:
        q_h = q_ref[0, :, lanes(hh)]
        qs.append(q_h)
        key_h = jnp.where(jrow < i, _order_key(_dot_nt(kmean_ref[hh], q_h)), INT32_MIN)
        chosen = jnp.zeros((nb, blk), F32)
        for _ in range(C_TOPK):
            best = jnp.max(key_h, axis=0, keepdims=True)
            arg = jnp.min(jnp.where(key_h == best, jrow, nb), axis=0, keepdims=True)
            hit = jrow == arg
            chosen = jnp.where(hit, 1.0, chosen)
            key_h = jnp.where(hit, INT32_MIN, key_h)
        pick_ref[hh] = jnp.where(jrow < i, chosen, 0.0)

    first = []
    for hh in heads:
        lg_own = jnp.where(key <= qry, logits(hh, qs[hh], i) + bias_ref[hh, 0:blk, :], NEG_BIG)
        lg_prev = logits(hh, qs[hh], j_prev) + bias_ref[hh, blk:2 * blk, :]
        pick_prev = pick_ref[hh, pl.ds(j_prev, 1), :] > 0.5
        first.append([(i, lg_own, True), (j_prev, lg_prev, pick_prev)])
    start = (tuple(jnp.full((1, blk), NEG_BIG, F32) for _ in heads),
             tuple(jnp.zeros((MOBA_VT_ROWS, blk), F32) for _ in heads))
    carry0 = softmax_step(start[0], start[1], first)

    n_far = jnp.maximum(i - 1, 0)

    def body(g, carry):
        tiles = []
        for hh in heads:
            row = []
            for u in range(grp):
                j = g * grp + u
                jc = jnp.minimum(j, nb - 1)
                picked = (pick_ref[hh, pl.ds(jc, 1), :] > 0.5) & (j < n_far)
                row.append((jc, logits(hh, qs[hh], jc), picked))
            tiles.append(row)
        return softmax_step(carry[0], carry[1], tiles)

    _, accs = lax.fori_loop(0, (n_far + grp - 1) // grp, body, carry0)
    for hh in heads:
        o_ref[0, :, lanes(hh)] = (accs[hh][0:hd] / accs[hh][hd:hd + 1]).T.astype(o_ref.dtype)


def _moba(q, k, v, bias_tiles):
    bsz, s, _ = q.shape
    blk, hd = C_BLOCK, C_HEAD_DIM
    assert s % blk == 0
    nb = s // blk
    hb = MOBA_HEADS_PER_STEP
    assert C_HEADS % hb == 0
    return pl.pallas_call(
        _moba_kernel,
        grid=(bsz, C_HEADS // hb, nb),
        in_specs=[pl.BlockSpec((1, blk, hb * hd), lambda b, h, i: (b, i, h)),
                  pl.BlockSpec((1, s, hb * hd), lambda b, h, i: (b, 0, h)),
                  pl.BlockSpec((1, s, hb * hd), lambda b, h, i: (b, 0, h)),
                  pl.BlockSpec((hb, 2 * blk, blk), lambda b, h, i: (h, 0, 0))],
        out_specs=pl.BlockSpec((1, blk, hb * hd), lambda b, h, i: (b, i, h)),
        out_shape=jax.ShapeDtypeStruct((bsz, s, C_HEADS * hd), BF16),
        scratch_shapes=[pltpu.VMEM((hb, nb, MOBA_VT_ROWS, blk), BF16),
                        pltpu.VMEM((hb, nb, hd), F32), pltpu.VMEM((hb, nb, blk), F32)],
        compiler_params=_cparams("arbitrary", "arbitrary", "arbitrary"),
        name="moba_attention",
    )(q, k, v, bias_tiles)


def _moe_input(x, nw_ref, mod_ref):
    h = _rms_rows(x, nw_ref[...])
    return h * (1.0 + mod_ref[0, 4:5, :]) + mod_ref[0, 3:4, :]


def _out_route_kernel(*refs, n_in):
    a_refs = refs[:n_in]
    w_refs = refs[n_in:2 * n_in]
    (x_ref, mod_ref, nw_ref, wr_ref, br_ref,
     xo_ref, ids_ref, gate_ref, rank_ref, cnt_ref, base_ref) = refs[2 * n_in:]
    first = (pl.program_id(0) == 0) & (pl.program_id(1) == 0)

    @pl.when(first)
    def _():
        base_ref[...] = jnp.zeros(base_ref.shape, F32)

    y = _dot(a_refs[0][0], w_refs[0][...])
    for a_ref, w_ref in zip(a_refs[1:], w_refs[1:]):
        y = y + _dot(a_ref[0], w_ref[...])
    x = x_ref[0] + mod_ref[0, 2:3, :] * y
    xo_ref[0] = x

    tm = x.shape[0]
    lt = _dot_nt(wr_ref[...], _moe_input(x, nw_ref, mod_ref)) + br_ref[...]
    g = [lt[r:r + 1, :] for r in range(MOE_GROUPS)]
    gmax = functools.reduce(jnp.maximum, g)
    gsel = jnp.full(gmax.shape, MOE_GROUPS - 1, jnp.int32)
    for r in range(MOE_GROUPS - 2, -1, -1):
        gsel = jnp.where(g[r] == gmax, r, gsel)
    p_g = 1.0 / functools.reduce(jnp.add, [jnp.exp(gr - gmax) for gr in g])
    epg = MOE_EXPERTS_PER_GROUP
    e_in = lt[8 + (MOE_GROUPS - 1) * epg:8 + MOE_GROUPS * epg, :]
    for r in range(MOE_GROUPS - 2, -1, -1):
        e_in = jnp.where(gsel == r, lt[8 + r * epg:8 + (r + 1) * epg, :], e_in)
    sub = lax.broadcasted_iota(jnp.int32, (epg, tm), 0)
    key1 = _order_key(e_in)
    i1 = jnp.min(jnp.where(key1 == jnp.max(key1, axis=0, keepdims=True), sub, epg), axis=0, keepdims=True)
    v1 = jnp.max(e_in, axis=0, keepdims=True)
    key2 = jnp.where(sub == i1, INT32_MIN, key1)
    i2 = jnp.min(jnp.where(key2 == jnp.max(key2, axis=0, keepdims=True), sub, epg), axis=0, keepdims=True)
    v2 = jnp.max(jnp.where(sub == i1, -jnp.inf, e_in), axis=0, keepdims=True)
    t = jnp.exp(v2 - v1)
    w1 = p_g / (1.0 + t)
    w2 = p_g * t / (1.0 + t)
    id1 = gsel * epg + i1
    id2 = gsel * epg + i2
    ids_ref[...] = jnp.concatenate([id1, id2], axis=0)
    gate_ref[...] = jnp.concatenate([w1, w2, jnp.zeros((LANES - 2, tm), F32)], axis=0).T

    eidx = lax.broadcasted_iota(jnp.int32, (MOE_EXPERTS, tm), 0)
    oh1 = eidx == id1
    oh2 = eidx == id2
    onehot = (oh1 | oh2).astype(BF16)
    tr = lax.broadcasted_iota(jnp.int32, (tm, tm), 0)
    tc = lax.broadcasted_iota(jnp.int32, (tm, tm), 1)
    before = (tr < tc).astype(BF16)
    prefix = jnp.dot(onehot, before, preferred_element_type=F32) + base_ref[...]
    r1 = jnp.sum(jnp.where(oh1, prefix, 0.0), axis=0, keepdims=True)
    r2 = jnp.sum(jnp.where(oh2, prefix, 0.0), axis=0, keepdims=True)
    rank_ref[...] = jnp.concatenate([r1, r2], axis=0).astype(jnp.int32)
    base_ref[...] = base_ref[...] + jnp.sum(onehot.astype(F32), axis=1, keepdims=True)
    cnt_ref[...] = base_ref[...].astype(jnp.int32)


def _out_project_route(acts, weights, x, mod, norm_w, wr_t, br):
    bsz, s, d = x.shape
    tm = min(ROUTER_ROWS, s)
    n = bsz * s
    nt = s // tm
    n_in = len(acts)
    tok = lambda b, i: (0, b * nt + i)
    const = lambda b, i: (0, 0)
    in_specs = [pl.BlockSpec((1, tm, a.shape[2]), lambda b, i: (b, i, 0)) for a in acts]
    in_specs += [pl.BlockSpec(w.shape, const) for w in weights]
    in_specs += [pl.BlockSpec((1, tm, d), lambda b, i: (b, i, 0)),
                 pl.BlockSpec((1, 6, d), lambda b, i: (b, 0, 0)),
                 pl.BlockSpec((1, d), const),
                 pl.BlockSpec((ROUTER_LOGIT_ROWS, d), const),
                 pl.BlockSpec((ROUTER_LOGIT_ROWS, 1), const)]
    return pl.pallas_call(
        functools.partial(_out_route_kernel, n_in=n_in),
        grid=(bsz, nt),
        in_specs=in_specs,
        out_specs=[pl.BlockSpec((1, tm, d), lambda b, i: (b, i, 0)),
                   pl.BlockSpec((MOE_TOPK, tm), tok),
                   pl.BlockSpec((tm, LANES), lambda b, i: (b * nt + i, 0)),
                   pl.BlockSpec((MOE_TOPK, tm), tok),
                   pl.BlockSpec((MOE_EXPERTS, 1), const)],
        out_shape=[jax.ShapeDtypeStruct((bsz, s, d), F32),
                   jax.ShapeDtypeStruct((MOE_TOPK, n), jnp.int32),
                   jax.ShapeDtypeStruct((n, LANES), F32),
                   jax.ShapeDtypeStruct((MOE_TOPK, n), jnp.int32),
                   jax.ShapeDtypeStruct((MOE_EXPERTS, 1), jnp.int32)],
        scratch_shapes=[pltpu.VMEM((MOE_EXPERTS, 1), F32)],
        compiler_params=_cparams("arbitrary", "arbitrary"),
        name="out_project_route",
    )(*acts, *weights, x, mod, norm_w.reshape(1, d), wr_t, br)


def _dest_kernel(ids_ref, rank_ref, start_ref, o_ref):
    ids = ids_ref[...]
    tm = ids.shape[1]
    eidx = lax.broadcasted_iota(jnp.int32, (MOE_EXPERTS, tm), 0)
    rows = []
    for k in range(MOE_TOPK):
        base = jnp.sum(jnp.where(eidx == ids[k:k + 1, :], start_ref[...], 0), axis=0, keepdims=True)
        rows.append(base + rank_ref[k:k + 1, :])
    o_ref[...] = jnp.concatenate(rows, axis=0)


def _dest_rows(ids, rank, start):
    n = ids.shape[1]
    tm = min(2048, n)
    return pl.pallas_call(
        _dest_kernel,
        grid=(n // tm,),
        in_specs=[pl.BlockSpec((MOE_TOPK, tm), lambda i: (0, i)),
                  pl.BlockSpec((MOE_TOPK, tm), lambda i: (0, i)),
                  pl.BlockSpec((MOE_EXPERTS, 1), lambda i: (0, 0))],
        out_specs=pl.BlockSpec((MOE_TOPK, tm), lambda i: (0, i)),
        out_shape=jax.ShapeDtypeStruct((MOE_TOPK, n), jnp.int32),
        compiler_params=_cparams("arbitrary"),
        name="moe_dest_rows",
    )(ids, rank, start)


def _row_copy(src, dst, sem, src_row, dst_row, rt):
    return pltpu.make_async_copy(src.at[pl.ds(pl.multiple_of(src_row * rt, rt), rt)],
                                 dst.at[pl.ds(pl.multiple_of(dst_row * rt, rt), rt)], sem)


def _dispatch_kernel(pend_ref, padded_ref, dest_ref, x_ref, mod_ref, nx_ref, nmod_ref, nw_ref, xs_ref,
                     h_ref, zero_ref, sem, *, rt):
    tm = x_ref.shape[1]
    zrows = zero_ref.shape[0] // rt
    step = pl.program_id(0) * pl.num_programs(1) + pl.program_id(1)
    first = step == 0
    slot = step % 2

    @pl.when(first)
    def _():
        zero_ref[...] = jnp.zeros(zero_ref.shape, F32)
        for e in range(MOE_EXPERTS):
            @pl.when(padded_ref[e] > 0)
            def _():
                row0 = pl.multiple_of((pend_ref[e] - zrows) * rt, zrows * rt)
                cp = pltpu.make_async_copy(zero_ref, xs_ref.at[pl.ds(row0, zrows * rt)], sem)
                cp.start()
                cp.wait()

        def zero_tail(b, _):
            row0 = pl.multiple_of(b * zrows * rt, zrows * rt)
            cp = pltpu.make_async_copy(zero_ref, xs_ref.at[pl.ds(row0, zrows * rt)], sem)
            cp.start()
            cp.wait()
            return 0

        lax.fori_loop(pend_ref[MOE_EXPERTS - 1] // zrows, xs_ref.shape[0] // (zrows * rt), zero_tail, 0)

    @pl.when(first)
    def _():
        _to_row_tiles(h_ref, _moe_input(x_ref[0], nw_ref, mod_ref), (0,))

    def start(t, _):
        for k in range(MOE_TOPK):
            _row_copy(h_ref.at[slot], xs_ref, sem, t, dest_ref[k, t], rt).start(priority=k)
        return 0

    lax.fori_loop(0, tm, start, 0, unroll=DMA_ISSUE_UNROLL)

    @pl.when(step + 1 < pl.num_programs(0) * pl.num_programs(1))
    def _():
        _to_row_tiles(h_ref, _moe_input(nx_ref[0], nw_ref, nmod_ref), (1 - slot,))

    for k in range(MOE_TOPK):
        pltpu.make_async_copy(h_ref.at[slot], xs_ref.at[pl.ds(0, tm * rt)], sem).wait()


def _dispatch(x, norm_w, mod, dest, pend, padded, n_rows):
    bsz, s, d = x.shape
    rt = d // LANES
    tm = min(DISPATCH_ROWS, s)
    nt = s // tm
    last = bsz * nt - 1

    def nxt(b, i):
        f = jnp.minimum(b * nt + i + 1, last)
        return f // nt, f % nt

    return pl.pallas_call(
        functools.partial(_dispatch_kernel, rt=rt),
        grid=(bsz, nt),
        in_specs=[pl.BlockSpec(memory_space=pltpu.SMEM),
                  pl.BlockSpec(memory_space=pltpu.SMEM),
                  pl.BlockSpec((MOE_TOPK, tm), lambda b, i: (0, b * nt + i), memory_space=pltpu.SMEM),
                  pl.BlockSpec((1, tm, d), lambda b, i: (b, i, 0)),
                  pl.BlockSpec((1, 6, d), lambda b, i: (b, 0, 0)),
                  pl.BlockSpec((1, tm, d), lambda b, i: (*nxt(b, i), 0)),
                  pl.BlockSpec((1, 6, d), lambda b, i: (nxt(b, i)[0], 0, 0)),
                  pl.BlockSpec((1, d), lambda b, i: (0, 0))],
        out_specs=pl.BlockSpec(memory_space=pl.ANY),
        out_shape=jax.ShapeDtypeStruct((n_rows * rt, LANES), F32),
        scratch_shapes=[pltpu.VMEM((2, tm * rt, LANES), F32), pltpu.VMEM((MOE_BLOCK_ROWS * rt, LANES), F32),
                        pltpu.SemaphoreType.DMA(())],
        compiler_params=_cparams("arbitrary", "arbitrary"),
        name="moe_dispatch",
    )(pend, padded, dest, x, mod, x, mod, norm_w.reshape(1, d))


def _expert_kernel(blk_e_ref, n_used_ref, next_e_ref, slot_ref, x_ref, w1_ref, w3_ref, w2_ref, o_ref,
                   w1f_ref, w3f_ref, w2f_ref, w1b_ref, w3b_ref, w2b_ref, sem, *, layer):
    i = pl.program_id(0)
    used = i < n_used_ref[0]
    e = blk_e_ref[i]
    new_expert = used & ((i == 0) | (e != blk_e_ref[jnp.maximum(i - 1, 0)]))
    slot = slot_ref[e]

    def fetch(expert, s):
        return [pltpu.make_async_copy(w_ref.at[layer, expert], buf_ref.at[s], sem.at[s])
                for w_ref, buf_ref in ((w1_ref, w1f_ref), (w3_ref, w3f_ref), (w2_ref, w2f_ref))]

    @pl.when(used & (i == 0))
    def _():
        for cp in fetch(e, slot):
            cp.start()

    @pl.when(new_expert)
    def _():
        nxt = next_e_ref[e]

        @pl.when(nxt >= 0)
        def _():
            for cp in fetch(nxt, 1 - slot):
                cp.start()

        for cp in fetch(e, slot):
            cp.wait()
        w1b_ref[...] = w1f_ref[slot].astype(BF16)
        w3b_ref[...] = w3f_ref[slot].astype(BF16)
        w2b_ref[...] = w2f_ref[slot].astype(BF16)

    @pl.when(used)
    def _():
        d = w1b_ref.shape[0]
        rows = x_ref.shape[0] * LANES // d
        x = _from_row_tiles(x_ref, rows, d).astype(BF16)
        a = jnp.dot(x, w1b_ref[...], preferred_element_type=F32)
        g = jnp.dot(x, w3b_ref[...], preferred_element_type=F32)
        y = jnp.dot((a * jax.nn.sigmoid(a) * g).astype(BF16), w2b_ref[...], preferred_element_type=F32)
        _to_row_tiles(o_ref, y)

    @pl.when(jnp.logical_not(used))
    def _():
        o_ref[...] = jnp.zeros(o_ref.shape, F32)


def _expert_ffn(xs, blk_e, n_used, next_e, slot_e, w1, w3, w2, layer):
    d, hid = w1.shape[2], w1.shape[3]
    rt = d // LANES
    n_rows = xs.shape[0] // rt
    tm = MOE_BLOCK_ROWS
    row = lambda i, be, nu, ne, sl: (jnp.minimum(i, nu[0] - 1), 0)
    return pl.pallas_call(
        functools.partial(_expert_kernel, layer=layer),
        grid_spec=pltpu.PrefetchScalarGridSpec(
            num_scalar_prefetch=4,
            grid=(n_rows // tm,),
            in_specs=[pl.BlockSpec((tm * rt, LANES), row),
                      pl.BlockSpec(memory_space=pl.ANY),
                      pl.BlockSpec(memory_space=pl.ANY),
                      pl.BlockSpec(memory_space=pl.ANY)],
            out_specs=pl.BlockSpec((tm * rt, LANES), lambda i, be, nu, ne, sl: (i, 0)),
            scratch_shapes=[pltpu.VMEM((2, d, hid), F32), pltpu.VMEM((2, d, hid), F32),
                            pltpu.VMEM((2, hid, d), F32),
                            pltpu.VMEM((d, hid), BF16), pltpu.VMEM((d, hid), BF16),
                            pltpu.VMEM((hid, d), BF16), pltpu.SemaphoreType.DMA((2,))]),
        out_shape=jax.ShapeDtypeStruct((n_rows * rt, LANES), F32),
        compiler_params=_cparams("arbitrary"),
        name="moe_expert_ffn",
    )(blk_e, n_used, next_e, slot_e, xs, w1, w3, w2)


def _combine_kernel(dest_ref, next_dest_ref, ys_ref, gate_ref, x_ref, mod_ref, *refs, outs):
    buf_ref, sem = refs[-2:]
    o_ref = refs[-3 - len(outs)]
    tm, d = x_ref.shape[1], x_ref.shape[2]
    rt = d // LANES
    step = pl.program_id(0) * pl.num_programs(1) + pl.program_id(1)
    n_steps = pl.num_programs(0) * pl.num_programs(1)
    slot = step % 2

    def gather(rows_ref, s):
        def start(t, _):
            for k in range(MOE_TOPK):
                _row_copy(ys_ref, buf_ref.at[s, k], sem.at[s], rows_ref[k, t], t, rt).start(priority=k)
            return 0
        lax.fori_loop(0, tm, start, 0, unroll=DMA_ISSUE_UNROLL)

    @pl.when(step == 0)
    def _():
        gather(dest_ref, 0)

    @pl.when(step + 1 < n_steps)
    def _():
        gather(next_dest_ref, 1 - slot)

    for k in range(MOE_TOPK):
        pltpu.make_async_copy(ys_ref.at[pl.ds(0, tm * rt)], buf_ref.at[slot, k], sem.at[slot]).wait()
    g = gate_ref[...]
    y = (g[:, 0:1] * _from_row_tiles(buf_ref, tm, d, (slot, 0))
         + g[:, 1:2] * _from_row_tiles(buf_ref, tm, d, (slot, 1)))
    x = x_ref[0] + mod_ref[0, 5:6, :] * y
    o_ref[0] = x
    if outs:
        nw_ref, next_mod_ref, w_ref = refs[0:3]
        n_norm = len(refs) - 6 - len(outs)
        _project_rows(x, nw_ref, next_mod_ref, w_ref, refs[3:3 + n_norm], refs[-2 - len(outs):-2], outs)


def _combine(ys, dest, gates, x, mod, proj=None):
    bsz, s, d = x.shape
    tm = min(COMBINE_ROWS, s)
    nt = s // tm
    last = bsz * nt - 1
    tile = lambda b, i: (b, i, 0)
    const = lambda b, i: (0, 0)
    in_specs = [pl.BlockSpec((MOE_TOPK, tm), lambda b, i: (0, b * nt + i), memory_space=pltpu.SMEM),
                pl.BlockSpec((MOE_TOPK, tm), lambda b, i: (0, jnp.minimum(b * nt + i + 1, last)),
                             memory_space=pltpu.SMEM),
                pl.BlockSpec(memory_space=pl.ANY),
                pl.BlockSpec((tm, LANES), lambda b, i: (b * nt + i, 0)),
                pl.BlockSpec((1, tm, d), tile),
                pl.BlockSpec((1, 6, d), lambda b, i: (b, 0, 0))]
    args = [dest, dest, ys, gates, x, mod]
    out_specs = [pl.BlockSpec((1, tm, d), tile)]
    out_shape = [jax.ShapeDtypeStruct((bsz, s, d), F32)]
    outs = ()
    if proj is not None:
        norm_w, next_mod, w_bf16, outs, dtypes, head_norms = proj
        in_specs += [pl.BlockSpec((1, d), const), pl.BlockSpec((1, 6, d), lambda b, i: (b, 0, 0)),
                     pl.BlockSpec(w_bf16.shape, const)]
        in_specs += [pl.BlockSpec((1, hn.shape[0]), const) for hn in head_norms]
        args += [norm_w.reshape(1, d), next_mod, w_bf16] + [hn.reshape(1, -1) for hn in head_norms]
        out_specs += [pl.BlockSpec((1, tm, o[0]), tile) for o in outs]
        out_shape += [jax.ShapeDtypeStruct((bsz, s, o[0]), dt) for o, dt in zip(outs, dtypes)]
    res = pl.pallas_call(
        functools.partial(_combine_kernel, outs=tuple(outs)),
        grid=(bsz, nt),
        in_specs=in_specs,
        out_specs=out_specs,
        out_shape=out_shape,
        scratch_shapes=[pltpu.VMEM((2, MOE_TOPK, tm * d // LANES, LANES), F32), pltpu.SemaphoreType.DMA((2,))],
        compiler_params=_cparams("arbitrary", "arbitrary"),
        name="moe_combine" if proj is None else "moe_combine_project",
    )(*args)
    return res[0] if proj is None else res


def _mixer_out_moe_residual(acts, w_outs, x, norm_w, mod, w_group, b_group, w_expert, b_expert, w1, w3, w2, layer,
                            next_proj):
    bsz, s, d = x.shape
    n = bsz * s
    tm = MOE_BLOCK_ROWS
    wr_t = jnp.zeros((ROUTER_LOGIT_ROWS, d), F32).at[0:MOE_GROUPS].set(w_group.T).at[8:].set(w_expert.T)
    br = jnp.zeros((ROUTER_LOGIT_ROWS, 1), F32).at[0:MOE_GROUPS, 0].set(b_group).at[8:, 0].set(b_expert)
    x_mid, ids, gates, rank, counts = _out_project_route(acts, w_outs, x, mod, norm_w, wr_t.astype(BF16), br)
    counts = counts[:, 0]
    padded = ((counts + tm - 1) // tm) * tm
    pend = jnp.cumsum(padded).astype(jnp.int32)
    start = pend - padded
    n_rows = ((n * MOE_TOPK + tm - 1) // tm) * tm + MOE_EXPERTS * tm
    blk_start = jnp.arange(n_rows // tm, dtype=jnp.int32) * tm
    blk_e = jnp.minimum(jnp.sum((pend[None, :] <= blk_start[:, None]).astype(jnp.int32), axis=1),
                        MOE_EXPERTS - 1)
    n_used = pend[-1:] // tm
    owns = padded > 0
    eidx = jnp.arange(MOE_EXPERTS, dtype=jnp.int32)
    later = owns[None, :] & (eidx[None, :] > eidx[:, None])
    next_e = jnp.where(jnp.any(later, axis=1), jnp.argmax(later, axis=1), -1).astype(jnp.int32)
    slot_e = ((jnp.cumsum(owns.astype(jnp.int32)) - 1) % 2).astype(jnp.int32)
    dest = _dest_rows(ids, rank, start.reshape(MOE_EXPERTS, 1))
    xs = _dispatch(x_mid, norm_w, mod, dest, pend, padded, n_rows)
    ys = _expert_ffn(xs, blk_e, n_used, next_e, slot_e, w1, w3, w2, layer)
    return _combine(ys, dest, gates, x_mid, mod, next_proj)


def _even_proj_spec(w_in, q_norm, k_norm):
    d, n_in = w_in.shape
    aq = A_HEADS * A_HEAD_DIM
    akv = A_KV_HEADS * A_HEAD_DIM
    bk = B_HEADS * B_KEY_DIM
    bv = B_HEADS * B_VAL_DIM
    w_pad = jnp.zeros((d, n_in - B_GATE_RANK + LANES), F32).at[:, :n_in].set(w_in).astype(BF16)
    outs = ((aq, A_HEAD_DIM, 0, A_HEAD_DIM ** -0.5 * LOG2E), (akv, A_HEAD_DIM, 1, 1.0), (akv, 0, 0, 1.0),
            (2 * bk, 0, 0, 1.0), (bv, 0, 0, 1.0), (bv, 0, 0, 1.0), (LANES, 0, 0, 1.0))
    dtypes = (BF16, BF16, BF16, F32, BF16, F32, F32)
    return w_pad, outs, dtypes, (q_norm, k_norm)


def _odd_proj_spec(w_in, q_norm, k_norm):
    mix = C_HEADS * C_HEAD_DIM
    outs = ((mix, C_HEAD_DIM, 0, C_HEAD_DIM ** -0.5 * LOG2E), (mix, C_HEAD_DIM, 1, 1.0), (mix, 0, 0, 1.0))
    return w_in.astype(BF16), outs, (BF16, BF16, BF16), (q_norm, k_norm)


def _even_mixer(groups, w_out, sinks, gate_up, gate_bias, out_norm, band_bias):
    qa, ka, va, qk, vb, rb, ab = groups
    aq = A_HEADS * A_HEAD_DIM
    bk = B_HEADS * B_KEY_DIM
    oa = _swa_attention(qa, ka, va, sinks, band_bias)
    gu_pad = jnp.zeros((LANES, bk), F32).at[:B_GATE_RANK].set(gate_up).astype(BF16)
    ob = _gla(qk, vb, rb, ab, gu_pad, gate_bias, out_norm)
    w_out = w_out.astype(BF16)
    return [oa, ob], [w_out[:aq], w_out[aq:]]


def _odd_mixer(groups, w_out, bias_tiles):
    q, k, v = groups
    return [_moba(q, k, v, bias_tiles)], [w_out.astype(BF16)]


def kernel(x, c, rel_bias, ada_w, ada_b, norm1_w, norm2_w, even_w_in, even_w_out, a_q_norm, a_k_norm, a_sinks, b_gate_up, b_gate_bias, b_out_norm, odd_w_in, odd_w_out, c_q_norm, c_k_norm, moe_w_group, moe_b_group, moe_w_expert, moe_b_expert, moe_w1, moe_w3, moe_w2):
    depth = ada_w.shape[0]
    bsz, _, d = x.shape
    moba_bias, band_bias = _bias_tiles(rel_bias)
    mod_all = _adaln(c, ada_w, ada_b).reshape(depth, bsz, 6, d)

    def proj_spec(layer):
        j = layer // 2
        if layer % 2 == 0:
            return _even_proj_spec(even_w_in[j], a_q_norm[j], a_k_norm[j])
        return _odd_proj_spec(odd_w_in[j], c_q_norm[j], c_k_norm[j])

    w_in, outs, dtypes, head_norms = proj_spec(0)
    groups = _norm_mod_project(x, norm1_w[0], mod_all[0], w_in, outs, dtypes, head_norms)
    for layer in range(depth):
        mod = mod_all[layer]
        j = layer // 2
        if layer % 2 == 0:
            acts, w_outs = _even_mixer(groups, even_w_out[j], a_sinks[j], b_gate_up[j], b_gate_bias[j],
                                       b_out_norm[j], band_bias)
        else:
            acts, w_outs = _odd_mixer(groups, odd_w_out[j], moba_bias)
        next_proj = None
        if layer + 1 < depth:
            w_in, outs, dtypes, head_norms = proj_spec(layer + 1)
            next_proj = (norm1_w[layer + 1], mod_all[layer + 1], w_in, outs, dtypes, head_norms)
        res = _mixer_out_moe_residual(acts, w_outs, x, norm2_w[layer], mod, moe_w_group[layer],
                                      moe_b_group[layer], moe_w_expert[layer], moe_b_expert[layer],
                                      moe_w1, moe_w3, moe_w2, layer, next_proj)
        x, groups = (res, None) if next_proj is None else (res[0], res[1:])
    return x
```

```python
import functools
import math

import jax
import jax.numpy as jnp
import numpy as np
from jax import lax
from jax.experimental import pallas as pl
from jax.experimental.pallas import tpu as pltpu

RMS_EPS = 1e-6
A_HEADS, A_KV_HEADS, A_HEAD_DIM, A_WINDOW = 8, 2, 64, 128
B_HEADS, B_KEY_DIM, B_VAL_DIM, B_GATE_RANK, B_GATE_TAU, B_CHUNK = 4, 64, 128, 16, 16.0, 64
C_HEADS, C_HEAD_DIM, C_BLOCK, C_TOPK = 8, 128, 256, 3
REL_BUCKETS, REL_MAX_DIST, REL_HEADS = 32, 128, 8
MOE_GROUPS, MOE_EXPERTS_PER_GROUP, MOE_TOPK = 4, 8, 2
MOE_EXPERTS = MOE_GROUPS * MOE_EXPERTS_PER_GROUP

LANES = 128
V7X_VMEM_LIMIT_BYTES = 56 * 1024 * 1024
NEG_BIG = -1e30
INT32_MIN = -2 ** 31
PROJ_ROWS = 512
GLA_ROWS = 1024
ROUTER_ROWS = 1024
MOE_BLOCK_ROWS = 512
DISPATCH_ROWS = 1024
COMBINE_ROWS = 512
DMA_ISSUE_UNROLL = 8
ROUTER_LOGIT_ROWS = 8 + MOE_EXPERTS
MOBA_GROUP = 2
MOBA_HEADS_PER_STEP = 4
LOG2E = math.log2(math.e)
SWA_VT_ROWS = A_HEAD_DIM + 16
MOBA_VT_ROWS = C_HEAD_DIM + 16

F32 = jnp.float32
BF16 = jnp.bfloat16


def _cparams(*sem):
    return pltpu.CompilerParams(dimension_semantics=sem, vmem_limit_bytes=V7X_VMEM_LIMIT_BYTES)


def _rel_bucket_np(dist):
    exact = REL_BUCKETS // 2
    d = np.maximum(dist, 0)
    logd = np.log(np.maximum(d, 1).astype(np.float64) / exact) / math.log(REL_MAX_DIST / exact)
    far = np.minimum(exact + (logd * (REL_BUCKETS - exact)).astype(np.int64), REL_BUCKETS - 1)
    return np.where(d < exact, d, far).astype(np.int32)


def _bucket_tiles():
    li = np.arange(C_BLOCK)
    own = _rel_bucket_np(li[None, :] - li[:, None])
    prev = _rel_bucket_np(li[None, :] - li[:, None] + C_BLOCK)
    band = _rel_bucket_np(np.arange(A_WINDOW)[None, :] + A_WINDOW - np.arange(2 * A_WINDOW)[:, None])
    return np.concatenate([own, prev], axis=0), band


def _rms_rows(x, w):
    return x * lax.rsqrt(jnp.mean(x * x, axis=-1, keepdims=True) + RMS_EPS) * w


def _dot(a, b):
    return jnp.dot(a.astype(BF16), b.astype(BF16), preferred_element_type=F32)


def _dot_nt(a, b):
    return lax.dot_general(a.astype(BF16), b.astype(BF16), (((1,), (1,)), ((), ())),
                           preferred_element_type=F32)


def _order_key(x):
    bits = lax.bitcast_convert_type(x, jnp.int32)
    return jnp.where(bits < 0, bits ^ jnp.int32(0x7FFFFFFF), bits)


def _to_row_tiles(ref, x, lead=()):
    rows, d = x.shape
    chunks = d // LANES
    for c in range(chunks):
        ref[lead + (pl.ds(c, rows, stride=chunks), slice(None))] = x[:, c * LANES:(c + 1) * LANES]


def _from_row_tiles(ref, rows, d, lead=()):
    chunks = d // LANES
    return jnp.concatenate([ref[lead + (pl.ds(c, rows, stride=chunks), slice(None))] for c in range(chunks)],
                           axis=1)


def _dot_tn(a, b):
    return lax.dot_general(a.astype(BF16), b.astype(BF16), (((0,), (0,)), ((), ())),
                           preferred_element_type=F32)


def _bias_kernel(rb_ref, bkt_ref, o_ref, *, relative_to_last):
    h = pl.program_id(0)
    bkt = bkt_ref[...]
    acc = jnp.zeros(bkt.shape, F32)
    for b in range(REL_BUCKETS):
        acc = jnp.where(bkt == b, rb_ref[b, h], acc)
    if relative_to_last:
        acc = acc - rb_ref[REL_BUCKETS - 1, h]
    o_ref[0] = acc * LOG2E


def _bias_tile(rel_bias, bkt, relative_to_last):
    rows, cols = bkt.shape
    return pl.pallas_call(
        functools.partial(_bias_kernel, relative_to_last=relative_to_last),
        grid=(REL_HEADS,),
        in_specs=[pl.BlockSpec(memory_space=pltpu.SMEM),
                  pl.BlockSpec((rows, cols), lambda h: (0, 0))],
        out_specs=pl.BlockSpec((1, rows, cols), lambda h: (h, 0, 0)),
        out_shape=jax.ShapeDtypeStruct((REL_HEADS, rows, cols), F32),
        compiler_params=_cparams("arbitrary"),
        name="rel_bias_tiles",
    )(rel_bias, jnp.asarray(bkt))


def _bias_tiles(rel_bias):
    moba_bkt, band_bkt = _bucket_tiles()
    return _bias_tile(rel_bias, moba_bkt, True), _bias_tile(rel_bias, band_bkt, False)


def _adaln_kernel(c_ref, w_ref, b_ref, o_ref):
    c = c_ref[...]
    cond = c * jax.nn.sigmoid(c)
    o_ref[0] = _dot(cond, w_ref[0]) + b_ref[0]


def _adaln(c, ada_w, ada_b):
    depth, d, n6 = ada_w.shape
    bsz = c.shape[0]
    tn = 1536 if n6 % 1536 == 0 else n6
    return pl.pallas_call(
        _adaln_kernel,
        grid=(depth, n6 // tn),
        in_specs=[pl.BlockSpec((bsz, d), lambda l, j: (0, 0)),
                  pl.BlockSpec((1, d, tn), lambda l, j: (l, 0, j)),
                  pl.BlockSpec((1, 1, tn), lambda l, j: (l, 0, j))],
        out_specs=pl.BlockSpec((1, bsz, tn), lambda l, j: (l, 0, j)),
        out_shape=jax.ShapeDtypeStruct((depth, bsz, n6), F32),
        compiler_params=_cparams("arbitrary", "arbitrary"),
        name="adaln_mod",
    )(c, ada_w, ada_b.reshape(depth, 1, n6))


def _head_rms(y, w_ref, hd, post_scale):
    cols = y.shape[1]
    lane = lax.broadcasted_iota(jnp.int32, (1, LANES), 1)
    w = w_ref[...] * post_scale
    out = []
    for c in range(cols // LANES):
        t = y[:, c * LANES:(c + 1) * LANES]
        sq = t * t
        if hd == LANES:
            r = lax.rsqrt(jnp.sum(sq, axis=-1, keepdims=True) * (1.0 / hd) + RMS_EPS)
            out.append(t * r * w)
        else:
            low = lane < hd
            s_lo = jnp.sum(jnp.where(low, sq, 0.0), axis=-1, keepdims=True)
            s_hi = jnp.sum(jnp.where(low, 0.0, sq), axis=-1, keepdims=True)
            r = lax.rsqrt(jnp.where(low, s_lo, s_hi) * (1.0 / hd) + RMS_EPS)
            out.append(t * r * jnp.concatenate([w, w], axis=1))
    return jnp.concatenate(out, axis=1) if len(out) > 1 else out[0]


def _project_rows(x, nw_ref, mod_ref, w_ref, norm_refs, o_refs, outs):
    h = _rms_rows(x, nw_ref[...])
    h = h * (1.0 + mod_ref[0, 1:2, :]) + mod_ref[0, 0:1, :]
    y = _dot(h, w_ref[...])
    off = 0
    for o_ref, (wd, hd, widx, post) in zip(o_refs, outs):
        t = y[:, off:off + wd]
        if hd:
            t = _head_rms(t, norm_refs[widx], hd, post)
        o_ref[0] = t.astype(o_ref.dtype)
        off += wd


def _proj_kernel(x_ref, nw_ref, mod_ref, w_ref, *refs, outs):
    n_norm = len(refs) - len(outs)
    _project_rows(x_ref[0], nw_ref, mod_ref, w_ref, refs[:n_norm], refs[n_norm:], outs)


def _norm_mod_project(x, norm_w, mod, w_bf16, outs, dtypes, head_norms):
    bsz, s, d = x.shape
    tm = min(PROJ_ROWS, s)
    n = w_bf16.shape[1]
    widths = [o[0] for o in outs]
    assert sum(widths) == n and all(wd % LANES == 0 for wd in widths)
    return pl.pallas_call(
        functools.partial(_proj_kernel, outs=tuple(outs)),
        grid=(bsz, s // tm),
        in_specs=[pl.BlockSpec((1, tm, d), lambda b, i: (b, i, 0)),
                  pl.BlockSpec((1, d), lambda b, i: (0, 0)),
                  pl.BlockSpec((1, 6, d), lambda b, i: (b, 0, 0)),
                  pl.BlockSpec((d, n), lambda b, i: (0, 0))]
                 + [pl.BlockSpec((1, hn.shape[0]), lambda b, i: (0, 0)) for hn in head_norms],
        out_specs=[pl.BlockSpec((1, tm, wd), lambda b, i: (b, i, 0)) for wd in widths],
        out_shape=[jax.ShapeDtypeStruct((bsz, s, wd), dt) for wd, dt in zip(widths, dtypes)],
        compiler_params=_cparams("arbitrary", "arbitrary"),
        name="norm_mod_project",
    )(x, norm_w.reshape(1, d), mod, w_bf16, *[hn.reshape(1, -1) for hn in head_norms])


def _swa_kernel(sink_ref, q_ref, kc_ref, kp_ref, vc_ref, vp_ref, bias_ref, o_ref):
    n = pl.program_id(1)
    w = A_WINDOW
    hd = A_HEAD_DIM
    group = A_HEADS // A_KV_HEADS
    ones_rows = (lax.broadcasted_iota(jnp.int32, (SWA_VT_ROWS - hd, 2 * w), 0) == 0).astype(BF16)
    key = lax.broadcasted_iota(jnp.int32, (2 * w, group * w), 0)
    qry = lax.broadcasted_iota(jnp.int32, (2 * w, group * w), 1) % w
    dist = qry + w - key
    in_window = (dist >= 0) & (dist < w)
    masks = [in_window & ((n > 0) | (key >= w)), in_window]
    head_of_lane = lax.broadcasted_iota(jnp.int32, (1, group * w), 1) // w
    k_bands = [jnp.concatenate([kp_ref[0], kc_ref[0, 0:w, :]], axis=0), kc_ref[0]]
    v_bands = [jnp.concatenate([vp_ref[0], vc_ref[0, 0:w, :]], axis=0), vc_ref[0]]
    v_ts = [v.astype(F32).T for v in v_bands]
    sinks = []
    for kv in range(A_KV_HEADS):
        h0 = kv * group
        sink = jnp.full((1, group * w), sink_ref[h0 + group - 1] * LOG2E, F32)
        for g in range(group - 2, -1, -1):
            sink = jnp.where(head_of_lane == g, sink_ref[h0 + g] * LOG2E, sink)
        sinks.append(sink)
    chains = [(blk, kv) for blk in range(2) for kv in range(A_KV_HEADS)]
    logits, vts = [], []
    for blk, kv in chains:
        h0 = kv * group
        k_g = k_bands[blk][:, kv * hd:(kv + 1) * hd]
        vts.append(jnp.concatenate([v_ts[blk][kv * hd:(kv + 1) * hd, :].astype(BF16), ones_rows], axis=0))
        q_g = jnp.concatenate([q_ref[0, blk * w:(blk + 1) * w, (h0 + g) * hd:(h0 + g + 1) * hd]
                               for g in range(group)], axis=0)
        bias = jnp.concatenate([bias_ref[h0 + g] for g in range(group)], axis=1)
        logits.append(jnp.where(masks[blk], _dot_nt(k_g, q_g) + bias, NEG_BIG))
    maxima = [jnp.maximum(jnp.max(lg, axis=0, keepdims=True), sinks[kv]) for lg, (_, kv) in zip(logits, chains)]
    accs = [jnp.dot(vt_g, jnp.exp2(lg - m).astype(BF16), preferred_element_type=F32)
            for vt_g, lg, m in zip(vts, logits, maxima)]
    for (blk, kv), acc, m in zip(chains, accs, maxima):
        h0 = kv * group
        o_t = acc[0:hd] / (acc[hd:hd + 1] + jnp.exp2(sinks[kv] - m))
        for pair in range(group // 2):
            two = jnp.concatenate([o_t[:, (2 * pair) * w:(2 * pair + 1) * w],
                                   o_t[:, (2 * pair + 1) * w:(2 * pair + 2) * w]], axis=0)
            c0 = (h0 + 2 * pair) * hd
            o_ref[0, blk * w:(blk + 1) * w, c0:c0 + 2 * hd] = two.T.astype(o_ref.dtype)


def _swa_attention(qa, ka, va, sinks, band_bias):
    bsz, s, _ = qa.shape
    w = A_WINDOW
    assert s % (2 * w) == 0
    kvw = A_KV_HEADS * A_HEAD_DIM
    cur = lambda b, n: (b, n, 0)
    prev = lambda b, n: (b, jnp.maximum(2 * n - 1, 0), 0)
    return pl.pallas_call(
        _swa_kernel,
        grid=(bsz, s // (2 * w)),
        in_specs=[pl.BlockSpec(memory_space=pltpu.SMEM),
                  pl.BlockSpec((1, 2 * w, A_HEADS * A_HEAD_DIM), cur),
                  pl.BlockSpec((1, 2 * w, kvw), cur),
                  pl.BlockSpec((1, w, kvw), prev),
                  pl.BlockSpec((1, 2 * w, kvw), cur),
                  pl.BlockSpec((1, w, kvw), prev),
                  pl.BlockSpec((REL_HEADS, 2 * w, w), lambda b, n: (0, 0, 0))],
        out_specs=pl.BlockSpec((1, 2 * w, A_HEADS * A_HEAD_DIM), cur),
        out_shape=jax.ShapeDtypeStruct((bsz, s, A_HEADS * A_HEAD_DIM), BF16),
        compiler_params=_cparams("arbitrary", "arbitrary"),
        name="swa_sink_attention",
    )(sinks, qa, ka, ka, va, va, band_bias)


def _gla_kernel(qk_ref, v_ref, r_ref, ab_ref, gu_ref, gb_ref, on_ref, o_ref, state_ref):
    s_idx = pl.program_id(1)
    c_len = B_CHUNK
    dk, dv = B_KEY_DIM, B_VAL_DIM
    hk = B_HEADS * dk

    @pl.when(s_idx == 0)
    def _():
        state_ref[...] = jnp.zeros(state_ref.shape, F32)

    z = _dot(ab_ref[0], gu_ref[...]) + gb_ref[...]
    log_a = (jnp.minimum(z, 0.0) - jnp.log(1.0 + jnp.exp(-jnp.abs(z)))) / B_GATE_TAU
    rows = qk_ref.shape[1]
    chunks = range(rows // c_len)
    heads = range(B_HEADS)
    ri = lax.broadcasted_iota(jnp.int32, (c_len, c_len), 0)
    ci = lax.broadcasted_iota(jnp.int32, (c_len, c_len), 1)
    tril = ri >= ci
    tri = tril.astype(BF16)
    g_hi = log_a.astype(BF16)
    g_lo = (log_a - g_hi.astype(F32)).astype(BF16)
    b_c = [jnp.dot(tri, g_hi[c * c_len:(c + 1) * c_len], preferred_element_type=F32)
           + jnp.dot(tri, g_lo[c * c_len:(c + 1) * c_len], preferred_element_type=F32) for c in chunks]
    b = jnp.concatenate(b_c, axis=0)
    last_c = [bc[c_len - 1:c_len, :] for bc in b_c]
    b_last = jnp.concatenate([jnp.broadcast_to(l, (c_len, hk)) for l in last_c], axis=0)
    q = qk_ref[0, :, 0:hk] * (dk ** -0.5)
    k = qk_ref[0, :, hk:2 * hk]
    q_dec = (q * jnp.exp(b)).astype(BF16)
    k_dec = (k * jnp.exp(-b)).astype(BF16)
    k_upd = (k * jnp.exp(b_last - b)).astype(BF16)
    v = v_ref[0]

    def rows_of(x, c, cols):
        return x[c * c_len:(c + 1) * c_len, cols]

    att = [[jnp.where(tril, _dot_nt(rows_of(q_dec, c, slice(h * dk, (h + 1) * dk)),
                                    rows_of(k_dec, c, slice(h * dk, (h + 1) * dk))), 0.0).astype(BF16)
            for h in heads] for c in chunks]
    upd = [[_dot_tn(rows_of(v, c, slice(h * dv, (h + 1) * dv)), rows_of(k_upd, c, slice(h * dk, (h + 1) * dk)))
            for h in heads] for c in chunks]
    state_in = []
    st = [state_ref[h] for h in heads]
    for c in chunks:
        state_in.append(st)
        decay = jnp.exp(last_c[c])
        st = [st[h] * decay[:, h * dk:(h + 1) * dk] + upd[c][h] for h in heads]
    for h in heads:
        state_ref[h] = st[h]
    for c in chunks:
        outs = []
        for h in heads:
            o_h = (_dot(att[c][h], rows_of(v, c, slice(h * dv, (h + 1) * dv)))
                   + _dot_nt(rows_of(q_dec, c, slice(h * dk, (h + 1) * dk)), state_in[c][h]))
            o_h = _rms_rows(o_h, on_ref[...])
            r_h = r_ref[0, c * c_len:(c + 1) * c_len, h * dv:(h + 1) * dv]
            outs.append(o_h * (r_h * jax.nn.sigmoid(r_h)))
        o_ref[0, c * c_len:(c + 1) * c_len, :] = jnp.concatenate(outs, axis=-1).astype(o_ref.dtype)


def _gla(qk, vb, rb, ab, gate_up_pad, gate_bias, out_norm):
    bsz, s, _ = qk.shape
    tm = min(GLA_ROWS, s)
    hv = B_HEADS * B_VAL_DIM
    hk = B_HEADS * B_KEY_DIM
    blk = lambda b, i: (b, i, 0)
    const = lambda b, i: (0, 0)
    return pl.pallas_call(
        _gla_kernel,
        grid=(bsz, s // tm),
        in_specs=[pl.BlockSpec((1, tm, 2 * hk), blk),
                  pl.BlockSpec((1, tm, hv), blk),
                  pl.BlockSpec((1, tm, hv), blk),
                  pl.BlockSpec((1, tm, LANES), blk),
                  pl.BlockSpec((LANES, hk), const),
                  pl.BlockSpec((1, hk), const),
                  pl.BlockSpec((1, B_VAL_DIM), const)],
        out_specs=pl.BlockSpec((1, tm, hv), blk),
        out_shape=jax.ShapeDtypeStruct((bsz, s, hv), BF16),
        scratch_shapes=[pltpu.VMEM((B_HEADS, B_VAL_DIM, B_KEY_DIM), F32)],
        compiler_params=_cparams("arbitrary", "arbitrary"),
        name="gated_linear_attention",
    )(qk, vb, rb, ab, gate_up_pad, gate_bias.reshape(1, hk), out_norm.reshape(1, -1))


def _moba_kernel(q_ref, k_ref, v_ref, bias_ref, o_ref, vt_ref, kmean_ref, pick_ref):
    i = pl.program_id(2)
    blk = C_BLOCK
    hd = C_HEAD_DIM
    nb = k_ref.shape[1] // blk
    grp = MOBA_GROUP
    heads = range(MOBA_HEADS_PER_STEP)

    def lanes(hh):
        return slice(hh * hd, (hh + 1) * hd)

    @pl.when(i == 0)
    def _():
        ones_rows = (lax.broadcasted_iota(jnp.int32, (MOBA_VT_ROWS - hd, blk), 0) == 0).astype(BF16)
        for hh in heads:
            kn = k_ref[0, :, lanes(hh)].astype(F32)
            kmean_ref[hh] = jnp.mean(kn.reshape(nb, blk, hd), axis=1)
            for j in range(nb):
                vt = v_ref[0, j * blk:(j + 1) * blk, lanes(hh)].astype(F32).T.astype(BF16)
                vt_ref[hh, j] = jnp.concatenate([vt, ones_rows], axis=0)

    def logits(hh, q_h, j):
        j0 = pl.multiple_of(j * blk, blk)
        return _dot_nt(k_ref[0, pl.ds(j0, blk), lanes(hh)], q_h)

    def col_max(lg, picked):
        return jnp.where(picked, jnp.max(lg, axis=0, keepdims=True), NEG_BIG)

    def weighted_v(hh, j, lg, m, picked):
        p = jnp.exp2(lg - m).astype(BF16)
        return jnp.where(picked, jnp.dot(vt_ref[hh, j], p, preferred_element_type=F32), 0.0)

    def softmax_step(ms, accs, tiles):
        m_new = []
        for hh in heads:
            m_h = ms[hh]
            for _, lg, picked in tiles[hh]:
                m_h = jnp.maximum(m_h, col_max(lg, picked))
            m_new.append(m_h)
        out = []
        for hh in heads:
            acc = accs[hh] * jnp.exp2(ms[hh] - m_new[hh])
            for j, lg, picked in tiles[hh]:
                acc = acc + weighted_v(hh, j, lg, m_new[hh], picked)
            out.append(acc)
        return tuple(m_new), tuple(out)

    jrow = lax.broadcasted_iota(jnp.int32, (nb, blk), 0)
    key = lax.broadcasted_iota(jnp.int32, (blk, blk), 0)
    qry = lax.broadcasted_iota(jnp.int32, (blk, blk), 1)
    j_prev = jnp.maximum(i - 1, 0)
    qs = []
    for hh in heads:
        q_h = q_ref[0, :, lanes(hh)]
        qs.append(q_h)
        key_h = jnp.where(jrow < i, _order_key(_dot_nt(kmean_ref[hh], q_h)), INT32_MIN)
        chosen = jnp.zeros((nb, blk), F32)
        for _ in range(C_TOPK):
            best = jnp.max(key_h, axis=0, keepdims=True)
            arg = jnp.min(jnp.where(key_h == best, jrow, nb), axis=0, keepdims=True)
            hit = jrow == arg
            chosen = jnp.where(hit, 1.0, chosen)
            key_h = jnp.where(hit, INT32_MIN, key_h)
        pick_ref[hh] = jnp.where(jrow < i, chosen, 0.0)

    first = []
    for hh in heads:
        lg_own = jnp.where(key <= qry, logits(hh, qs[hh], i) + bias_ref[hh, 0:blk, :], NEG_BIG)
        lg_prev = logits(hh, qs[hh], j_prev) + bias_ref[hh, blk:2 * blk, :]
        pick_prev = pick_ref[hh, pl.ds(j_prev, 1), :] > 0.5
        first.append([(i, lg_own, True), (j_prev, lg_prev, pick_prev)])
    start = (tuple(jnp.full((1, blk), NEG_BIG, F32) for _ in heads),
             tuple(jnp.zeros((MOBA_VT_ROWS, blk), F32) for _ in heads))
    carry0 = softmax_step(start[0], start[1], first)

    n_far = jnp.maximum(i - 1, 0)

    def body(g, carry):
        tiles = []
        for hh in heads:
            row = []
            for u in range(grp):
                j = g * grp + u
                jc = jnp.minimum(j, nb - 1)
                picked = (pick_ref[hh, pl.ds(jc, 1), :] > 0.5) & (j < n_far)
                row.append((jc, logits(hh, qs[hh], jc), picked))
            tiles.append(row)
        return softmax_step(carry[0], carry[1], tiles)

    _, accs = lax.fori_loop(0, (n_far + grp - 1) // grp, body, carry0)
    for hh in heads:
        o_ref[0, :, lanes(hh)] = (accs[hh][0:hd] / accs[hh][hd:hd + 1]).T.astype(o_ref.dtype)


def _moba(q, k, v, bias_tiles):
    bsz, s, _ = q.shape
    blk, hd = C_BLOCK, C_HEAD_DIM
    assert s % blk == 0
    nb = s // blk
    hb = MOBA_HEADS_PER_STEP
    assert C_HEADS % hb == 0
    return pl.pallas_call(
        _moba_kernel,
        grid=(bsz, C_HEADS // hb, nb),
        in_specs=[pl.BlockSpec((1, blk, hb * hd), lambda b, h, i: (b, i, h)),
                  pl.BlockSpec((1, s, hb * hd), lambda b, h, i: (b, 0, h)),
                  pl.BlockSpec((1, s, hb * hd), lambda b, h, i: (b, 0, h)),
                  pl.BlockSpec((hb, 2 * blk, blk), lambda b, h, i: (h, 0, 0))],
        out_specs=pl.BlockSpec((1, blk, hb * hd), lambda b, h, i: (b, i, h)),
        out_shape=jax.ShapeDtypeStruct((bsz, s, C_HEADS * hd), BF16),
        scratch_shapes=[pltpu.VMEM((hb, nb, MOBA_VT_ROWS, blk), BF16),
                        pltpu.VMEM((hb, nb, hd), F32), pltpu.VMEM((hb, nb, blk), F32)],
        compiler_params=_cparams("arbitrary", "arbitrary", "arbitrary"),
        name="moba_attention",
    )(q, k, v, bias_tiles)


def _moe_input(x, nw_ref, mod_ref):
    h = _rms_rows(x, nw_ref[...])
    return h * (1.0 + mod_ref[0, 4:5, :]) + mod_ref[0, 3:4, :]


def _out_route_kernel(*refs, n_in):
    a_refs = refs[:n_in]
    w_refs = refs[n_in:2 * n_in]
    (x_ref, mod_ref, nw_ref, wr_ref, br_ref,
     xo_ref, ids_ref, gate_ref, rank_ref, cnt_ref, base_ref) = refs[2 * n_in:]
    first = (pl.program_id(0) == 0) & (pl.program_id(1) == 0)

    @pl.when(first)
    def _():
        base_ref[...] = jnp.zeros(base_ref.shape, F32)

    y = _dot(a_refs[0][0], w_refs[0][...])
    for a_ref, w_ref in zip(a_refs[1:], w_refs[1:]):
        y = y + _dot(a_ref[0], w_ref[...])
    x = x_ref[0] + mod_ref[0, 2:3, :] * y
    xo_ref[0] = x

    tm = x.shape[0]
    lt = _dot_nt(wr_ref[...], _moe_input(x, nw_ref, mod_ref)) + br_ref[...]
    g = [lt[r:r + 1, :] for r in range(MOE_GROUPS)]
    gmax = functools.reduce(jnp.maximum, g)
    gsel = jnp.full(gmax.shape, MOE_GROUPS - 1, jnp.int32)
    for r in range(MOE_GROUPS - 2, -1, -1):
        gsel = jnp.where(g[r] == gmax, r, gsel)
    p_g = 1.0 / functools.reduce(jnp.add, [jnp.exp(gr - gmax) for gr in g])
    epg = MOE_EXPERTS_PER_GROUP
    e_in = lt[8 + (MOE_GROUPS - 1) * epg:8 + MOE_GROUPS * epg, :]
    for r in range(MOE_GROUPS - 2, -1, -1):
        e_in = jnp.where(gsel == r, lt[8 + r * epg:8 + (r + 1) * epg, :], e_in)
    sub = lax.broadcasted_iota(jnp.int32, (epg, tm), 0)
    key1 = _order_key(e_in)
    i1 = jnp.min(jnp.where(key1 == jnp.max(key1, axis=0, keepdims=True), sub, epg), axis=0, keepdims=True)
    v1 = jnp.max(e_in, axis=0, keepdims=True)
    key2 = jnp.where(sub == i1, INT32_MIN, key1)
    i2 = jnp.min(jnp.where(key2 == jnp.max(key2, axis=0, keepdims=True), sub, epg), axis=0, keepdims=True)
    v2 = jnp.max(jnp.where(sub == i1, -jnp.inf, e_in), axis=0, keepdims=True)
    t = jnp.exp(v2 - v1)
    w1 = p_g / (1.0 + t)
    w2 = p_g * t / (1.0 + t)
    id1 = gsel * epg + i1
    id2 = gsel * epg + i2
    ids_ref[...] = jnp.concatenate([id1, id2], axis=0)
    gate_ref[...] = jnp.concatenate([w1, w2, jnp.zeros((LANES - 2, tm), F32)], axis=0).T

    eidx = lax.broadcasted_iota(jnp.int32, (MOE_EXPERTS, tm), 0)
    oh1 = eidx == id1
    oh2 = eidx == id2
    onehot = (oh1 | oh2).astype(BF16)
    tr = lax.broadcasted_iota(jnp.int32, (tm, tm), 0)
    tc = lax.broadcasted_iota(jnp.int32, (tm, tm), 1)
    before = (tr < tc).astype(BF16)
    prefix = jnp.dot(onehot, before, preferred_element_type=F32) + base_ref[...]
    r1 = jnp.sum(jnp.where(oh1, prefix, 0.0), axis=0, keepdims=True)
    r2 = jnp.sum(jnp.where(oh2, prefix, 0.0), axis=0, keepdims=True)
    rank_ref[...] = jnp.concatenate([r1, r2], axis=0).astype(jnp.int32)
    base_ref[...] = base_ref[...] + jnp.sum(onehot.astype(F32), axis=1, keepdims=True)
    cnt_ref[...] = base_ref[...].astype(jnp.int32)


def _out_project_route(acts, weights, x, mod, norm_w, wr_t, br):
    bsz, s, d = x.shape
    tm = min(ROUTER_ROWS, s)
    n = bsz * s
    nt = s // tm
    n_in = len(acts)
    tok = lambda b, i: (0, b * nt + i)
    const = lambda b, i: (0, 0)
    in_specs = [pl.BlockSpec((1, tm, a.shape[2]), lambda b, i: (b, i, 0)) for a in acts]
    in_specs += [pl.BlockSpec(w.shape, const) for w in weights]
    in_specs += [pl.BlockSpec((1, tm, d), lambda b, i: (b, i, 0)),
                 pl.BlockSpec((1, 6, d), lambda b, i: (b, 0, 0)),
                 pl.BlockSpec((1, d), const),
                 pl.BlockSpec((ROUTER_LOGIT_ROWS, d), const),
                 pl.BlockSpec((ROUTER_LOGIT_ROWS, 1), const)]
    return pl.pallas_call(
        functools.partial(_out_route_kernel, n_in=n_in),
        grid=(bsz, nt),
        in_specs=in_specs,
        out_specs=[pl.BlockSpec((1, tm, d), lambda b, i: (b, i, 0)),
                   pl.BlockSpec((MOE_TOPK, tm), tok),
                   pl.BlockSpec((tm, LANES), lambda b, i: (b * nt + i, 0)),
                   pl.BlockSpec((MOE_TOPK, tm), tok),
                   pl.BlockSpec((MOE_EXPERTS, 1), const)],
        out_shape=[jax.ShapeDtypeStruct((bsz, s, d), F32),
                   jax.ShapeDtypeStruct((MOE_TOPK, n), jnp.int32),
                   jax.ShapeDtypeStruct((n, LANES), F32),
                   jax.ShapeDtypeStruct((MOE_TOPK, n), jnp.int32),
                   jax.ShapeDtypeStruct((MOE_EXPERTS, 1), jnp.int32)],
        scratch_shapes=[pltpu.VMEM((MOE_EXPERTS, 1), F32)],
        compiler_params=_cparams("arbitrary", "arbitrary"),
        name="out_project_route",
    )(*acts, *weights, x, mod, norm_w.reshape(1, d), wr_t, br)


def _dest_kernel(ids_ref, rank_ref, start_ref, o_ref):
    ids = ids_ref[...]
    tm = ids.shape[1]
    eidx = lax.broadcasted_iota(jnp.int32, (MOE_EXPERTS, tm), 0)
    rows = []
    for k in range(MOE_TOPK):
        base = jnp.sum(jnp.where(eidx == ids[k:k + 1, :], start_ref[...], 0), axis=0, keepdims=True)
        rows.append(base + rank_ref[k:k + 1, :])
    o_ref[...] = jnp.concatenate(rows, axis=0)


def _dest_rows(ids, rank, start):
    n = ids.shape[1]
    tm = min(2048, n)
    return pl.pallas_call(
        _dest_kernel,
        grid=(n // tm,),
        in_specs=[pl.BlockSpec((MOE_TOPK, tm), lambda i: (0, i)),
                  pl.BlockSpec((MOE_TOPK, tm), lambda i: (0, i)),
                  pl.BlockSpec((MOE_EXPERTS, 1), lambda i: (0, 0))],
        out_specs=pl.BlockSpec((MOE_TOPK, tm), lambda i: (0, i)),
        out_shape=jax.ShapeDtypeStruct((MOE_TOPK, n), jnp.int32),
        compiler_params=_cparams("arbitrary"),
        name="moe_dest_rows",
    )(ids, rank, start)


def _row_copy(src, dst, sem, src_row, dst_row, rt):
    return pltpu.make_async_copy(src.at[pl.ds(pl.multiple_of(src_row * rt, rt), rt)],
                                 dst.at[pl.ds(pl.multiple_of(dst_row * rt, rt), rt)], sem)


def _dispatch_kernel(pend_ref, padded_ref, dest_ref, x_ref, mod_ref, nx_ref, nmod_ref, nw_ref, xs_ref,
                     h_ref, zero_ref, sem, *, rt):
    tm = x_ref.shape[1]
    zrows = zero_ref.shape[0] // rt
    step = pl.program_id(0) * pl.num_programs(1) + pl.program_id(1)
    first = step == 0
    slot = step % 2

    @pl.when(first)
    def _():
        zero_ref[...] = jnp.zeros(zero_ref.shape, F32)
        for e in range(MOE_EXPERTS):
            @pl.when(padded_ref[e] > 0)
            def _():
                row0 = pl.multiple_of((pend_ref[e] - zrows) * rt, zrows * rt)
                cp = pltpu.make_async_copy(zero_ref, xs_ref.at[pl.ds(row0, zrows * rt)], sem)
                cp.start()
                cp.wait()

        def zero_tail(b, _):
            row0 = pl.multiple_of(b * zrows * rt, zrows * rt)
            cp = pltpu.make_async_copy(zero_ref, xs_ref.at[pl.ds(row0, zrows * rt)], sem)
            cp.start()
            cp.wait()
            return 0

        lax.fori_loop(pend_ref[MOE_EXPERTS - 1] // zrows, xs_ref.shape[0] // (zrows * rt), zero_tail, 0)

    @pl.when(first)
    def _():
        _to_row_tiles(h_ref, _moe_input(x_ref[0], nw_ref, mod_ref), (0,))

    def start(t, _):
        for k in range(MOE_TOPK):
            _row_copy(h_ref.at[slot], xs_ref, sem, t, dest_ref[k, t], rt).start(priority=k)
        return 0

    lax.fori_loop(0, tm, start, 0, unroll=DMA_ISSUE_UNROLL)

    @pl.when(step + 1 < pl.num_programs(0) * pl.num_programs(1))
    def _():
        _to_row_tiles(h_ref, _moe_input(nx_ref[0], nw_ref, nmod_ref), (1 - slot,))

    for k in range(MOE_TOPK):
        pltpu.make_async_copy(h_ref.at[slot], xs_ref.at[pl.ds(0, tm * rt)], sem).wait()


def _dispatch(x, norm_w, mod, dest, pend, padded, n_rows):
    bsz, s, d = x.shape
    rt = d // LANES
    tm = min(DISPATCH_ROWS, s)
    nt = s // tm
    last = bsz * nt - 1

    def nxt(b, i):
        f = jnp.minimum(b * nt + i + 1, last)
        return f // nt, f % nt

    return pl.pallas_call(
        functools.partial(_dispatch_kernel, rt=rt),
        grid=(bsz, nt),
        in_specs=[pl.BlockSpec(memory_space=pltpu.SMEM),
                  pl.BlockSpec(memory_space=pltpu.SMEM),
                  pl.BlockSpec((MOE_TOPK, tm), lambda b, i: (0, b * nt + i), memory_space=pltpu.SMEM),
                  pl.BlockSpec((1, tm, d), lambda b, i: (b, i, 0)),
                  pl.BlockSpec((1, 6, d), lambda b, i: (b, 0, 0)),
                  pl.BlockSpec((1, tm, d), lambda b, i: (*nxt(b, i), 0)),
                  pl.BlockSpec((1, 6, d), lambda b, i: (nxt(b, i)[0], 0, 0)),
                  pl.BlockSpec((1, d), lambda b, i: (0, 0))],
        out_specs=pl.BlockSpec(memory_space=pl.ANY),
        out_shape=jax.ShapeDtypeStruct((n_rows * rt, LANES), F32),
        scratch_shapes=[pltpu.VMEM((2, tm * rt, LANES), F32), pltpu.VMEM((MOE_BLOCK_ROWS * rt, LANES), F32),
                        pltpu.SemaphoreType.DMA(())],
        compiler_params=_cparams("arbitrary", "arbitrary"),
        name="moe_dispatch",
    )(pend, padded, dest, x, mod, x, mod, norm_w.reshape(1, d))


def _expert_kernel(blk_e_ref, n_used_ref, next_e_ref, slot_ref, x_ref, w1_ref, w3_ref, w2_ref, o_ref,
                   w1f_ref, w3f_ref, w2f_ref, w1b_ref, w3b_ref, w2b_ref, sem, *, layer):
    i = pl.program_id(0)
    used = i < n_used_ref[0]
    e = blk_e_ref[i]
    new_expert = used & ((i == 0) | (e != blk_e_ref[jnp.maximum(i - 1, 0)]))
    slot = slot_ref[e]

    def fetch(expert, s):
        return [pltpu.make_async_copy(w_ref.at[layer, expert], buf_ref.at[s], sem.at[s])
                for w_ref, buf_ref in ((w1_ref, w1f_ref), (w3_ref, w3f_ref), (w2_ref, w2f_ref))]

    @pl.when(used & (i == 0))
    def _():
        for cp in fetch(e, slot):
            cp.start()

    @pl.when(new_expert)
    def _():
        nxt = next_e_ref[e]

        @pl.when(nxt >= 0)
        def _():
            for cp in fetch(nxt, 1 - slot):
                cp.start()

        for cp in fetch(e, slot):
            cp.wait()
        w1b_ref[...] = w1f_ref[slot].astype(BF16)
        w3b_ref[...] = w3f_ref[slot].astype(BF16)
        w2b_ref[...] = w2f_ref[slot].astype(BF16)

    @pl.when(used)
    def _():
        d = w1b_ref.shape[0]
        rows = x_ref.shape[0] * LANES // d
        x = _from_row_tiles(x_ref, rows, d).astype(BF16)
        a = jnp.dot(x, w1b_ref[...], preferred_element_type=F32)
        g = jnp.dot(x, w3b_ref[...], preferred_element_type=F32)
        y = jnp.dot((a * jax.nn.sigmoid(a) * g).astype(BF16), w2b_ref[...], preferred_element_type=F32)
        _to_row_tiles(o_ref, y)

    @pl.when(jnp.logical_not(used))
    def _():
        o_ref[...] = jnp.zeros(o_ref.shape, F32)


def _expert_ffn(xs, blk_e, n_used, next_e, slot_e, w1, w3, w2, layer):
    d, hid = w1.shape[2], w1.shape[3]
    rt = d // LANES
    n_rows = xs.shape[0] // rt
    tm = MOE_BLOCK_ROWS
    row = lambda i, be, nu, ne, sl: (jnp.minimum(i, nu[0] - 1), 0)
    return pl.pallas_call(
        functools.partial(_expert_kernel, layer=layer),
        grid_spec=pltpu.PrefetchScalarGridSpec(
            num_scalar_prefetch=4,
            grid=(n_rows // tm,),
            in_specs=[pl.BlockSpec((tm * rt, LANES), row),
                      pl.BlockSpec(memory_space=pl.ANY),
                      pl.BlockSpec(memory_space=pl.ANY),
                      pl.BlockSpec(memory_space=pl.ANY)],
            out_specs=pl.BlockSpec((tm * rt, LANES), lambda i, be, nu, ne, sl: (i, 0)),
            scratch_shapes=[pltpu.VMEM((2, d, hid), F32), pltpu.VMEM((2, d, hid), F32),
                            pltpu.VMEM((2, hid, d), F32),
                            pltpu.VMEM((d, hid), BF16), pltpu.VMEM((d, hid), BF16),
                            pltpu.VMEM((hid, d), BF16), pltpu.SemaphoreType.DMA((2,))]),
        out_shape=jax.ShapeDtypeStruct((n_rows * rt, LANES), F32),
        compiler_params=_cparams("arbitrary"),
        name="moe_expert_ffn",
    )(blk_e, n_used, next_e, slot_e, xs, w1, w3, w2)


def _combine_kernel(dest_ref, next_dest_ref, ys_ref, gate_ref, x_ref, mod_ref, *refs, outs):
    buf_ref, sem = refs[-2:]
    o_ref = refs[-3 - len(outs)]
    tm, d = x_ref.shape[1], x_ref.shape[2]
    rt = d // LANES
    step = pl.program_id(0) * pl.num_programs(1) + pl.program_id(1)
    n_steps = pl.num_programs(0) * pl.num_programs(1)
    slot = step % 2

    def gather(rows_ref, s):
        def start(t, _):
            for k in range(MOE_TOPK):
                _row_copy(ys_ref, buf_ref.at[s, k], sem.at[s], rows_ref[k, t], t, rt).start(priority=k)
            return 0
        lax.fori_loop(0, tm, start, 0, unroll=DMA_ISSUE_UNROLL)

    @pl.when(step == 0)
    def _():
        gather(dest_ref, 0)

    @pl.when(step + 1 < n_steps)
    def _():
        gather(next_dest_ref, 1 - slot)

    for k in range(MOE_TOPK):
        pltpu.make_async_copy(ys_ref.at[pl.ds(0, tm * rt)], buf_ref.at[slot, k], sem.at[slot]).wait()
    g = gate_ref[...]
    y = (g[:, 0:1] * _from_row_tiles(buf_ref, tm, d, (slot, 0))
         + g[:, 1:2] * _from_row_tiles(buf_ref, tm, d, (slot, 1)))
    x = x_ref[0] + mod_ref[0, 5:6, :] * y
    o_ref[0] = x
    if outs:
        nw_ref, next_mod_ref, w_ref = refs[0:3]
        n_norm = len(refs) - 6 - len(outs)
        _project_rows(x, nw_ref, next_mod_ref, w_ref, refs[3:3 + n_norm], refs[-2 - len(outs):-2], outs)


def _combine(ys, dest, gates, x, mod, proj=None):
    bsz, s, d = x.shape
    tm = min(COMBINE_ROWS, s)
    nt = s // tm
    last = bsz * nt - 1
    tile = lambda b, i: (b, i, 0)
    const = lambda b, i: (0, 0)
    in_specs = [pl.BlockSpec((MOE_TOPK, tm), lambda b, i: (0, b * nt + i), memory_space=pltpu.SMEM),
                pl.BlockSpec((MOE_TOPK, tm), lambda b, i: (0, jnp.minimum(b * nt + i + 1, last)),
                             memory_space=pltpu.SMEM),
                pl.BlockSpec(memory_space=pl.ANY),
                pl.BlockSpec((tm, LANES), lambda b, i: (b * nt + i, 0)),
                pl.BlockSpec((1, tm, d), tile),
                pl.BlockSpec((1, 6, d), lambda b, i: (b, 0, 0))]
    args = [dest, dest, ys, gates, x, mod]
    out_specs = [pl.BlockSpec((1, tm, d), tile)]
    out_shape = [jax.ShapeDtypeStruct((bsz, s, d), F32)]
    outs = ()
    if proj is not None:
        norm_w, next_mod, w_bf16, outs, dtypes, head_norms = proj
        in_specs += [pl.BlockSpec((1, d), const), pl.BlockSpec((1, 6, d), lambda b, i: (b, 0, 0)),
                     pl.BlockSpec(w_bf16.shape, const)]
        in_specs += [pl.BlockSpec((1, hn.shape[0]), const) for hn in head_norms]
        args += [norm_w.reshape(1, d), next_mod, w_bf16] + [hn.reshape(1, -1) for hn in head_norms]
        out_specs += [pl.BlockSpec((1, tm, o[0]), tile) for o in outs]
        out_shape += [jax.ShapeDtypeStruct((bsz, s, o[0]), dt) for o, dt in zip(outs, dtypes)]
    res = pl.pallas_call(
        functools.partial(_combine_kernel, outs=tuple(outs)),
        grid=(bsz, nt),
        in_specs=in_specs,
        out_specs=out_specs,
        out_shape=out_shape,
        scratch_shapes=[pltpu.VMEM((2, MOE_TOPK, tm * d // LANES, LANES), F32), pltpu.SemaphoreType.DMA((2,))],
        compiler_params=_cparams("arbitrary", "arbitrary"),
        name="moe_combine" if proj is None else "moe_combine_project",
    )(*args)
    return res[0] if proj is None else res


def _mixer_out_moe_residual(acts, w_outs, x, norm_w, mod, w_group, b_group, w_expert, b_expert, w1, w3, w2, layer,
                            next_proj):
    bsz, s, d = x.shape
    n = bsz * s
    tm = MOE_BLOCK_ROWS
    wr_t = jnp.zeros((ROUTER_LOGIT_ROWS, d), F32).at[0:MOE_GROUPS].set(w_group.T).at[8:].set(w_expert.T)
    br = jnp.zeros((ROUTER_LOGIT_ROWS, 1), F32).at[0:MOE_GROUPS, 0].set(b_group).at[8:, 0].set(b_expert)
    x_mid, ids, gates, rank, counts = _out_project_route(acts, w_outs, x, mod, norm_w, wr_t.astype(BF16), br)
    counts = counts[:, 0]
    padded = ((counts + tm - 1) // tm) * tm
    pend = jnp.cumsum(padded).astype(jnp.int32)
    start = pend - padded
    n_rows = ((n * MOE_TOPK + tm - 1) // tm) * tm + MOE_EXPERTS * tm
    blk_start = jnp.arange(n_rows // tm, dtype=jnp.int32) * tm
    blk_e = jnp.minimum(jnp.sum((pend[None, :] <= blk_start[:, None]).astype(jnp.int32), axis=1),
                        MOE_EXPERTS - 1)
    n_used = pend[-1:] // tm
    owns = padded > 0
    eidx = jnp.arange(MOE_EXPERTS, dtype=jnp.int32)
    later = owns[None, :] & (eidx[None, :] > eidx[:, None])
    next_e = jnp.where(jnp.any(later, axis=1), jnp.argmax(later, axis=1), -1).astype(jnp.int32)
    slot_e = ((jnp.cumsum(owns.astype(jnp.int32)) - 1) % 2).astype(jnp.int32)
    dest = _dest_rows(ids, rank, start.reshape(MOE_EXPERTS, 1))
    xs = _dispatch(x_mid, norm_w, mod, dest, pend, padded, n_rows)
    ys = _expert_ffn(xs, blk_e, n_used, next_e, slot_e, w1, w3, w2, layer)
    return _combine(ys, dest, gates, x_mid, mod, next_proj)


def _even_proj_spec(w_in, q_norm, k_norm):
    d, n_in = w_in.shape
    aq = A_HEADS * A_HEAD_DIM
    akv = A_KV_HEADS * A_HEAD_DIM
    bk = B_HEADS * B_KEY_DIM
    bv = B_HEADS * B_VAL_DIM
    w_pad = jnp.zeros((d, n_in - B_GATE_RANK + LANES), F32).at[:, :n_in].set(w_in).astype(BF16)
    outs = ((aq, A_HEAD_DIM, 0, A_HEAD_DIM ** -0.5 * LOG2E), (akv, A_HEAD_DIM, 1, 1.0), (akv, 0, 0, 1.0),
            (2 * bk, 0, 0, 1.0), (bv, 0, 0, 1.0), (bv, 0, 0, 1.0), (LANES, 0, 0, 1.0))
    dtypes = (BF16, BF16, BF16, F32, BF16, F32, F32)
    return w_pad, outs, dtypes, (q_norm, k_norm)


def _odd_proj_spec(w_in, q_norm, k_norm):
    mix = C_HEADS * C_HEAD_DIM
    outs = ((mix, C_HEAD_DIM, 0, C_HEAD_DIM ** -0.5 * LOG2E), (mix, C_HEAD_DIM, 1, 1.0), (mix, 0, 0, 1.0))
    return w_in.astype(BF16), outs, (BF16, BF16, BF16), (q_norm, k_norm)


def _even_mixer(groups, w_out, sinks, gate_up, gate_bias, out_norm, band_bias):
    qa, ka, va, qk, vb, rb, ab = groups
    aq = A_HEADS * A_HEAD_DIM
    bk = B_HEADS * B_KEY_DIM
    oa = _swa_attention(qa, ka, va, sinks, band_bias)
    gu_pad = jnp.zeros((LANES, bk), F32).at[:B_GATE_RANK].set(gate_up).astype(BF16)
    ob = _gla(qk, vb, rb, ab, gu_pad, gate_bias, out_norm)
    w_out = w_out.astype(BF16)
    return [oa, ob], [w_out[:aq], w_out[aq:]]


def _odd_mixer(groups, w_out, bias_tiles):
    q, k, v = groups
    return [_moba(q, k, v, bias_tiles)], [w_out.astype(BF16)]


def kernel(x, c, rel_bias, ada_w, ada_b, norm1_w, norm2_w, even_w_in, even_w_out, a_q_norm, a_k_norm, a_sinks, b_gate_up, b_gate_bias, b_out_norm, odd_w_in, odd_w_out, c_q_norm, c_k_norm, moe_w_group, moe_b_group, moe_w_expert, moe_b_expert, moe_w1, moe_w3, moe_w2):
    depth = ada_w.shape[0]
    bsz, _, d = x.shape
    moba_bias, band_bias = _bias_tiles(rel_bias)
    mod_all = _adaln(c, ada_w, ada_b).reshape(depth, bsz, 6, d)

    def proj_spec(layer):
        j = layer // 2
        if layer % 2 == 0:
            return _even_proj_spec(even_w_in[j], a_q_norm[j], a_k_norm[j])
        return _odd_proj_spec(odd_w_in[j], c_q_norm[j], c_k_norm[j])

    w_in, outs, dtypes, head_norms = proj_spec(0)
    groups = _norm_mod_project(x, norm1_w[0], mod_all[0], w_in, outs, dtypes, head_norms)
    for layer in range(depth):
        mod = mod_all[layer]
        j = layer // 2
        if layer % 2 == 0:
            acts, w_outs = _even_mixer(groups, even_w_out[j], a_sinks[j], b_gate_up[j], b_gate_bias[j],
                                       b_out_norm[j], band_bias)
        else:
            acts, w_outs = _odd_mixer(groups, odd_w_out[j], moba_bias)
        next_proj = None
        if layer + 1 < depth:
            w_in, outs, dtypes, head_norms = proj_spec(layer + 1)
            next_proj = (norm1_w[layer + 1], mod_all[layer + 1], w_in, outs, dtypes, head_norms)
        res = _mixer_out_moe_residual(acts, w_outs, x, norm2_w[layer], mod, moe_w_group[layer],
                                      moe_b_group[layer], moe_w_expert[layer], moe_b_expert[layer],
                                      moe_w1, moe_w3, moe_w2, layer, next_proj)
        x, groups = (res, None) if next_proj is None else (res[0], res[1:])
    return x
```
